```python
import math
import jax
import jax.numpy as jnp
from jax import lax
import numpy as np

D_MODEL = 1024
BATCH = 4
SEQ = 8192
DEPTH = 2

HEAD_DIM = 64
GRID_W = 64
NEG_INF = -1e30

MLA_HEADS = 4
MLA_NOPE = 64
MLA_ROPE = 32
MLA_V = 64
Q_LORA = 256
KV_LORA = 128
ROPE_THETA = 10000.0
Q_BLOCK = 128

SWA_HEADS = 4
SWA_KV_HEADS = 2
SWA_HALF = 128

NA_HEADS = 4
NA_KH_MAX = 8
NA_KW = 16

DIL_CONFIGS = ((128, 1), (512, 4), (2048, 16))
DIL_GROUPS = 3
DIL_HEADS = 4

T5_BUCKETS = 32
T5_MAX_DIST = 1024
T5_HEADS = SWA_HEADS + DIL_GROUPS * DIL_HEADS

N_BRANCH = 4
BRANCH_W = 256
D_FF = 4 * D_MODEL
PLE_DIM = 256

SPLIT_WIDTHS = (Q_LORA, KV_LORA, MLA_ROPE,
                SWA_HEADS * HEAD_DIM, SWA_KV_HEADS * HEAD_DIM, SWA_KV_HEADS * HEAD_DIM,
                3 * NA_HEADS * HEAD_DIM, 3 * DIL_GROUPS * DIL_HEADS * HEAD_DIM,
                N_BRANCH * D_MODEL)
IN_COLS = sum(SPLIT_WIDTHS)

kernel_name = 'hybrid_gated_parallel_encoder'


def layer_norm(x, g, b, eps=1e-5):
    xf = x.astype(jnp.float32)
    mu = xf.mean(-1, keepdims=True)
    var = jnp.square(xf - mu).mean(-1, keepdims=True)
    y = (xf - mu) * lax.rsqrt(var + eps) * g.astype(jnp.float32) + b.astype(jnp.float32)
    return y.astype(x.dtype)


def rms_norm(x, g, eps=1e-6):
    xf = x.astype(jnp.float32)
    y = xf * lax.rsqrt(jnp.mean(jnp.square(xf), -1, keepdims=True) + eps) * g.astype(jnp.float32)
    return y.astype(x.dtype)


def rope(x, positions):
    half = x.shape[-1] // 2
    inv = ROPE_THETA ** (-jnp.arange(half, dtype=jnp.float32) / half)
    ang = positions.astype(jnp.float32)[:, None] * inv[None, :]
    shape = (ang.shape[0],) + (1,) * (x.ndim - 3) + (half,)
    cos, sin = jnp.cos(ang).reshape(shape), jnp.sin(ang).reshape(shape)
    x1, x2 = x[..., :half].astype(jnp.float32), x[..., half:].astype(jnp.float32)
    return jnp.concatenate([x1 * cos - x2 * sin, x1 * sin + x2 * cos], -1).astype(x.dtype)


def t5_bucket(rel):
    nb = T5_BUCKETS // 2
    max_exact = nb // 2
    n = jnp.abs(rel)
    large = max_exact + (jnp.log(jnp.maximum(n, 1).astype(jnp.float32) / max_exact)
                         / math.log(T5_MAX_DIST / max_exact) * (nb - max_exact)).astype(jnp.int32)
    large = jnp.minimum(large, nb - 1)
    return jnp.where(rel > 0, nb, 0) + jnp.where(n < max_exact, n, large)


def band_offsets(half):
    return jnp.arange(3 * half)[None, :] - half - jnp.arange(half)[:, None]


def banded_attention(q, k, v, half, bias, sink=None):
    bsz, L, hk, g, dh = q.shape
    nb = -(-L // half)
    lp = nb * half
    q = jnp.pad(q, ((0, 0), (0, lp - L), (0, 0), (0, 0), (0, 0)))
    kv_pad = ((0, 0), (half, lp - L + half), (0, 0), (0, 0))

    def windows(t):
        tb = jnp.pad(t, kv_pad).reshape(bsz, nb + 2, half, hk, dh)
        return jnp.concatenate([tb[:, :-2], tb[:, 1:-1], tb[:, 2:]], axis=2)

    kw, vw = windows(k), windows(v)
    qb = q.reshape(bsz, nb, half, hk, g, dh)
    s = jnp.einsum('bnqhgd,bnkhd->bnhgqk', qb, kw).astype(jnp.float32) * dh ** -0.5
    s = s + bias.astype(jnp.float32)
    rel = band_offsets(half)
    kpos = jnp.arange(nb)[:, None] * half - half + jnp.arange(3 * half)[None, :]
    mask = (jnp.abs(rel) <= half)[None] & ((kpos >= 0) & (kpos < L))[:, None, :]
    s = jnp.where(mask[None, :, None, None], s, NEG_INF)
    m = s.max(-1)
    if sink is not None:
        sk = sink.astype(jnp.float32)[:, :, None]
        m = jnp.maximum(m, sk)
    e = jnp.exp(s - m[..., None])
    den = e.sum(-1)
    if sink is not None:
        den = den + jnp.exp(sk - m)
    pr = (e / den[..., None]).astype(v.dtype)
    o = jnp.einsum('bnhgqk,bnkhd->bnqhgd', pr, vw).reshape(bsz, lp, hk, g, dh)[:, :L]
    lse = (m + jnp.log(den)).transpose(0, 1, 4, 2, 3).reshape(bsz, lp, hk, g)[:, :L]
    return o, lse


def mla_attention(cq_in, ckv_in, kr_in, g_q, w_uq, g_kv, w_ukv, positions):
    bsz, S, _ = cq_in.shape
    q = (rms_norm(cq_in, g_q) @ w_uq).reshape(bsz, S, MLA_HEADS, MLA_NOPE + MLA_ROPE)
    q_nope, q_rope = q[..., :MLA_NOPE], rope(q[..., MLA_NOPE:], positions)
    kv = (rms_norm(ckv_in, g_kv) @ w_ukv).reshape(bsz, S, MLA_HEADS, MLA_NOPE + MLA_V)
    k_nope, v = kv[..., :MLA_NOPE], kv[..., MLA_NOPE:]
    k_rope = rope(kr_in, positions)
    scale = (MLA_NOPE + MLA_ROPE) ** -0.5
    nq = S // Q_BLOCK

    def to_blocks(t):
        return jnp.moveaxis(t.reshape(bsz, nq, Q_BLOCK, *t.shape[2:]), 1, 0)

    def block(args):
        qn, qr = args
        s = (jnp.einsum('bqhd,bkhd->bhqk', qn, k_nope)
             + jnp.einsum('bqhr,bkr->bhqk', qr, k_rope)).astype(jnp.float32) * scale
        pr = jax.nn.softmax(s, axis=-1).astype(v.dtype)
        return jnp.einsum('bhqk,bkhd->bqhd', pr, v)

    o = lax.map(block, (to_blocks(q_nope), to_blocks(q_rope)))
    return jnp.moveaxis(o, 0, 1).reshape(bsz, S, MLA_HEADS * MLA_V)


def neighborhood_attention(q, k, v, rpb):
    bsz, S, H, dh = q.shape
    rows = S // GRID_W
    kh = min(NA_KH_MAX, rows)
    q = q.reshape(bsz, rows, GRID_W, H, dh)
    k = k.reshape(bsz, rows, GRID_W, H, dh)
    v = v.reshape(bsz, rows, GRID_W, H, dh)
    r = jnp.arange(rows)
    row_idx = jnp.clip(r - kh // 2, 0, rows - kh)[:, None] + jnp.arange(kh)[None, :]
    kr = jnp.take(k, row_idx, axis=1)
    vr = jnp.take(v, row_idx, axis=1)
    c = jnp.arange(GRID_W)
    col_start = jnp.clip(c - NA_KW // 2, 0, GRID_W - NA_KW)
    col_ok = (c[None, :] >= col_start[:, None]) & (c[None, :] < col_start[:, None] + NA_KW)
    dr = row_idx - r[:, None] + (NA_KH_MAX - 1)
    dc = jnp.clip(c[None, :] - c[:, None], -(NA_KW - 1), NA_KW - 1) + (NA_KW - 1)
    bias = rpb[:, dr[:, None, :, None], dc[None, :, None, :]]
    s = jnp.einsum('brchd,brawhd->brhcaw', q, kr).astype(jnp.float32) * dh ** -0.5
    s = s + bias.astype(jnp.float32).transpose(1, 0, 2, 3, 4)[None]
    s = jnp.where(col_ok[:, None, :], s, NEG_INF)
    pr = jax.nn.softmax(s.reshape(bsz, rows, H, GRID_W, kh * GRID_W), axis=-1)
    pr = pr.reshape(s.shape).astype(v.dtype)
    o = jnp.einsum('brhcaw,brawhd->brchd', pr, vr)
    return o.reshape(bsz, S, H * dh)


def dilated_attention(q, k, v, rel_bias):
    bsz, S, _, H, dh = q.shape
    outs, lses = [], []
    for g, (window, r) in enumerate(DIL_CONFIGS):
        half = window // (2 * r)
        L = S // r

        def fold(t):
            return t.reshape(bsz, L, r, *t.shape[2:]).swapaxes(1, 2).reshape(bsz * r, L, *t.shape[2:])

        def unfold(t):
            return t.reshape(bsz, r, L, *t.shape[2:]).swapaxes(1, 2).reshape(bsz, S, *t.shape[2:])

        lo = SWA_HEADS + g * DIL_HEADS
        bias = rel_bias[t5_bucket(band_offsets(half) * r)][..., lo:lo + DIL_HEADS]
        bias = bias.transpose(2, 0, 1)[:, None]
        o, lse = banded_attention(fold(q[:, :, g])[:, :, :, None], fold(k[:, :, g]),
                                  fold(v[:, :, g]), half, bias)
        outs.append(unfold(o[:, :, :, 0]))
        lses.append(unfold(lse[:, :, :, 0]))
    w = jax.nn.softmax(jnp.stack(lses), axis=0)
    o = jnp.einsum('gbsh,gbshd->bshd', w, jnp.stack(outs).astype(jnp.float32))
    return o.reshape(bsz, S, H * dh).astype(q.dtype)


def setup_inputs(seed: int = 0) -> dict:
    key = jax.random.key(seed)
    ks = jax.random.split(key, 22)
    beta = (8 * DEPTH) ** -0.25

    def nrm(k, shape, scale):
        return jax.random.normal(k, shape, jnp.float32) * scale

    L = DEPTH
    return {
        'x': nrm(ks[0], (BATCH, SEQ, D_MODEL), 1.0),
        'p': nrm(ks[1], (DEPTH, BATCH, SEQ, PLE_DIM), 1.0),
        'ln_emb_g': 1.0 + nrm(ks[2], (D_MODEL,), 0.02),
        'ln_emb_b': nrm(ks[3], (D_MODEL,), 0.02),
        'rel_bias': nrm(ks[4], (T5_BUCKETS, T5_HEADS), 0.2),
        'w_in': nrm(ks[5], (L, D_MODEL, IN_COLS), D_MODEL ** -0.5),
        'mla_q_norm': 1.0 + nrm(ks[6], (L, Q_LORA), 0.02),
        'mla_w_uq': nrm(ks[7], (L, Q_LORA, MLA_HEADS * (MLA_NOPE + MLA_ROPE)), Q_LORA ** -0.5),
        'mla_kv_norm': 1.0 + nrm(ks[8], (L, KV_LORA), 0.02),
        'mla_w_ukv': nrm(ks[9], (L, KV_LORA, MLA_HEADS * (MLA_NOPE + MLA_V)), KV_LORA ** -0.5),
        'swa_sink': nrm(ks[10], (L, SWA_HEADS), 1.0),
        'na_rpb': nrm(ks[11], (L, NA_HEADS, 2 * NA_KH_MAX - 1, 2 * NA_KW - 1), 0.2),
        'w_branch': nrm(ks[12], (L, N_BRANCH, BRANCH_W, D_MODEL), BRANCH_W ** -0.5),
        'w_out': nrm(ks[13], (L, D_MODEL, D_MODEL), beta * D_MODEL ** -0.5),
        'ln1_g': 1.0 + nrm(ks[14], (L, D_MODEL), 0.02),
        'ln1_b': nrm(ks[15], (L, D_MODEL), 0.02),
        'w_ff1': nrm(ks[16], (L, D_MODEL, D_FF), beta * D_MODEL ** -0.5),
        'w_ff2': nrm(ks[17], (L, D_FF, D_MODEL), beta * D_FF ** -0.5),
        'w_ple': nrm(ks[18], (L, PLE_DIM, D_MODEL), beta * PLE_DIM ** -0.5),
        'w_ple_gate': nrm(ks[19], (L, D_MODEL, D_MODEL), D_MODEL ** -0.5),
        'ln2_g': 1.0 + nrm(ks[20], (L, D_MODEL), 0.02),
        'ln2_b': nrm(ks[21], (L, D_MODEL), 0.02),
    }


def reference(x, p, ln_emb_g, ln_emb_b, rel_bias, w_in, mla_q_norm, mla_w_uq, mla_kv_norm,
              mla_w_ukv, swa_sink, na_rpb, w_branch, w_out, ln1_g, ln1_b, w_ff1, w_ff2,
              w_ple, w_ple_gate, ln2_g, ln2_b):
    bsz, S, _ = x.shape
    positions = jnp.arange(S)
    alpha = (2 * DEPTH) ** 0.25
    offsets = [int(o) for o in np.cumsum(SPLIT_WIDTHS[:-1])]
    g_swa = SWA_HEADS // SWA_KV_HEADS
    bias_swa = rel_bias[t5_bucket(band_offsets(SWA_HALF))][..., :SWA_HEADS]
    bias_swa = bias_swa.transpose(2, 0, 1).reshape(SWA_KV_HEADS, g_swa, SWA_HALF, 3 * SWA_HALF)

    h = layer_norm(x, ln_emb_g, ln_emb_b)
    for i in range(DEPTH):
        z = h @ w_in[i]
        a_q, a_kv, a_kr, b_q, b_k, b_v, c_qkv, d_qkv, gates = jnp.split(z, offsets, axis=-1)
        y_a = mla_attention(a_q, a_kv, a_kr, mla_q_norm[i], mla_w_uq[i], mla_kv_norm[i],
                            mla_w_ukv[i], positions)
        o_b, _ = banded_attention(b_q.reshape(bsz, S, SWA_KV_HEADS, g_swa, HEAD_DIM),
                                  b_k.reshape(bsz, S, SWA_KV_HEADS, HEAD_DIM),
                                  b_v.reshape(bsz, S, SWA_KV_HEADS, HEAD_DIM),
                                  SWA_HALF, bias_swa, swa_sink[i].reshape(SWA_KV_HEADS, g_swa))
        y_b = o_b.reshape(bsz, S, SWA_HEADS * HEAD_DIM)
        c = c_qkv.reshape(bsz, S, 3, NA_HEADS, HEAD_DIM)
        y_c = neighborhood_attention(c[:, :, 0], c[:, :, 1], c[:, :, 2], na_rpb[i])
        d = d_qkv.reshape(bsz, S, 3, DIL_GROUPS, DIL_HEADS, HEAD_DIM)
        y_d = dilated_attention(d[:, :, 0], d[:, :, 1], d[:, :, 2], rel_bias)
        gate = jax.nn.sigmoid(gates.reshape(bsz, S, N_BRANCH, D_MODEL))
        branches = (y_a, y_b, y_c, y_d)
        merged = gate[:, :, 0] * (branches[0] @ w_branch[i, 0])
        for n in range(1, N_BRANCH):
            merged = merged + gate[:, :, n] * (branches[n] @ w_branch[i, n])
        h = layer_norm(alpha * h + merged @ w_out[i], ln1_g[i], ln1_b[i])
        ff = jnp.square(jax.nn.relu(h @ w_ff1[i])) @ w_ff2[i]
        ple = (p[i] @ w_ple[i]) * jax.nn.sigmoid(h @ w_ple_gate[i])
        h = layer_norm(alpha * h + ff + ple, ln2_g[i], ln2_b[i])
    return h
```

```python
import functools
import math

import jax
import jax.numpy as jnp
from jax import lax
from jax.experimental import pallas as pl
from jax.experimental.pallas import tpu as pltpu

F32 = jnp.float32
BF16 = jnp.bfloat16

D_MODEL = 1024
DEPTH = 2
HEAD_DIM = 64
GRID_W = 64
NEG_INF = -1e30

MLA_HEADS = 4
MLA_NOPE = 64
MLA_ROPE = 32
MLA_V = 64
Q_LORA = 256
KV_LORA = 128
ROPE_THETA = 10000.0

SWA_HEADS = 4
SWA_KV_HEADS = 2
SWA_HALF = 128

NA_HEADS = 4
NA_KH = 8
NA_KW = 16

DIL_CONFIGS = ((128, 1), (512, 4), (2048, 16))
DIL_GROUPS = 3
DIL_HEADS = 4

T5_BUCKETS = 32
T5_MAX_DIST = 1024

N_BRANCH = 4
BRANCH_W = 256
D_FF = 4 * D_MODEL
PLE_DIM = 256

ALPHA = (2 * DEPTH) ** 0.25

LANES = 128
HEADS = 4
HEAD_BLOCK = HEADS * HEAD_DIM

GATE_COLS = N_BRANCH * D_MODEL
A_COL = GATE_COLS
A_WIDTH = 512
B_BLK = (A_COL + A_WIDTH) // HEAD_BLOCK
C_BLK = B_BLK + 3
D_BLK = C_BLK + 3
Z_BLKS = D_BLK + 3 * DIL_GROUPS
Z_COLS = Z_BLKS * HEAD_BLOCK

MLA_HEAD_PAD = 128
MLA_W = MLA_HEADS * MLA_HEAD_PAD
LOG2E = math.log2(math.e)

VMEM_LIMIT = 56 * 1024 * 1024


def _cparams(sem):
    return pltpu.CompilerParams(dimension_semantics=sem, vmem_limit_bytes=VMEM_LIMIT)


def _layer_norm(u, g, b, eps=1e-5):
    mu = jnp.mean(u, axis=-1, keepdims=True)
    d = u - mu
    var = jnp.mean(d * d, axis=-1, keepdims=True)
    return d * lax.rsqrt(var + eps) * g + b


def _rms_norm(u, g, eps=1e-6):
    return u * lax.rsqrt(jnp.mean(u * u, axis=-1, keepdims=True) + eps) * g


def _ln_kernel(x_ref, g_ref, b_ref, h_ref, hb_ref):
    y = _layer_norm(x_ref[...], g_ref[...], b_ref[...])
    h_ref[...] = y
    hb_ref[...] = y.astype(BF16)


def _embed_layer_norm(x2, g, b, tm=512):
    m, d = x2.shape
    row = pl.BlockSpec((tm, d), lambda i: (i, 0))
    vec = pl.BlockSpec((1, d), lambda i: (0, 0))
    return pl.pallas_call(
        _ln_kernel,
        grid=(m // tm,),
        in_specs=[row, vec, vec],
        out_specs=[row, row],
        out_shape=[jax.ShapeDtypeStruct((m, d), F32), jax.ShapeDtypeStruct((m, d), BF16)],
        compiler_params=_cparams(("parallel",)),
        name="embed_ln",
    )(x2, g.reshape(1, d), b.reshape(1, d))


def _mm_kernel(a_ref, w_ref, o_ref):
    o_ref[...] = jnp.dot(a_ref[...], w_ref[...], preferred_element_type=F32).astype(o_ref.dtype)


def _matmul(a, w, tm, tn, out_dtype, name):
    m, k = a.shape
    n = w.shape[1]
    return pl.pallas_call(
        _mm_kernel,
        grid=(m // tm, n // tn),
        in_specs=[pl.BlockSpec((tm, k), lambda i, j: (i, 0)),
                  pl.BlockSpec((k, tn), lambda i, j: (0, j))],
        out_specs=pl.BlockSpec((tm, tn), lambda i, j: (i, j)),
        out_shape=jax.ShapeDtypeStruct((m, n), out_dtype),
        compiler_params=_cparams(("parallel", "arbitrary")),
        name=name,
    )(a, w)


def _prep_w_in(w):
    o = 0
    a = w[:, o:o + Q_LORA + KV_LORA + MLA_ROPE]
    o += Q_LORA + KV_LORA + MLA_ROPE
    bq = w[:, o:o + HEAD_BLOCK]
    o += HEAD_BLOCK
    kvw = SWA_KV_HEADS * HEAD_DIM
    bk = w[:, o:o + kvw]
    o += kvw
    bv = w[:, o:o + kvw]
    o += kvw
    c = w[:, o:o + 3 * HEAD_BLOCK]
    o += 3 * HEAD_BLOCK
    d = w[:, o:o + 3 * DIL_GROUPS * HEAD_BLOCK]
    o += 3 * DIL_GROUPS * HEAD_BLOCK
    gates = w[:, o:]
    rep = SWA_HEADS // SWA_KV_HEADS

    def expand(t):
        t = t.reshape(D_MODEL, SWA_KV_HEADS, 1, HEAD_DIM)
        return jnp.broadcast_to(t, (D_MODEL, SWA_KV_HEADS, rep, HEAD_DIM)).reshape(D_MODEL, HEAD_BLOCK)

    a = jnp.pad(a, ((0, 0), (0, A_WIDTH - a.shape[1])))
    d = d.reshape(D_MODEL, 3, DIL_GROUPS, HEAD_BLOCK).transpose(0, 2, 1, 3).reshape(D_MODEL, -1)
    return jnp.concatenate([gates, a, bq, expand(bk), expand(bv), c, d], axis=1).astype(BF16)


def _rope_tables(seq):
    half = MLA_ROPE // 2
    inv = ROPE_THETA ** (-jnp.arange(half, dtype=F32) / half)
    ang = jnp.arange(seq).astype(F32)[:, None] * inv[None, :]
    cos, sin = jnp.cos(ang), jnp.sin(ang)
    pad = MLA_HEAD_PAD - MLA_NOPE - MLA_ROPE
    cos_t = jnp.concatenate([jnp.ones((seq, MLA_NOPE), F32), cos, cos, jnp.zeros((seq, pad), F32)], axis=1)
    sin_t = jnp.concatenate([jnp.zeros((seq, MLA_NOPE), F32), sin, sin, jnp.zeros((seq, pad), F32)], axis=1)
    return cos_t, sin_t


def _rot_cols(w_rope):
    half = MLA_ROPE // 2
    return jnp.concatenate([-w_rope[..., half:], w_rope[..., :half]], axis=-1)


def _prep_mla_weights(w_uq, w_ukv):
    pad = MLA_HEAD_PAD - MLA_NOPE - MLA_ROPE
    wq = w_uq.reshape(Q_LORA, MLA_HEADS, MLA_NOPE + MLA_ROPE)
    wq_main = jnp.pad(wq, ((0, 0), (0, 0), (0, pad))).reshape(Q_LORA, MLA_W)
    wq_rot = jnp.concatenate([jnp.zeros((Q_LORA, MLA_HEADS, MLA_NOPE), F32), _rot_cols(wq[..., MLA_NOPE:]),
                              jnp.zeros((Q_LORA, MLA_HEADS, pad), F32)], axis=-1).reshape(Q_LORA, MLA_W)
    wkv = w_ukv.reshape(KV_LORA, MLA_HEADS, MLA_NOPE + MLA_V)
    wk_n = jnp.pad(wkv[..., :MLA_NOPE], ((0, 0), (0, 0), (0, MLA_HEAD_PAD - MLA_NOPE))).reshape(KV_LORA, MLA_W)
    wv = jnp.pad(wkv[..., MLA_NOPE:], ((0, 0), (0, 0), (0, MLA_HEAD_PAD - MLA_V))).reshape(KV_LORA, MLA_W)
    eye = jnp.eye(MLA_ROPE, dtype=F32)
    place = jnp.concatenate([jnp.zeros((MLA_ROPE, MLA_NOPE), F32), eye, jnp.zeros((MLA_ROPE, pad), F32)], axis=1)
    place_rot = jnp.concatenate([jnp.zeros((MLA_ROPE, MLA_NOPE), F32), _rot_cols(eye),
                                 jnp.zeros((MLA_ROPE, pad), F32)], axis=1)

    def lift(pm):
        pm = jnp.pad(pm, ((0, LANES - MLA_ROPE), (0, 0)))
        return jnp.tile(pm, (1, MLA_HEADS))

    wk_main = jnp.concatenate([wk_n, lift(place)], axis=0)
    wk_rot = jnp.concatenate([jnp.zeros((KV_LORA, MLA_W), F32), lift(place_rot)], axis=0)
    return tuple(t.astype(BF16) for t in (wq_main, wq_rot, wk_main, wk_rot, wv))


def _mla_prep_kernel(za_ref, cos_ref, sin_ref, gq_ref, gkv_ref, wq_ref, wqr_ref, wk_ref, wkr_ref, wv_ref,
                     q_ref, k_ref, v_ref):
    za = za_ref[...]
    cq = _rms_norm(za[:, :Q_LORA].astype(F32), gq_ref[...]).astype(BF16)
    ckv = _rms_norm(za[:, Q_LORA:Q_LORA + KV_LORA].astype(F32), gkv_ref[...]).astype(BF16)
    cos = jnp.concatenate([cos_ref[...]] * MLA_HEADS, axis=1)
    sin = jnp.concatenate([sin_ref[...]] * MLA_HEADS, axis=1)
    dot = functools.partial(jnp.dot, preferred_element_type=F32)
    q = dot(cq, wq_ref[...]) * cos + dot(cq, wqr_ref[...]) * sin
    q_ref[...] = (q * ((MLA_NOPE + MLA_ROPE) ** -0.5 * LOG2E)).astype(BF16)
    lhs = jnp.concatenate([ckv, za[:, Q_LORA + KV_LORA:]], axis=1)
    k_ref[...] = (dot(lhs, wk_ref[...]) * cos + dot(lhs, wkr_ref[...]) * sin).astype(BF16)
    lane = lax.broadcasted_iota(jnp.int32, (1, MLA_W), 1)
    ones_col = jnp.where(lane % MLA_HEAD_PAD == MLA_V, 1.0, 0.0)
    v_ref[...] = (dot(ckv, wv_ref[...]) + ones_col).astype(BF16)


def _mla_prep(z3, cos_t, sin_t, g_q, g_kv, weights, tm=512):
    bsz, seq, _ = z3.shape
    wq, wqr, wk, wkr, wv = weights

    def full(t):
        return pl.BlockSpec(t.shape, lambda b, i: (0,) * t.ndim)

    tab = pl.BlockSpec((tm, MLA_HEAD_PAD), lambda b, i: (i, 0))
    out = pl.BlockSpec((None, tm, MLA_W), lambda b, i: (b, i, 0))
    g_q = g_q.reshape(1, Q_LORA)
    g_kv = g_kv.reshape(1, KV_LORA)
    shp = jax.ShapeDtypeStruct((bsz, seq, MLA_W), BF16)
    return pl.pallas_call(
        _mla_prep_kernel,
        grid=(bsz, seq // tm),
        in_specs=[pl.BlockSpec((None, tm, A_WIDTH), lambda b, i: (b, i, A_COL // A_WIDTH)),
                  tab, tab, full(g_q), full(g_kv), full(wq), full(wqr), full(wk), full(wkr), full(wv)],
        out_specs=[out, out, out],
        out_shape=[shp, shp, shp],
        compiler_params=_cparams(("parallel", "parallel")),
        name="mla_prep",
    )(z3, cos_t, sin_t, g_q, g_kv, wq, wqr, wk, wkr, wv)


def _mla_flash_kernel(q_ref, k_ref, v_ref, o_ref, m_scr, acc_scr):
    ki = pl.program_id(2)

    @pl.when(ki == 0)
    def _():
        m_scr[...] = jnp.full(m_scr.shape, -jnp.inf, F32)
        acc_scr[...] = jnp.zeros(acc_scr.shape, F32)

    for h in range(MLA_HEADS):
        sl = slice(h * MLA_HEAD_PAD, (h + 1) * MLA_HEAD_PAD)
        s = lax.dot_general(q_ref[:, sl], k_ref[:, sl], (((1,), (1,)), ((), ())),
                            preferred_element_type=F32)
        m_prev = m_scr[h]
        m_new = jnp.maximum(m_prev, jnp.max(s, axis=1, keepdims=True))
        p = jnp.exp2(s - m_new[:, :1]).astype(BF16)
        acc_scr[h] = jnp.exp2(m_prev - m_new) * acc_scr[h] + jnp.dot(p, v_ref[:, sl], preferred_element_type=F32)
        m_scr[h] = m_new

    @pl.when(ki == pl.num_programs(2) - 1)
    def _():
        for h in range(MLA_HEADS):
            acc = acc_scr[h]
            o_ref[:, h * MLA_HEAD_PAD:(h + 1) * MLA_HEAD_PAD] = (acc / acc[:, MLA_V:MLA_V + 1]).astype(o_ref.dtype)


def _mla_flash(q, k, v, tq=512, tk=1024):
    bsz, seq, _ = q.shape
    tq, tk = min(tq, seq), min(tk, seq)
    return pl.pallas_call(
        _mla_flash_kernel,
        grid=(bsz, seq // tq, seq // tk),
        in_specs=[pl.BlockSpec((None, tq, MLA_W), lambda b, i, j: (b, i, 0)),
                  pl.BlockSpec((None, tk, MLA_W), lambda b, i, j: (b, j, 0)),
                  pl.BlockSpec((None, tk, MLA_W), lambda b, i, j: (b, j, 0))],
        out_specs=pl.BlockSpec((None, tq, MLA_W), lambda b, i, j: (b, i, 0)),
        out_shape=jax.ShapeDtypeStruct((bsz, seq, MLA_W), BF16),
        scratch_shapes=[pltpu.VMEM((MLA_HEADS, tq, MLA_HEAD_PAD), F32),
                        pltpu.VMEM((MLA_HEADS, tq, MLA_HEAD_PAD), F32)],
        compiler_params=_cparams(("parallel", "parallel", "arbitrary")),
        name="mla_flash",
    )(q, k, v)


def _t5_bucket(rel):
    nb = T5_BUCKETS // 2
    max_exact = nb // 2
    n = jnp.abs(rel)
    large = max_exact + (jnp.log(jnp.maximum(n, 1).astype(F32) / max_exact)
                         / math.log(T5_MAX_DIST / max_exact) * (nb - max_exact)).astype(jnp.int32)
    large = jnp.minimum(large, nb - 1)
    return jnp.where(rel > 0, nb, 0) + jnp.where(n < max_exact, n, large)


def _band_bias_tables(rel_bias, head_lo, tile, half, dilation):
    width = tile + 2 * half
    col = jnp.arange(width)[None, :]
    rel = col - half - jnp.arange(tile)[:, None]
    bias = rel_bias[_t5_bucket(rel * dilation)][..., head_lo:head_lo + HEADS].transpose(2, 0, 1)
    band = jnp.abs(rel) <= half
    left_ok = col >= half
    right_ok = col < tile + half
    masks = jnp.stack([band, band & left_ok, band & right_ok, band & left_ok & right_ok])
    return jnp.where(masks[:, None], bias[None].astype(F32), NEG_INF)


def _head_of_lane():
    return lax.broadcasted_iota(jnp.int32, (1, HEAD_BLOCK), 1) // HEAD_DIM


def _per_head_lanes(cols):
    head = _head_of_lane()
    out = cols[HEADS - 1]
    for h in range(HEADS - 2, -1, -1):
        out = jnp.where(head == h, cols[h], out)
    return out


def _banded_kernel(*refs, tile, half, length, use_sink, want_lse):
    refs = list(refs)
    sink_ref = refs.pop(0) if use_sink else None
    q_ref, k_ref, v_ref, bias_ref, o_ref = refs[:5]
    lse_ref = refs[5] if want_lse else None
    i = pl.program_id(2)
    nt = length // tile
    left = pl.multiple_of(jnp.maximum(i * tile - half, 0), half)
    main = pl.multiple_of(i * tile, tile)
    right = pl.multiple_of(jnp.minimum(i * tile + tile, length - half), half)
    variant = (i == 0).astype(jnp.int32) + 2 * (i == nt - 1).astype(jnp.int32)

    def window(ref):
        return jnp.concatenate([ref[pl.ds(left, half), :], ref[pl.ds(main, tile), :],
                                ref[pl.ds(right, half), :]], axis=0)

    kwin, vwin = window(k_ref), window(v_ref)
    q = q_ref[...] * (HEAD_DIM ** -0.5)
    head = _head_of_lane()
    acc = None
    inv_den, lse = [], []
    for h in range(HEADS):
        mine = head == h
        s = lax.dot_general(q, jnp.where(mine, kwin, 0), (((1,), (1,)), ((), ())),
                            preferred_element_type=F32)
        s = s + bias_ref[variant, h]
        m = jnp.max(s, axis=1, keepdims=True)
        if use_sink:
            m = jnp.maximum(m, sink_ref[h])
        e = jnp.exp(s - m)
        den = jnp.sum(e, axis=1, keepdims=True)
        if use_sink:
            den = den + jnp.exp(sink_ref[h] - m)
        part = jnp.dot(e.astype(BF16), jnp.where(mine, vwin, 0), preferred_element_type=F32)
        acc = part if acc is None else acc + part
        inv_den.append(1.0 / den)
        if want_lse:
            lse.append(m + jnp.log(den))
    o_ref[...] = (acc * _per_head_lanes(inv_den)).astype(o_ref.dtype)
    if want_lse:
        lse_ref[...] = _per_head_lanes(lse)


def _banded_attention(z3, blk, bias, sink, *, half, dilation, tile, want_lse):
    bsz, seq, _ = z3.shape
    length = seq // dilation
    tile = min(tile, length)
    nt = length // tile
    zf = z3.reshape(bsz, length, dilation * Z_COLS)
    use_sink = sink is not None
    in_specs = [pl.BlockSpec((None, tile, HEAD_BLOCK), lambda b, j, i: (b, i, j * Z_BLKS + blk)),
                pl.BlockSpec((None, length, HEAD_BLOCK), lambda b, j, i: (b, 0, j * Z_BLKS + blk + 1)),
                pl.BlockSpec((None, length, HEAD_BLOCK), lambda b, j, i: (b, 0, j * Z_BLKS + blk + 2)),
                pl.BlockSpec(bias.shape, lambda b, j, i: (0, 0, 0, 0))]
    args = [zf, zf, zf, bias]
    if use_sink:
        in_specs.insert(0, pl.BlockSpec(memory_space=pltpu.SMEM))
        args.insert(0, sink.astype(F32))
    o_spec = pl.BlockSpec((None, tile, HEAD_BLOCK), lambda b, j, i: (b, i, j))
    out_specs = [o_spec]
    out_shape = [jax.ShapeDtypeStruct((bsz, length, dilation * HEAD_BLOCK), BF16)]
    if want_lse:
        out_specs.append(o_spec)
        out_shape.append(jax.ShapeDtypeStruct((bsz, length, dilation * HEAD_BLOCK), F32))
    outs = pl.pallas_call(
        functools.partial(_banded_kernel, tile=tile, half=half, length=length,
                          use_sink=use_sink, want_lse=want_lse),
        grid=(bsz, dilation, nt),
        in_specs=in_specs,
        out_specs=out_specs,
        out_shape=out_shape,
        compiler_params=_cparams(("parallel", "parallel", "arbitrary")),
        name=f"banded_r{dilation}",
    )(*args)
    return [t.reshape(bsz, seq, HEAD_BLOCK) for t in outs]


NA_TILE_ROWS = 4
NA_WIN_ROWS = NA_TILE_ROWS + NA_KH


def _na_bias_tables(rpb, rows):
    tr, wr = NA_TILE_ROWS, NA_WIN_ROWS
    tabs = []
    for r0, ws in ((0, 0), (tr, 0), (rows - tr, rows - wr)):
        r = r0 + jnp.arange(tr)
        start = jnp.clip(r - NA_KH // 2, 0, rows - NA_KH)
        kr = ws + jnp.arange(wr)
        row_ok = (kr[None, :] >= start[:, None]) & (kr[None, :] < start[:, None] + NA_KH)
        dr = jnp.clip(kr[None, :] - r[:, None] + (NA_KH - 1), 0, 2 * NA_KH - 2)
        c = jnp.arange(GRID_W)
        cs = jnp.clip(c - NA_KW // 2, 0, GRID_W - NA_KW)
        col_ok = (c[None, :] >= cs[:, None]) & (c[None, :] < cs[:, None] + NA_KW)
        dc = jnp.clip(c[None, :] - c[:, None], -(NA_KW - 1), NA_KW - 1) + (NA_KW - 1)
        bias = rpb[:, dr[:, None, :, None], dc[None, :, None, :]]
        ok = row_ok[:, None, :, None] & col_ok[None, :, None, :]
        tabs.append(jnp.where(ok[None], bias.astype(F32), NEG_INF).reshape(HEADS, tr * GRID_W, wr * GRID_W))
    return jnp.stack(tabs)


def _na_kernel(q_ref, k_ref, v_ref, bias_ref, o_ref, *, rows):
    i = pl.program_id(1)
    nt = rows // NA_TILE_ROWS
    win = NA_WIN_ROWS * GRID_W
    ws = jnp.clip(i * NA_TILE_ROWS - NA_KH // 2, 0, rows - NA_WIN_ROWS) * GRID_W
    ws = pl.multiple_of(ws, GRID_W)
    variant = jnp.where(i == 0, 0, jnp.where(i == nt - 1, 2, 1))
    kwin = k_ref[pl.ds(ws, win), :]
    vwin = v_ref[pl.ds(ws, win), :]
    q = q_ref[...] * (HEAD_DIM ** -0.5)
    head = _head_of_lane()
    acc = None
    inv_den = []
    for h in range(HEADS):
        mine = head == h
        s = lax.dot_general(q, jnp.where(mine, kwin, 0), (((1,), (1,)), ((), ())),
                            preferred_element_type=F32)
        s = s + bias_ref[variant, h]
        m = jnp.max(s, axis=1, keepdims=True)
        e = jnp.exp(s - m)
        inv_den.append(1.0 / jnp.sum(e, axis=1, keepdims=True))
        part = jnp.dot(e.astype(BF16), jnp.where(mine, vwin, 0), preferred_element_type=F32)
        acc = part if acc is None else acc + part
    o_ref[...] = (acc * _per_head_lanes(inv_den)).astype(o_ref.dtype)


def _neighborhood_attention(z3, bias):
    bsz, seq, _ = z3.shape
    rows = seq // GRID_W
    tq = NA_TILE_ROWS * GRID_W
    return pl.pallas_call(
        functools.partial(_na_kernel, rows=rows),
        grid=(bsz, rows // NA_TILE_ROWS),
        in_specs=[pl.BlockSpec((None, tq, HEAD_BLOCK), lambda b, i: (b, i, C_BLK)),
                  pl.BlockSpec((None, seq, HEAD_BLOCK), lambda b, i: (b, 0, C_BLK + 1)),
                  pl.BlockSpec((None, seq, HEAD_BLOCK), lambda b, i: (b, 0, C_BLK + 2)),
                  pl.BlockSpec(bias.shape, lambda b, i: (0, 0, 0, 0))],
        out_specs=pl.BlockSpec((None, tq, HEAD_BLOCK), lambda b, i: (b, i, 0)),
        out_shape=jax.ShapeDtypeStruct((bsz, seq, HEAD_BLOCK), BF16),
        compiler_params=_cparams(("parallel", "arbitrary")),
        name="neighborhood",
    )(z3, z3, z3, bias)


def _merge_kernel(gate_ref, ya_ref, yb_ref, yc_ref, o0_ref, o1_ref, o2_ref, l0_ref, l1_ref, l2_ref, h_ref,
                  wa_ref, wb_ref, wo_ref, g_ref, b_ref, h_out_ref, hb_out_ref):
    dot = functools.partial(jnp.dot, preferred_element_type=F32)
    l0, l1, l2 = l0_ref[...], l1_ref[...], l2_ref[...]
    lm = jnp.maximum(jnp.maximum(l0, l1), l2)
    e0, e1, e2 = jnp.exp(l0 - lm), jnp.exp(l1 - lm), jnp.exp(l2 - lm)
    yd = (e0 * o0_ref[...].astype(F32) + e1 * o1_ref[...].astype(F32) + e2 * o2_ref[...].astype(F32)) / (e0 + e1 + e2)
    branches = (dot(ya_ref[...], wa_ref[...]), dot(yb_ref[...], wb_ref[0]), dot(yc_ref[...], wb_ref[1]),
                dot(yd.astype(BF16), wb_ref[2]))
    merged = None
    for n in range(N_BRANCH):
        gate = jax.nn.sigmoid(gate_ref[:, n * D_MODEL:(n + 1) * D_MODEL].astype(F32))
        term = gate * branches[n]
        merged = term if merged is None else merged + term
    u = ALPHA * h_ref[...] + dot(merged.astype(BF16), wo_ref[...])
    y = _layer_norm(u, g_ref[...], b_ref[...])
    h_out_ref[...] = y
    hb_out_ref[...] = y.astype(BF16)


def _merge(z2, ya, yb, yc, od, ld, h, wa, wb, wo, g, b, tm=256):
    m = h.shape[0]

    def rows(width, blk=0):
        return pl.BlockSpec((tm, width), lambda i: (i, blk))

    def full(t):
        return pl.BlockSpec(t.shape, lambda i: (0,) * t.ndim)

    g = g.reshape(1, D_MODEL)
    b = b.reshape(1, D_MODEL)
    return pl.pallas_call(
        _merge_kernel,
        grid=(m // tm,),
        in_specs=[rows(GATE_COLS), rows(MLA_W), rows(HEAD_BLOCK), rows(HEAD_BLOCK)]
        + [rows(HEAD_BLOCK)] * 6 + [rows(D_MODEL), full(wa), full(wb), full(wo), full(g), full(b)],
        out_specs=[rows(D_MODEL), rows(D_MODEL)],
        out_shape=[jax.ShapeDtypeStruct((m, D_MODEL), F32), jax.ShapeDtypeStruct((m, D_MODEL), BF16)],
        compiler_params=_cparams(("parallel",)),
        name="merge",
    )(z2, ya, yb, yc, *od, *ld, h, wa, wb, wo, g, b)


FF_CHUNK = 1024


def _ffn_kernel(h_ref, hb_ref, p_ref, w1_ref, w2_ref, wp_ref, wg_ref, g_ref, b_ref, h_out_ref, hb_out_ref):
    dot = functools.partial(jnp.dot, preferred_element_type=F32)
    hb = hb_ref[...]
    ff = None
    for c in range(D_FF // FF_CHUNK):
        sl = slice(c * FF_CHUNK, (c + 1) * FF_CHUNK)
        a = jnp.maximum(dot(hb, w1_ref[:, sl]), 0.0)
        part = dot((a * a).astype(BF16), w2_ref[sl, :])
        ff = part if ff is None else ff + part
    ple = dot(p_ref[...].astype(BF16), wp_ref[...]) * jax.nn.sigmoid(dot(hb, wg_ref[...]))
    y = _layer_norm(ALPHA * h_ref[...] + ff + ple, g_ref[...], b_ref[...])
    h_out_ref[...] = y
    hb_out_ref[...] = y.astype(BF16)


def _ffn(h, hb, p2, w1, w2, wp, wg, g, b, tm=256):
    m = h.shape[0]

    def rows(width):
        return pl.BlockSpec((tm, width), lambda i: (i, 0))

    def full(t):
        return pl.BlockSpec(t.shape, lambda i: (0,) * t.ndim)

    g = g.reshape(1, D_MODEL)
    b = b.reshape(1, D_MODEL)
    return pl.pallas_call(
        _ffn_kernel,
        grid=(m // tm,),
        in_specs=[rows(D_MODEL), rows(D_MODEL), rows(PLE_DIM), full(w1), full(w2), full(wp), full(wg),
                  full(g), full(b)],
        out_specs=[rows(D_MODEL), rows(D_MODEL)],
        out_shape=[jax.ShapeDtypeStruct((m, D_MODEL), F32), jax.ShapeDtypeStruct((m, D_MODEL), BF16)],
        compiler_params=_cparams(("parallel",)),
        name="ffn",
    )(h, hb, p2, w1, w2, wp, wg, g, b)


BAND_TILE = 128


def kernel(x, p, ln_emb_g, ln_emb_b, rel_bias, w_in, mla_q_norm, mla_w_uq, mla_kv_norm, mla_w_ukv, swa_sink,
           na_rpb, w_branch, w_out, ln1_g, ln1_b, w_ff1, w_ff2, w_ple, w_ple_gate, ln2_g, ln2_b):
    bsz, seq, _ = x.shape
    m = bsz * seq
    rows = seq // GRID_W
    cos_t, sin_t = _rope_tables(seq)
    swa_bias = _band_bias_tables(rel_bias, 0, min(BAND_TILE, seq), SWA_HALF, 1)
    dil_bias = []
    for g, (window, r) in enumerate(DIL_CONFIGS):
        half = window // (2 * r)
        dil_bias.append(_band_bias_tables(rel_bias, SWA_HEADS + g * DIL_HEADS, min(BAND_TILE, seq // r), half, r))

    h, hb = _embed_layer_norm(x.reshape(m, D_MODEL), ln_emb_g, ln_emb_b)
    for i in range(DEPTH):
        z2 = _matmul(hb, _prep_w_in(w_in[i]), 1024, 768, BF16, "in_proj")
        z3 = z2.reshape(bsz, seq, Z_COLS)
        q_a, k_a, v_a = _mla_prep(z3, cos_t, sin_t, mla_q_norm[i], mla_kv_norm[i],
                                  _prep_mla_weights(mla_w_uq[i], mla_w_ukv[i]))
        y_a = _mla_flash(q_a, k_a, v_a).reshape(m, MLA_W)
        (y_b,) = _banded_attention(z3, B_BLK, swa_bias, swa_sink[i], half=SWA_HALF, dilation=1,
                                   tile=BAND_TILE, want_lse=False)
        y_c = _neighborhood_attention(z3, _na_bias_tables(na_rpb[i], rows))
        o_d, l_d = [], []
        for g, (window, r) in enumerate(DIL_CONFIGS):
            o_g, l_g = _banded_attention(z3, D_BLK + 3 * g, dil_bias[g], None, half=window // (2 * r),
                                         dilation=r, tile=BAND_TILE, want_lse=True)
            o_d.append(o_g.reshape(m, HEAD_BLOCK))
            l_d.append(l_g.reshape(m, HEAD_BLOCK))
        wa = jnp.pad(w_branch[i, 0].reshape(MLA_HEADS, MLA_V, D_MODEL),
                     ((0, 0), (0, MLA_HEAD_PAD - MLA_V), (0, 0))).reshape(MLA_W, D_MODEL).astype(BF16)
        h, hb = _merge(z2, y_a, y_b.reshape(m, HEAD_BLOCK), y_c.reshape(m, HEAD_BLOCK), o_d, l_d, h,
                       wa, w_branch[i, 1:].astype(BF16), w_out[i].astype(BF16), ln1_g[i], ln1_b[i])
        h, hb = _ffn(h, hb, p[i].reshape(m, PLE_DIM), w_ff1[i].astype(BF16), w_ff2[i].astype(BF16),
                     w_ple[i].astype(BF16), w_ple_gate[i].astype(BF16), ln2_g[i], ln2_b[i])
    return h.reshape(bsz, seq, D_MODEL)
```

```python
import functools
import math

import jax
import jax.numpy as jnp
import numpy as np
from jax import lax
from jax.experimental import pallas as pl
from jax.experimental.pallas import tpu as pltpu

F32 = jnp.float32
BF16 = jnp.bfloat16

D_MODEL = 1024
DEPTH = 2
HEAD_DIM = 64
GRID_W = 64
NEG_INF = -1e30

MLA_HEADS = 4
MLA_NOPE = 64
MLA_ROPE = 32
MLA_V = 64
Q_LORA = 256
KV_LORA = 128
ROPE_THETA = 10000.0

SWA_HEADS = 4
SWA_KV_HEADS = 2
SWA_HALF = 128

NA_HEADS = 4
NA_KH = 8
NA_KW = 16

DIL_CONFIGS = ((128, 1), (512, 4), (2048, 16))
DIL_GROUPS = 3
DIL_HEADS = 4

T5_BUCKETS = 32
T5_MAX_DIST = 1024

N_BRANCH = 4
BRANCH_W = 256
D_FF = 4 * D_MODEL
PLE_DIM = 256

ALPHA = (2 * DEPTH) ** 0.25

LANES = 128
HEADS = 4
HEAD_BLOCK = HEADS * HEAD_DIM

GATE_COLS = N_BRANCH * D_MODEL
A_COL = GATE_COLS
A_WIDTH = 512
B_BLK = (A_COL + A_WIDTH) // HEAD_BLOCK
C_BLK = B_BLK + 3
Z_BLKS = C_BLK + 3
Z_COLS = Z_BLKS * HEAD_BLOCK
QKV_COLS = 3 * HEAD_BLOCK

MLA_HEAD_PAD = 128
MLA_W = MLA_HEADS * MLA_HEAD_PAD
LOG2E = math.log2(math.e)

VMEM_LIMIT = 56 * 1024 * 1024


def _cparams(sem):
    return pltpu.CompilerParams(dimension_semantics=sem, vmem_limit_bytes=VMEM_LIMIT)


def _layer_norm(u, g, b, eps=1e-5):
    mu = jnp.mean(u, axis=-1, keepdims=True)
    d = u - mu
    var = jnp.mean(d * d, axis=-1, keepdims=True)
    return d * lax.rsqrt(var + eps) * g + b


def _rms_norm(u, g, eps=1e-6):
    return u * lax.rsqrt(jnp.mean(u * u, axis=-1, keepdims=True) + eps) * g


def _ln_kernel(x_ref, g_ref, b_ref, h_ref, hb_ref):
    y = _layer_norm(x_ref[...], g_ref[...], b_ref[...])
    h_ref[...] = y
    hb_ref[...] = y.astype(BF16)


def _embed_layer_norm(x2, g, b, tm=512):
    m, d = x2.shape
    row = pl.BlockSpec((tm, d), lambda i: (i, 0))
    vec = pl.BlockSpec((1, d), lambda i: (0, 0))
    return pl.pallas_call(
        _ln_kernel,
        grid=(m // tm,),
        in_specs=[row, vec, vec],
        out_specs=[row, row],
        out_shape=[jax.ShapeDtypeStruct((m, d), F32), jax.ShapeDtypeStruct((m, d), BF16)],
        compiler_params=_cparams(("parallel",)),
        name="embed_ln",
    )(x2, g.reshape(1, d), b.reshape(1, d))


def _mm_kernel(a_ref, w_ref, o_ref):
    o_ref[...] = jnp.dot(a_ref[...], w_ref[...], preferred_element_type=F32).astype(o_ref.dtype)


def _matmul(a, w, tm, tn, out_dtype, name):
    m, k = a.shape
    n = w.shape[1]
    return pl.pallas_call(
        _mm_kernel,
        grid=(m // tm, n // tn),
        in_specs=[pl.BlockSpec((tm, k), lambda i, j: (i, 0)),
                  pl.BlockSpec((k, tn), lambda i, j: (0, j))],
        out_specs=pl.BlockSpec((tm, tn), lambda i, j: (i, j)),
        out_shape=jax.ShapeDtypeStruct((m, n), out_dtype),
        compiler_params=_cparams(("parallel", "arbitrary")),
        name=name,
    )(a, w)


def _prep_w_in(w):
    o = 0
    a = w[:, o:o + Q_LORA + KV_LORA + MLA_ROPE]
    o += Q_LORA + KV_LORA + MLA_ROPE
    bq = w[:, o:o + HEAD_BLOCK]
    o += HEAD_BLOCK
    kvw = SWA_KV_HEADS * HEAD_DIM
    bk = w[:, o:o + kvw]
    o += kvw
    bv = w[:, o:o + kvw]
    o += kvw
    c = w[:, o:o + 3 * HEAD_BLOCK]
    o += 3 * HEAD_BLOCK
    d = w[:, o:o + 3 * DIL_GROUPS * HEAD_BLOCK]
    o += 3 * DIL_GROUPS * HEAD_BLOCK
    gates = w[:, o:]
    rep = SWA_HEADS // SWA_KV_HEADS

    def expand(t):
        t = t.reshape(D_MODEL, SWA_KV_HEADS, 1, HEAD_DIM)
        return jnp.broadcast_to(t, (D_MODEL, SWA_KV_HEADS, rep, HEAD_DIM)).reshape(D_MODEL, HEAD_BLOCK)

    a = jnp.pad(a, ((0, 0), (0, A_WIDTH - a.shape[1])))
    d = d.reshape(D_MODEL, 3, DIL_GROUPS, HEAD_BLOCK).transpose(2, 0, 1, 3).reshape(DIL_GROUPS, D_MODEL, QKV_COLS)
    return jnp.concatenate([gates, a, bq, expand(bk), expand(bv), c], axis=1).astype(BF16), d.astype(BF16)


def _mm_fold_kernel(a_ref, w_ref, o_ref, acc_ref, *, dilation):
    acc = jnp.dot(a_ref[...], w_ref[...], preferred_element_type=F32)
    nblk, tm, _ = acc_ref.shape
    for c in range(nblk):
        acc_ref[c] = acc[:, c * LANES:(c + 1) * LANES]
    n = tm // dilation
    for j in range(dilation):
        for c in range(nblk):
            o_ref[j, :, c * LANES:(c + 1) * LANES] = (
                acc_ref[c, pl.ds(j, n, stride=dilation), :].astype(o_ref.dtype))


def _matmul_fold(a3, w, dilation, tm, name):
    bsz, seq, k = a3.shape
    n = w.shape[1]
    return pl.pallas_call(
        functools.partial(_mm_fold_kernel, dilation=dilation),
        grid=(bsz, seq // tm),
        in_specs=[pl.BlockSpec((None, tm, k), lambda b, i: (b, i, 0)),
                  pl.BlockSpec((k, n), lambda b, i: (0, 0))],
        out_specs=pl.BlockSpec((None, dilation, tm // dilation, n), lambda b, i: (b, 0, i, 0)),
        out_shape=jax.ShapeDtypeStruct((bsz, dilation, seq // dilation, n), BF16),
        scratch_shapes=[pltpu.VMEM((n // LANES, tm, LANES), F32)],
        compiler_params=_cparams(("parallel", "parallel")),
        name=name,
    )(a3, w)


def _rope_tables(seq):
    half = MLA_ROPE // 2
    inv = ROPE_THETA ** (-jnp.arange(half, dtype=F32) / half)
    ang = jnp.arange(seq).astype(F32)[:, None] * inv[None, :]
    cos, sin = jnp.cos(ang), jnp.sin(ang)
    pad = MLA_HEAD_PAD - MLA_NOPE - MLA_ROPE
    cos_t = jnp.concatenate([jnp.ones((seq, MLA_NOPE), F32), cos, cos, jnp.zeros((seq, pad), F32)], axis=1)
    sin_t = jnp.concatenate([jnp.zeros((seq, MLA_NOPE), F32), sin, sin, jnp.zeros((seq, pad), F32)], axis=1)
    return cos_t, sin_t


def _rot_cols(w_rope):
    half = MLA_ROPE // 2
    return jnp.concatenate([-w_rope[..., half:], w_rope[..., :half]], axis=-1)


def _prep_mla_weights(w_uq, w_ukv):
    pad = MLA_HEAD_PAD - MLA_NOPE - MLA_ROPE
    wq = w_uq.reshape(Q_LORA, MLA_HEADS, MLA_NOPE + MLA_ROPE)
    wq_main = jnp.pad(wq, ((0, 0), (0, 0), (0, pad))).reshape(Q_LORA, MLA_W)
    wq_rot = jnp.concatenate([jnp.zeros((Q_LORA, MLA_HEADS, MLA_NOPE), F32), _rot_cols(wq[..., MLA_NOPE:]),
                              jnp.zeros((Q_LORA, MLA_HEADS, pad), F32)], axis=-1).reshape(Q_LORA, MLA_W)
    wkv = w_ukv.reshape(KV_LORA, MLA_HEADS, MLA_NOPE + MLA_V)
    wk_n = jnp.pad(wkv[..., :MLA_NOPE], ((0, 0), (0, 0), (0, MLA_HEAD_PAD - MLA_NOPE))).reshape(KV_LORA, MLA_W)
    wv = jnp.pad(wkv[..., MLA_NOPE:], ((0, 0), (0, 0), (0, MLA_HEAD_PAD - MLA_V))).reshape(KV_LORA, MLA_W)
    eye = jnp.eye(MLA_ROPE, dtype=F32)
    place = jnp.concatenate([jnp.zeros((MLA_ROPE, MLA_NOPE), F32), eye, jnp.zeros((MLA_ROPE, pad), F32)], axis=1)
    place_rot = jnp.concatenate([jnp.zeros((MLA_ROPE, MLA_NOPE), F32), _rot_cols(eye),
                                 jnp.zeros((MLA_ROPE, pad), F32)], axis=1)

    def lift(pm):
        pm = jnp.pad(pm, ((0, LANES - MLA_ROPE), (0, 0)))
        return jnp.tile(pm, (1, MLA_HEADS))

    wk_main = jnp.concatenate([wk_n, lift(place)], axis=0)
    wk_rot = jnp.concatenate([jnp.zeros((KV_LORA, MLA_W), F32), lift(place_rot)], axis=0)
    return tuple(t.astype(BF16) for t in (wq_main, wq_rot, wk_main, wk_rot, wv))


def _mla_prep_kernel(za_ref, cos_ref, sin_ref, gq_ref, gkv_ref, wq_ref, wqr_ref, wk_ref, wkr_ref, wv_ref,
                     q_ref, k_ref, v_ref):
    za = za_ref[...]
    cq = _rms_norm(za[:, :Q_LORA].astype(F32), gq_ref[...]).astype(BF16)
    ckv = _rms_norm(za[:, Q_LORA:Q_LORA + KV_LORA].astype(F32), gkv_ref[...]).astype(BF16)
    cos = jnp.concatenate([cos_ref[...]] * MLA_HEADS, axis=1)
    sin = jnp.concatenate([sin_ref[...]] * MLA_HEADS, axis=1)
    dot = functools.partial(jnp.dot, preferred_element_type=F32)
    q = dot(cq, wq_ref[...]) * cos + dot(cq, wqr_ref[...]) * sin
    q_ref[...] = (q * ((MLA_NOPE + MLA_ROPE) ** -0.5 * LOG2E)).astype(BF16)
    lhs = jnp.concatenate([ckv, za[:, Q_LORA + KV_LORA:]], axis=1)
    k_ref[...] = (dot(lhs, wk_ref[...]) * cos + dot(lhs, wkr_ref[...]) * sin).astype(BF16)
    lane = lax.broadcasted_iota(jnp.int32, (1, MLA_W), 1)
    ones_col = jnp.where(lane % MLA_HEAD_PAD == MLA_V, 1.0, 0.0)
    v_ref[...] = (dot(ckv, wv_ref[...]) + ones_col).astype(BF16)


def _mla_prep(z3, cos_t, sin_t, g_q, g_kv, weights, tm=512):
    bsz, seq, _ = z3.shape
    wq, wqr, wk, wkr, wv = weights

    def full(t):
        return pl.BlockSpec(t.shape, lambda b, i: (0,) * t.ndim)

    tab = pl.BlockSpec((tm, MLA_HEAD_PAD), lambda b, i: (i, 0))
    out = pl.BlockSpec((None, tm, MLA_W), lambda b, i: (b, i, 0))
    g_q = g_q.reshape(1, Q_LORA)
    g_kv = g_kv.reshape(1, KV_LORA)
    shp = jax.ShapeDtypeStruct((bsz, seq, MLA_W), BF16)
    return pl.pallas_call(
        _mla_prep_kernel,
        grid=(bsz, seq // tm),
        in_specs=[pl.BlockSpec((None, tm, A_WIDTH), lambda b, i: (b, i, A_COL // A_WIDTH)),
                  tab, tab, full(g_q), full(g_kv), full(wq), full(wqr), full(wk), full(wkr), full(wv)],
        out_specs=[out, out, out],
        out_shape=[shp, shp, shp],
        compiler_params=_cparams(("parallel", "parallel")),
        name="mla_prep",
    )(z3, cos_t, sin_t, g_q, g_kv, wq, wqr, wk, wkr, wv)


def _mla_flash_kernel(q_ref, k_ref, v_ref, o_ref, m_scr, acc_scr):
    ki = pl.program_id(2)

    @pl.when(ki == 0)
    def _():
        m_scr[...] = jnp.full(m_scr.shape, -jnp.inf, F32)
        acc_scr[...] = jnp.zeros(acc_scr.shape, F32)

    for h in range(MLA_HEADS):
        sl = slice(h * MLA_HEAD_PAD, (h + 1) * MLA_HEAD_PAD)
        s = lax.dot_general(q_ref[:, sl], k_ref[:, sl], (((1,), (1,)), ((), ())),
                            preferred_element_type=F32)
        m_prev = m_scr[h]
        m_new = jnp.maximum(m_prev, jnp.max(s, axis=1, keepdims=True))
        p = jnp.exp2(s - m_new[:, :1]).astype(BF16)
        acc_scr[h] = jnp.exp2(m_prev - m_new) * acc_scr[h] + jnp.dot(p, v_ref[:, sl], preferred_element_type=F32)
        m_scr[h] = m_new

    @pl.when(ki == pl.num_programs(2) - 1)
    def _():
        for h in range(MLA_HEADS):
            acc = acc_scr[h]
            o_ref[:, h * MLA_HEAD_PAD:(h + 1) * MLA_HEAD_PAD] = (acc / acc[:, MLA_V:MLA_V + 1]).astype(o_ref.dtype)


def _mla_flash(q, k, v, tq=512, tk=1024):
    bsz, seq, _ = q.shape
    tq, tk = min(tq, seq), min(tk, seq)
    return pl.pallas_call(
        _mla_flash_kernel,
        grid=(bsz, seq // tq, seq // tk),
        in_specs=[pl.BlockSpec((None, tq, MLA_W), lambda b, i, j: (b, i, 0)),
                  pl.BlockSpec((None, tk, MLA_W), lambda b, i, j: (b, j, 0)),
                  pl.BlockSpec((None, tk, MLA_W), lambda b, i, j: (b, j, 0))],
        out_specs=pl.BlockSpec((None, tq, MLA_W), lambda b, i, j: (b, i, 0)),
        out_shape=jax.ShapeDtypeStruct((bsz, seq, MLA_W), BF16),
        scratch_shapes=[pltpu.VMEM((MLA_HEADS, tq, MLA_HEAD_PAD), F32),
                        pltpu.VMEM((MLA_HEADS, tq, MLA_HEAD_PAD), F32)],
        compiler_params=_cparams(("parallel", "parallel", "arbitrary")),
        name="mla_flash",
    )(q, k, v)


def _t5_bucket(rel):
    nb = T5_BUCKETS // 2
    max_exact = nb // 2
    n = jnp.abs(rel)
    large = max_exact + (jnp.log(jnp.maximum(n, 1).astype(F32) / max_exact)
                         / math.log(T5_MAX_DIST / max_exact) * (nb - max_exact)).astype(jnp.int32)
    large = jnp.minimum(large, nb - 1)
    return jnp.where(rel > 0, nb, 0) + jnp.where(n < max_exact, n, large)


def _band_bias_tables(rel_bias, head_lo, tile, half, dilation):
    width = tile + 2 * half
    col = jnp.arange(width)[None, :]
    rel = col - half - jnp.arange(tile)[:, None]
    onehot = jax.nn.one_hot(_t5_bucket(rel * dilation), T5_BUCKETS, dtype=F32)
    bias = jnp.einsum("qkn,nh->hqk", onehot, rel_bias[:, head_lo:head_lo + HEADS].astype(F32),
                      precision=lax.Precision.HIGHEST)
    band = jnp.abs(rel) <= half
    left_ok = col >= half
    right_ok = col < tile + half
    masks = jnp.stack([band, band & left_ok, band & right_ok, band & left_ok & right_ok])
    return jnp.where(masks[:, None], bias[None].astype(F32), NEG_INF)


def _head_of_lane():
    return lax.broadcasted_iota(jnp.int32, (1, HEAD_BLOCK), 1) // HEAD_DIM


def _per_head_lanes(cols):
    head = _head_of_lane()
    out = cols[HEADS - 1]
    for h in range(HEADS - 2, -1, -1):
        out = jnp.where(head == h, cols[h], out)
    return out


def _banded_kernel(*refs, tile, half, length, use_sink, want_lse):
    refs = list(refs)
    sink_ref = refs.pop(0) if use_sink else None
    q_ref, k_ref, v_ref, bias_ref, o_ref = refs[:5]
    lse_ref = refs[5] if want_lse else None
    i = pl.program_id(2)
    nt = length // tile
    left = pl.multiple_of(jnp.maximum(i * tile - half, 0), half)
    main = pl.multiple_of(i * tile, tile)
    right = pl.multiple_of(jnp.minimum(i * tile + tile, length - half), half)
    variant = (i == 0).astype(jnp.int32) + 2 * (i == nt - 1).astype(jnp.int32)

    def window(ref):
        return jnp.concatenate([ref[pl.ds(left, half), :], ref[pl.ds(main, tile), :],
                                ref[pl.ds(right, half), :]], axis=0)

    kwin, vwin = window(k_ref), window(v_ref)
    q = q_ref[...] * (HEAD_DIM ** -0.5)
    head = _head_of_lane()
    acc = None
    inv_den, lse = [], []
    for h in range(HEADS):
        mine = head == h
        s = lax.dot_general(q, jnp.where(mine, kwin, 0), (((1,), (1,)), ((), ())),
                            preferred_element_type=F32)
        s = s + bias_ref[variant, h]
        m = jnp.max(s, axis=1, keepdims=True)
        if use_sink:
            m = jnp.maximum(m, sink_ref[h])
        e = jnp.exp(s - m)
        den = jnp.sum(e, axis=1, keepdims=True)
        if use_sink:
            den = den + jnp.exp(sink_ref[h] - m)
        part = jnp.dot(e.astype(BF16), jnp.where(mine, vwin, 0), preferred_element_type=F32)
        acc = part if acc is None else acc + part
        inv_den.append(1.0 / den)
        if want_lse:
            lse.append(m + jnp.log(den))
    o_ref[...] = (acc * _per_head_lanes(inv_den)).astype(o_ref.dtype)
    if want_lse:
        lse_ref[...] = _per_head_lanes(lse)


def _banded_attention(zf, blk, bias, sink, *, half, tile, want_lse):
    bsz, dilation, length, _ = zf.shape
    tile = min(tile, length)
    nt = length // tile
    use_sink = sink is not None
    in_specs = [pl.BlockSpec((None, None, tile, HEAD_BLOCK), lambda b, j, i: (b, j, i, blk)),
                pl.BlockSpec((None, None, length, HEAD_BLOCK), lambda b, j, i: (b, j, 0, blk + 1)),
                pl.BlockSpec((None, None, length, HEAD_BLOCK), lambda b, j, i: (b, j, 0, blk + 2)),
                pl.BlockSpec(bias.shape, lambda b, j, i: (0, 0, 0, 0))]
    args = [zf, zf, zf, bias]
    if use_sink:
        in_specs.insert(0, pl.BlockSpec(memory_space=pltpu.SMEM))
        args.insert(0, sink.astype(F32))
    o_spec = pl.BlockSpec((None, None, tile, HEAD_BLOCK), lambda b, j, i: (b, j, i, 0))
    out_specs = [o_spec]
    out_shape = [jax.ShapeDtypeStruct((bsz, dilation, length, HEAD_BLOCK), BF16)]
    if want_lse:
        out_specs.append(o_spec)
        out_shape.append(jax.ShapeDtypeStruct((bsz, dilation, length, HEAD_BLOCK), F32))
    return pl.pallas_call(
        functools.partial(_banded_kernel, tile=tile, half=half, length=length,
                          use_sink=use_sink, want_lse=want_lse),
        grid=(bsz, dilation, nt),
        in_specs=in_specs,
        out_specs=out_specs,
        out_shape=out_shape,
        compiler_params=_cparams(("parallel", "parallel", "arbitrary")),
        name=f"banded_r{dilation}",
    )(*args)


NA_TILE_ROWS = 4
NA_WIN_ROWS = NA_TILE_ROWS + NA_KH


def _na_bias_tables(rpb, rows):
    tr, wr = NA_TILE_ROWS, NA_WIN_ROWS
    exact = functools.partial(jnp.einsum, precision=lax.Precision.HIGHEST)
    c = np.arange(GRID_W)
    cs = np.clip(c - NA_KW // 2, 0, GRID_W - NA_KW)
    col_ok = (c[None, :] >= cs[:, None]) & (c[None, :] < cs[:, None] + NA_KW)
    dc = np.clip(c[None, :] - c[:, None], -(NA_KW - 1), NA_KW - 1) + (NA_KW - 1)
    by_col = exact("hdn,qkn->hdqk", rpb.astype(F32), np.eye(2 * NA_KW - 1, dtype=np.float32)[dc])
    tabs = []
    for r0, ws in ((0, 0), (tr, 0), (rows - tr, rows - wr)):
        r = r0 + np.arange(tr)
        start = np.clip(r - NA_KH // 2, 0, rows - NA_KH)
        kr = ws + np.arange(wr)
        row_ok = (kr[None, :] >= start[:, None]) & (kr[None, :] < start[:, None] + NA_KH)
        dr = np.clip(kr[None, :] - r[:, None] + (NA_KH - 1), 0, 2 * NA_KH - 2)
        bias = exact("rad,hdqk->hrqak", np.eye(2 * NA_KH - 1, dtype=np.float32)[dr], by_col)
        ok = row_ok[:, None, :, None] & col_ok[None, :, None, :]
        tabs.append(jnp.where(ok[None], bias, NEG_INF).reshape(HEADS, tr * GRID_W, wr * GRID_W))
    return jnp.stack(tabs)


def _na_kernel(q_ref, k_ref, v_ref, bias_ref, o_ref, *, rows):
    i = pl.program_id(1)
    nt = rows // NA_TILE_ROWS
    win = NA_WIN_ROWS * GRID_W
    ws = jnp.clip(i * NA_TILE_ROWS - NA_KH // 2, 0, rows - NA_WIN_ROWS) * GRID_W
    ws = pl.multiple_of(ws, GRID_W)
    variant = jnp.where(i == 0, 0, jnp.where(i == nt - 1, 2, 1))
    kwin = k_ref[pl.ds(ws, win), :]
    vwin = v_ref[pl.ds(ws, win), :]
    q = q_ref[...] * (HEAD_DIM ** -0.5)
    head = _head_of_lane()
    acc = None
    inv_den = []
    for h in range(HEADS):
        mine = head == h
        s = lax.dot_general(q, jnp.where(mine, kwin, 0), (((1,), (1,)), ((), ())),
                            preferred_element_type=F32)
        s = s + bias_ref[variant, h]
        m = jnp.max(s, axis=1, keepdims=True)
        e = jnp.exp(s - m)
        inv_den.append(1.0 / jnp.sum(e, axis=1, keepdims=True))
        part = jnp.dot(e.astype(BF16), jnp.where(mine, vwin, 0), preferred_element_type=F32)
        acc = part if acc is None else acc + part
    o_ref[...] = (acc * _per_head_lanes(inv_den)).astype(o_ref.dtype)


def _neighborhood_attention(z3, bias):
    bsz, seq, _ = z3.shape
    rows = seq // GRID_W
    tq = NA_TILE_ROWS * GRID_W
    return pl.pallas_call(
        functools.partial(_na_kernel, rows=rows),
        grid=(bsz, rows // NA_TILE_ROWS),
        in_specs=[pl.BlockSpec((None, tq, HEAD_BLOCK), lambda b, i: (b, i, C_BLK)),
                  pl.BlockSpec((None, seq, HEAD_BLOCK), lambda b, i: (b, 0, C_BLK + 1)),
                  pl.BlockSpec((None, seq, HEAD_BLOCK), lambda b, i: (b, 0, C_BLK + 2)),
                  pl.BlockSpec(bias.shape, lambda b, i: (0, 0, 0, 0))],
        out_specs=pl.BlockSpec((None, tq, HEAD_BLOCK), lambda b, i: (b, i, 0)),
        out_shape=jax.ShapeDtypeStruct((bsz, seq, HEAD_BLOCK), BF16),
        compiler_params=_cparams(("parallel", "arbitrary")),
        name="neighborhood",
    )(z3, z3, z3, bias)


def _unfold(ref, scr):
    dilation, n, cols = ref.shape
    if dilation == 1:
        return ref[0].astype(F32)
    nblk = cols // LANES
    for j in range(dilation):
        for c in range(nblk):
            scr[c, pl.ds(j, n, stride=dilation), :] = ref[j, :, c * LANES:(c + 1) * LANES].astype(F32)
    return jnp.concatenate([scr[c] for c in range(nblk)], axis=1)


def _merge_kernel(gate_ref, ya_ref, yb_ref, yc_ref, o0_ref, o1_ref, o2_ref, l0_ref, l1_ref, l2_ref, h_ref,
                  wa_ref, wb_ref, wo_ref, g_ref, b_ref, h_out_ref, hb_out_ref, *scr):
    dot = functools.partial(jnp.dot, preferred_element_type=F32)
    l0, l1, l2 = _unfold(l0_ref, None), _unfold(l1_ref, scr[0]), _unfold(l2_ref, scr[1])
    o0, o1, o2 = _unfold(o0_ref, None), _unfold(o1_ref, scr[2]), _unfold(o2_ref, scr[3])
    lm = jnp.maximum(jnp.maximum(l0, l1), l2)
    e0, e1, e2 = jnp.exp(l0 - lm), jnp.exp(l1 - lm), jnp.exp(l2 - lm)
    yd = (e0 * o0 + e1 * o1 + e2 * o2) / (e0 + e1 + e2)
    branches = (dot(ya_ref[...], wa_ref[...]), dot(yb_ref[0], wb_ref[0]), dot(yc_ref[...], wb_ref[1]),
                dot(yd.astype(BF16), wb_ref[2]))
    merged = None
    for n in range(N_BRANCH):
        gate = jax.nn.sigmoid(gate_ref[:, n * D_MODEL:(n + 1) * D_MODEL].astype(F32))
        term = gate * branches[n]
        merged = term if merged is None else merged + term
    u = ALPHA * h_ref[...] + dot(merged.astype(BF16), wo_ref[...])
    y = _layer_norm(u, g_ref[...], b_ref[...])
    h_out_ref[...] = y
    hb_out_ref[...] = y.astype(BF16)


def _merge(z3, ya, yb, yc, od, ld, h3, wa, wb, wo, g, b, tm=256):
    bsz, seq, _ = h3.shape

    def rows(width):
        return pl.BlockSpec((None, tm, width), lambda b, i: (b, i, 0))

    def folded(t):
        dilation = t.shape[1]
        return pl.BlockSpec((None, dilation, tm // dilation, HEAD_BLOCK), lambda b, i: (b, 0, i, 0))

    def full(t):
        return pl.BlockSpec(t.shape, lambda b, i: (0,) * t.ndim)

    g = g.reshape(1, D_MODEL)
    b = b.reshape(1, D_MODEL)
    return pl.pallas_call(
        _merge_kernel,
        grid=(bsz, seq // tm),
        in_specs=[rows(GATE_COLS), rows(MLA_W), folded(yb), rows(HEAD_BLOCK)]
        + [folded(t) for t in od] + [folded(t) for t in ld]
        + [rows(D_MODEL), full(wa), full(wb), full(wo), full(g), full(b)],
        out_specs=[rows(D_MODEL), rows(D_MODEL)],
        out_shape=[jax.ShapeDtypeStruct((bsz, seq, D_MODEL), F32), jax.ShapeDtypeStruct((bsz, seq, D_MODEL), BF16)],
        scratch_shapes=[pltpu.VMEM((HEAD_BLOCK // LANES, tm, LANES), F32)] * 4,
        compiler_params=_cparams(("parallel", "parallel")),
        name="merge",
    )(z3, ya, yb, yc, *od, *ld, h3, wa, wb, wo, g, b)


FF_CHUNK = 1024


def _ffn_kernel(h_ref, hb_ref, p_ref, w1_ref, w2_ref, wp_ref, wg_ref, g_ref, b_ref, h_out_ref, hb_out_ref):
    dot = functools.partial(jnp.dot, preferred_element_type=F32)
    hb = hb_ref[...]
    ff = None
    for c in range(D_FF // FF_CHUNK):
        sl = slice(c * FF_CHUNK, (c + 1) * FF_CHUNK)
        a = jnp.maximum(dot(hb, w1_ref[:, sl]), 0.0)
        part = dot((a * a).astype(BF16), w2_ref[sl, :])
        ff = part if ff is None else ff + part
    ple = dot(p_ref[...].astype(BF16), wp_ref[...]) * jax.nn.sigmoid(dot(hb, wg_ref[...]))
    y = _layer_norm(ALPHA * h_ref[...] + ff + ple, g_ref[...], b_ref[...])
    h_out_ref[...] = y
    hb_out_ref[...] = y.astype(BF16)


def _ffn(h, hb, p2, w1, w2, wp, wg, g, b, tm=256):
    m = h.shape[0]

    def rows(width):
        return pl.BlockSpec((tm, width), lambda i: (i, 0))

    def full(t):
        return pl.BlockSpec(t.shape, lambda i: (0,) * t.ndim)

    g = g.reshape(1, D_MODEL)
    b = b.reshape(1, D_MODEL)
    return pl.pallas_call(
        _ffn_kernel,
        grid=(m // tm,),
        in_specs=[rows(D_MODEL), rows(D_MODEL), rows(PLE_DIM), full(w1), full(w2), full(wp), full(wg),
                  full(g), full(b)],
        out_specs=[rows(D_MODEL), rows(D_MODEL)],
        out_shape=[jax.ShapeDtypeStruct((m, D_MODEL), F32), jax.ShapeDtypeStruct((m, D_MODEL), BF16)],
        compiler_params=_cparams(("parallel",)),
        name="ffn",
    )(h, hb, p2, w1, w2, wp, wg, g, b)


BAND_TILE = 128


def kernel(x, p, ln_emb_g, ln_emb_b, rel_bias, w_in, mla_q_norm, mla_w_uq, mla_kv_norm, mla_w_ukv, swa_sink,
           na_rpb, w_branch, w_out, ln1_g, ln1_b, w_ff1, w_ff2, w_ple, w_ple_gate, ln2_g, ln2_b):
    bsz, seq, _ = x.shape
    m = bsz * seq
    rows = seq // GRID_W
    cos_t, sin_t = _rope_tables(seq)
    swa_bias = _band_bias_tables(rel_bias, 0, min(BAND_TILE, seq), SWA_HALF, 1)
    dil_bias = []
    for g, (window, r) in enumerate(DIL_CONFIGS):
        half = window // (2 * r)
        dil_bias.append(_band_bias_tables(rel_bias, SWA_HEADS + g * DIL_HEADS, min(BAND_TILE, seq // r), half, r))

    h, hb = _embed_layer_norm(x.reshape(m, D_MODEL), ln_emb_g, ln_emb_b)
    for i in range(DEPTH):
        w_main, w_dil = _prep_w_in(w_in[i])
        z3 = _matmul(hb, w_main, 1024, 768, BF16, "in_proj").reshape(bsz, seq, Z_COLS)
        hb3 = hb.reshape(bsz, seq, D_MODEL)
        q_a, k_a, v_a = _mla_prep(z3, cos_t, sin_t, mla_q_norm[i], mla_kv_norm[i],
                                  _prep_mla_weights(mla_w_uq[i], mla_w_ukv[i]))
        y_a = _mla_flash(q_a, k_a, v_a)
        (y_b,) = _banded_attention(z3.reshape(bsz, 1, seq, Z_COLS), B_BLK, swa_bias, swa_sink[i],
                                   half=SWA_HALF, tile=BAND_TILE, want_lse=False)
        y_c = _neighborhood_attention(z3, _na_bias_tables(na_rpb[i], rows))
        o_d, l_d = [], []
        for g, (window, r) in enumerate(DIL_CONFIGS):
            zd = _matmul_fold(hb3, w_dil[g], r, 1024, f"in_proj_r{r}")
            o_g, l_g = _banded_attention(zd, 0, dil_bias[g], None, half=window // (2 * r),
                                         tile=BAND_TILE, want_lse=True)
            o_d.append(o_g)
            l_d.append(l_g)
        wa = jnp.pad(w_branch[i, 0].reshape(MLA_HEADS, MLA_V, D_MODEL),
                     ((0, 0), (0, MLA_HEAD_PAD - MLA_V), (0, 0))).reshape(MLA_W, D_MODEL).astype(BF16)
        h, hb = _merge(z3, y_a, y_b, y_c, o_d, l_d, h.reshape(bsz, seq, D_MODEL),
                       wa, w_branch[i, 1:].astype(BF16), w_out[i].astype(BF16), ln1_g[i], ln1_b[i])
        h, hb = _ffn(h.reshape(m, D_MODEL), hb.reshape(m, D_MODEL), p[i].reshape(m, PLE_DIM),
                     w_ff1[i].astype(BF16), w_ff2[i].astype(BF16),
                     w_ple[i].astype(BF16), w_ple_gate[i].astype(BF16), ln2_g[i], ln2_b[i])
    return h.reshape(bsz, seq, D_MODEL)
```

```python
import functools
import math

import jax
import jax.numpy as jnp
import numpy as np
from jax import lax
from jax.experimental import pallas as pl
from jax.experimental.pallas import tpu as pltpu

F32 = jnp.float32
BF16 = jnp.bfloat16

D_MODEL = 1024
DEPTH = 2
HEAD_DIM = 64
GRID_W = 64
NEG_INF = -1e30

MLA_HEADS = 4
MLA_NOPE = 64
MLA_ROPE = 32
MLA_V = 64
Q_LORA = 256
KV_LORA = 128
ROPE_THETA = 10000.0

SWA_HEADS = 4
SWA_KV_HEADS = 2
SWA_HALF = 128

NA_HEADS = 4
NA_KH = 8
NA_KW = 16

DIL_CONFIGS = ((128, 1), (512, 4), (2048, 16))
DIL_GROUPS = 3
DIL_HEADS = 4

T5_BUCKETS = 32
T5_MAX_DIST = 1024

N_BRANCH = 4
BRANCH_W = 256
D_FF = 4 * D_MODEL
PLE_DIM = 256

ALPHA = (2 * DEPTH) ** 0.25

LANES = 128
HEADS = 4
HEAD_BLOCK = HEADS * HEAD_DIM

GATE_COLS = N_BRANCH * D_MODEL
A_COL = GATE_COLS
A_WIDTH = 512
B_BLK = (A_COL + A_WIDTH) // HEAD_BLOCK
C_BLK = B_BLK + 3
Z_BLKS = C_BLK + 3
Z_COLS = Z_BLKS * HEAD_BLOCK
QKV_COLS = 3 * HEAD_BLOCK

MLA_HEAD_PAD = 128
MLA_W = MLA_HEADS * MLA_HEAD_PAD
LOG2E = math.log2(math.e)

VMEM_LIMIT = 56 * 1024 * 1024


def _cparams(sem):
    return pltpu.CompilerParams(dimension_semantics=sem, vmem_limit_bytes=VMEM_LIMIT)


def _layer_norm(u, g, b, eps=1e-5):
    mu = jnp.mean(u, axis=-1, keepdims=True)
    d = u - mu
    var = jnp.mean(d * d, axis=-1, keepdims=True)
    return d * lax.rsqrt(var + eps) * g + b


def _rms_norm(u, g, eps=1e-6):
    return u * lax.rsqrt(jnp.mean(u * u, axis=-1, keepdims=True) + eps) * g


def _ln_kernel(x_ref, g_ref, b_ref, h_ref, hb_ref):
    y = _layer_norm(x_ref[...], g_ref[...], b_ref[...])
    h_ref[...] = y
    hb_ref[...] = y.astype(BF16)


def _embed_layer_norm(x2, g, b, tm=512):
    m, d = x2.shape
    row = pl.BlockSpec((tm, d), lambda i: (i, 0))
    vec = pl.BlockSpec((1, d), lambda i: (0, 0))
    return pl.pallas_call(
        _ln_kernel,
        grid=(m // tm,),
        in_specs=[row, vec, vec],
        out_specs=[row, row],
        out_shape=[jax.ShapeDtypeStruct((m, d), F32), jax.ShapeDtypeStruct((m, d), BF16)],
        compiler_params=_cparams(("parallel",)),
        name="embed_ln",
    )(x2, g.reshape(1, d), b.reshape(1, d))


def _mm_kernel(a_ref, w_ref, o_ref):
    o_ref[...] = jnp.dot(a_ref[...], w_ref[...], preferred_element_type=F32).astype(o_ref.dtype)


def _matmul(a, w, tm, tn, out_dtype, name):
    m, k = a.shape
    n = w.shape[1]
    return pl.pallas_call(
        _mm_kernel,
        grid=(m // tm, n // tn),
        in_specs=[pl.BlockSpec((tm, k), lambda i, j: (i, 0)),
                  pl.BlockSpec((k, tn), lambda i, j: (0, j))],
        out_specs=pl.BlockSpec((tm, tn), lambda i, j: (i, j)),
        out_shape=jax.ShapeDtypeStruct((m, n), out_dtype),
        compiler_params=_cparams(("parallel", "arbitrary")),
        name=name,
    )(a, w)


def _prep_w_in(w):
    o = 0
    a = w[:, o:o + Q_LORA + KV_LORA + MLA_ROPE]
    o += Q_LORA + KV_LORA + MLA_ROPE
    bq = w[:, o:o + HEAD_BLOCK]
    o += HEAD_BLOCK
    kvw = SWA_KV_HEADS * HEAD_DIM
    bk = w[:, o:o + kvw]
    o += kvw
    bv = w[:, o:o + kvw]
    o += kvw
    c = w[:, o:o + 3 * HEAD_BLOCK]
    o += 3 * HEAD_BLOCK
    d = w[:, o:o + 3 * DIL_GROUPS * HEAD_BLOCK]
    o += 3 * DIL_GROUPS * HEAD_BLOCK
    gates = w[:, o:]
    rep = SWA_HEADS // SWA_KV_HEADS

    def expand(t):
        t = t.reshape(D_MODEL, SWA_KV_HEADS, 1, HEAD_DIM)
        return jnp.broadcast_to(t, (D_MODEL, SWA_KV_HEADS, rep, HEAD_DIM)).reshape(D_MODEL, HEAD_BLOCK)

    a = jnp.pad(a, ((0, 0), (0, A_WIDTH - a.shape[1])))
    d = d.reshape(D_MODEL, 3, DIL_GROUPS, HEAD_BLOCK).transpose(2, 0, 1, 3).reshape(DIL_GROUPS, D_MODEL, QKV_COLS)
    return jnp.concatenate([0.5 * gates, a, bq, expand(bk), expand(bv), c], axis=1).astype(BF16), d.astype(BF16)


def _mm_fold_kernel(a_ref, w_ref, o_ref, acc_ref, *, dilation):
    acc = jnp.dot(a_ref[...], w_ref[...], preferred_element_type=F32)
    nblk, tm, _ = acc_ref.shape
    for c in range(nblk):
        acc_ref[c] = acc[:, c * LANES:(c + 1) * LANES]
    n = tm // dilation
    for j in range(dilation):
        for c in range(nblk):
            o_ref[j, :, c * LANES:(c + 1) * LANES] = (
                acc_ref[c, pl.ds(j, n, stride=dilation), :].astype(o_ref.dtype))


def _matmul_fold(a3, w, dilation, tm, name):
    bsz, seq, k = a3.shape
    n = w.shape[1]
    return pl.pallas_call(
        functools.partial(_mm_fold_kernel, dilation=dilation),
        grid=(bsz, seq // tm),
        in_specs=[pl.BlockSpec((None, tm, k), lambda b, i: (b, i, 0)),
                  pl.BlockSpec((k, n), lambda b, i: (0, 0))],
        out_specs=pl.BlockSpec((None, dilation, tm // dilation, n), lambda b, i: (b, 0, i, 0)),
        out_shape=jax.ShapeDtypeStruct((bsz, dilation, seq // dilation, n), BF16),
        scratch_shapes=[pltpu.VMEM((n // LANES, tm, LANES), F32)],
        compiler_params=_cparams(("parallel", "parallel")),
        name=name,
    )(a3, w)


def _rope_tables(seq):
    half = MLA_ROPE // 2
    inv = ROPE_THETA ** (-jnp.arange(half, dtype=F32) / half)
    ang = jnp.arange(seq).astype(F32)[:, None] * inv[None, :]
    cos, sin = jnp.cos(ang), jnp.sin(ang)
    pad = MLA_HEAD_PAD - MLA_NOPE - MLA_ROPE
    cos_t = jnp.concatenate([jnp.ones((seq, MLA_NOPE), F32), cos, cos, jnp.zeros((seq, pad), F32)], axis=1)
    sin_t = jnp.concatenate([jnp.zeros((seq, MLA_NOPE), F32), sin, sin, jnp.zeros((seq, pad), F32)], axis=1)
    return cos_t, sin_t


def _rot_cols(w_rope):
    half = MLA_ROPE // 2
    return jnp.concatenate([-w_rope[..., half:], w_rope[..., :half]], axis=-1)


def _prep_mla_weights(w_uq, w_ukv):
    pad = MLA_HEAD_PAD - MLA_NOPE - MLA_ROPE
    wq = w_uq.reshape(Q_LORA, MLA_HEADS, MLA_NOPE + MLA_ROPE)
    wq_main = jnp.pad(wq, ((0, 0), (0, 0), (0, pad))).reshape(Q_LORA, MLA_W)
    wq_rot = jnp.concatenate([jnp.zeros((Q_LORA, MLA_HEADS, MLA_NOPE), F32), _rot_cols(wq[..., MLA_NOPE:]),
                              jnp.zeros((Q_LORA, MLA_HEADS, pad), F32)], axis=-1).reshape(Q_LORA, MLA_W)
    wkv = w_ukv.reshape(KV_LORA, MLA_HEADS, MLA_NOPE + MLA_V)
    wk_n = jnp.pad(wkv[..., :MLA_NOPE], ((0, 0), (0, 0), (0, MLA_HEAD_PAD - MLA_NOPE))).reshape(KV_LORA, MLA_W)
    wv = jnp.pad(wkv[..., MLA_NOPE:], ((0, 0), (0, 0), (0, MLA_HEAD_PAD - MLA_V))).reshape(KV_LORA, MLA_W)
    eye = jnp.eye(MLA_ROPE, dtype=F32)
    place = jnp.concatenate([jnp.zeros((MLA_ROPE, MLA_NOPE), F32), eye, jnp.zeros((MLA_ROPE, pad), F32)], axis=1)
    place_rot = jnp.concatenate([jnp.zeros((MLA_ROPE, MLA_NOPE), F32), _rot_cols(eye),
                                 jnp.zeros((MLA_ROPE, pad), F32)], axis=1)

    def lift(pm):
        pm = jnp.pad(pm, ((0, LANES - MLA_ROPE), (0, 0)))
        return jnp.tile(pm, (1, MLA_HEADS))

    wk_main = jnp.concatenate([wk_n, lift(place)], axis=0)
    wk_rot = jnp.concatenate([jnp.zeros((KV_LORA, MLA_W), F32), lift(place_rot)], axis=0)
    return tuple(t.astype(BF16) for t in (wq_main, wq_rot, wk_main, wk_rot, wv))


def _mla_prep_kernel(za_ref, cos_ref, sin_ref, gq_ref, gkv_ref, wq_ref, wqr_ref, wk_ref, wkr_ref, wv_ref,
                     q_ref, k_ref, vt_ref):
    za = za_ref[...]
    cq = _rms_norm(za[:, :Q_LORA].astype(F32), gq_ref[...]).astype(BF16)
    ckv = _rms_norm(za[:, Q_LORA:Q_LORA + KV_LORA].astype(F32), gkv_ref[...]).astype(BF16)
    cos = jnp.concatenate([cos_ref[...]] * MLA_HEADS, axis=1)
    sin = jnp.concatenate([sin_ref[...]] * MLA_HEADS, axis=1)
    dot = functools.partial(jnp.dot, preferred_element_type=F32)
    q = dot(cq, wq_ref[...]) * cos + dot(cq, wqr_ref[...]) * sin
    q_ref[...] = (q * ((MLA_NOPE + MLA_ROPE) ** -0.5 * LOG2E)).astype(BF16)
    lhs = jnp.concatenate([ckv, za[:, Q_LORA + KV_LORA:]], axis=1)
    k_ref[...] = (dot(lhs, wk_ref[...]) * cos + dot(lhs, wkr_ref[...]) * sin).astype(BF16)
    lane = lax.broadcasted_iota(jnp.int32, (1, MLA_W), 1)
    ones_col = jnp.where(lane % MLA_HEAD_PAD == MLA_V, 1.0, 0.0)
    vt_ref[...] = (dot(ckv, wv_ref[...]) + ones_col).T.astype(BF16)


def _mla_prep(z3, cos_t, sin_t, g_q, g_kv, weights, tm=512):
    bsz, seq, _ = z3.shape
    wq, wqr, wk, wkr, wv = weights

    def full(t):
        return pl.BlockSpec(t.shape, lambda b, i: (0,) * t.ndim)

    tab = pl.BlockSpec((tm, MLA_HEAD_PAD), lambda b, i: (i, 0))
    out = pl.BlockSpec((None, tm, MLA_W), lambda b, i: (b, i, 0))
    g_q = g_q.reshape(1, Q_LORA)
    g_kv = g_kv.reshape(1, KV_LORA)
    shp = jax.ShapeDtypeStruct((bsz, seq, MLA_W), BF16)
    return pl.pallas_call(
        _mla_prep_kernel,
        grid=(bsz, seq // tm),
        in_specs=[pl.BlockSpec((None, tm, A_WIDTH), lambda b, i: (b, i, A_COL // A_WIDTH)),
                  tab, tab, full(g_q), full(g_kv), full(wq), full(wqr), full(wk), full(wkr), full(wv)],
        out_specs=[out, out, pl.BlockSpec((None, MLA_W, tm), lambda b, i: (b, 0, i))],
        out_shape=[shp, shp, jax.ShapeDtypeStruct((bsz, MLA_W, seq), BF16)],
        compiler_params=_cparams(("parallel", "parallel")),
        name="mla_prep",
    )(z3, cos_t, sin_t, g_q, g_kv, wq, wqr, wk, wkr, wv)


def _mla_flash_kernel(q_ref, k_ref, vt_ref, o_ref, m_scr, acc_scr, s_scr, p_scr):
    ki = pl.program_id(2)

    @pl.when(ki == 0)
    def _():
        m_scr[...] = jnp.full(m_scr.shape, -jnp.inf, F32)
        acc_scr[...] = jnp.zeros(acc_scr.shape, F32)

    heads = [slice(h * MLA_HEAD_PAD, (h + 1) * MLA_HEAD_PAD) for h in range(MLA_HEADS)]
    for h, sl in enumerate(heads):
        s_scr[h] = lax.dot_general(k_ref[:, sl], q_ref[:, sl], (((1,), (1,)), ((), ())),
                                   preferred_element_type=F32)
    alphas = []
    for h in range(MLA_HEADS):
        s = s_scr[h]
        m_prev = m_scr[h]
        m_new = jnp.maximum(m_prev, jnp.max(s, axis=0, keepdims=True))
        p_scr[h] = jnp.exp2(s - m_new).astype(BF16)
        alphas.append(jnp.exp2(m_prev - m_new))
        m_scr[h] = m_new
    for h, sl in enumerate(heads):
        acc_scr[h] = alphas[h] * acc_scr[h] + jnp.dot(vt_ref[sl, :], p_scr[h], preferred_element_type=F32)

    @pl.when(ki == pl.num_programs(2) - 1)
    def _():
        for h, sl in enumerate(heads):
            acc = acc_scr[h]
            o_ref[:, sl] = (acc / acc[MLA_V:MLA_V + 1, :]).T.astype(o_ref.dtype)


def _mla_flash(q, k, vt, tq=512, tk=1024):
    bsz, seq, _ = q.shape
    tq, tk = min(tq, seq), min(tk, seq)
    return pl.pallas_call(
        _mla_flash_kernel,
        grid=(bsz, seq // tq, seq // tk),
        in_specs=[pl.BlockSpec((None, tq, MLA_W), lambda b, i, j: (b, i, 0)),
                  pl.BlockSpec((None, tk, MLA_W), lambda b, i, j: (b, j, 0)),
                  pl.BlockSpec((None, MLA_W, tk), lambda b, i, j: (b, 0, j))],
        out_specs=pl.BlockSpec((None, tq, MLA_W), lambda b, i, j: (b, i, 0)),
        out_shape=jax.ShapeDtypeStruct((bsz, seq, MLA_W), BF16),
        scratch_shapes=[pltpu.VMEM((MLA_HEADS, 1, tq), F32),
                        pltpu.VMEM((MLA_HEADS, MLA_HEAD_PAD, tq), F32),
                        pltpu.VMEM((MLA_HEADS, tk, tq), F32),
                        pltpu.VMEM((MLA_HEADS, tk, tq), BF16)],
        compiler_params=_cparams(("parallel", "parallel", "arbitrary")),
        name="mla_flash",
    )(q, k, vt)


def _t5_bucket(rel):
    nb = T5_BUCKETS // 2
    max_exact = nb // 2
    n = jnp.abs(rel)
    large = max_exact + (jnp.log(jnp.maximum(n, 1).astype(F32) / max_exact)
                         / math.log(T5_MAX_DIST / max_exact) * (nb - max_exact)).astype(jnp.int32)
    large = jnp.minimum(large, nb - 1)
    return jnp.where(rel > 0, nb, 0) + jnp.where(n < max_exact, n, large)


def _band_bias_tables(rel_bias, head_lo, tile, half, dilation):
    width = tile + 2 * half
    col = jnp.arange(width)[None, :]
    rel = col - half - jnp.arange(tile)[:, None]
    onehot = jax.nn.one_hot(_t5_bucket(rel * dilation), T5_BUCKETS, dtype=F32)
    bias = jnp.einsum("qkn,nh->hqk", onehot, rel_bias[:, head_lo:head_lo + HEADS].astype(F32),
                      precision=lax.Precision.HIGHEST)
    band = jnp.abs(rel) <= half
    left_ok = col >= half
    right_ok = col < tile + half
    masks = jnp.stack([band, band & left_ok, band & right_ok, band & left_ok & right_ok])
    return jnp.where(masks[:, None], bias[None].astype(F32), NEG_INF)


def _head_of_lane():
    return lax.broadcasted_iota(jnp.int32, (1, HEAD_BLOCK), 1) // HEAD_DIM


def _per_head_lanes(cols):
    head = _head_of_lane()
    out = cols[HEADS - 1]
    for h in range(HEADS - 2, -1, -1):
        out = jnp.where(head == h, cols[h], out)
    return out


def _banded_kernel(*refs, tile, subtiles, half, length, use_sink, want_lse):
    refs = list(refs)
    sink_ref = refs.pop(0) if use_sink else None
    q_ref, k_ref, v_ref, bias_ref, o_ref = refs[:5]
    lse_ref = refs[5] if want_lse else None
    i = pl.program_id(2)
    nt = length // tile
    width = tile + 2 * half
    head = _head_of_lane()
    for u in range(subtiles):
        t = i * subtiles + u
        main = t * tile
        can_first, can_last = u == 0, u == subtiles - 1
        if can_first or can_last:
            left = jnp.maximum(main - half, 0) if can_first else main - half
            right = jnp.minimum(main + tile, length - half) if can_last else main + tile
            left, right = pl.multiple_of(left, half), pl.multiple_of(right, half)
            main = pl.multiple_of(main, tile)

            def window(ref):
                return jnp.concatenate([ref[pl.ds(left, half), :], ref[pl.ds(main, tile), :],
                                        ref[pl.ds(right, half), :]], axis=0)

            variant = 0
            if can_first:
                variant = variant + (t == 0).astype(jnp.int32)
            if can_last:
                variant = variant + 2 * (t == nt - 1).astype(jnp.int32)
        else:
            start = pl.multiple_of(main - half, half)

            def window(ref):
                return ref[pl.ds(start, width), :]

            variant = 0
        rows = slice(u * tile, (u + 1) * tile)
        kwin, vwin = window(k_ref), window(v_ref)
        q = q_ref[rows, :] * (HEAD_DIM ** -0.5)
        acc = None
        inv_den, lse = [], []
        for h in range(HEADS):
            mine = head == h
            s = lax.dot_general(q, jnp.where(mine, kwin, 0), (((1,), (1,)), ((), ())),
                                preferred_element_type=F32)
            s = s + bias_ref[variant, h]
            m = jnp.max(s, axis=1, keepdims=True)
            if use_sink:
                m = jnp.maximum(m, sink_ref[h])
            e = jnp.exp(s - m)
            den = jnp.sum(e, axis=1, keepdims=True)
            if use_sink:
                den = den + jnp.exp(sink_ref[h] - m)
            part = jnp.dot(e.astype(BF16), jnp.where(mine, vwin, 0), preferred_element_type=F32)
            acc = part if acc is None else acc + part
            inv_den.append(1.0 / den)
            if want_lse:
                lse.append(m + jnp.log(den))
        o_ref[rows, :] = (acc * _per_head_lanes(inv_den)).astype(o_ref.dtype)
        if want_lse:
            lse_ref[rows, :] = _per_head_lanes(lse)


def _banded_attention(zf, blk, bias, sink, *, half, tile, want_lse):
    bsz, dilation, length, _ = zf.shape
    tile = min(tile, length)
    subtiles = min(BAND_SUBTILES, length // tile)
    step = tile * subtiles
    use_sink = sink is not None
    in_specs = [pl.BlockSpec((None, None, step, HEAD_BLOCK), lambda b, j, i: (b, j, i, blk)),
                pl.BlockSpec((None, None, length, HEAD_BLOCK), lambda b, j, i: (b, j, 0, blk + 1)),
                pl.BlockSpec((None, None, length, HEAD_BLOCK), lambda b, j, i: (b, j, 0, blk + 2)),
                pl.BlockSpec(bias.shape, lambda b, j, i: (0, 0, 0, 0))]
    args = [zf, zf, zf, bias]
    if use_sink:
        in_specs.insert(0, pl.BlockSpec(memory_space=pltpu.SMEM))
        args.insert(0, sink.astype(F32))
    o_spec = pl.BlockSpec((None, None, step, HEAD_BLOCK), lambda b, j, i: (b, j, i, 0))
    out_specs = [o_spec]
    out_shape = [jax.ShapeDtypeStruct((bsz, dilation, length, HEAD_BLOCK), BF16)]
    if want_lse:
        out_specs.append(o_spec)
        out_shape.append(jax.ShapeDtypeStruct((bsz, dilation, length, HEAD_BLOCK), F32))
    return pl.pallas_call(
        functools.partial(_banded_kernel, tile=tile, subtiles=subtiles, half=half, length=length,
                          use_sink=use_sink, want_lse=want_lse),
        grid=(bsz, dilation, length // step),
        in_specs=in_specs,
        out_specs=out_specs,
        out_shape=out_shape,
        compiler_params=_cparams(("parallel", "parallel", "arbitrary")),
        name=f"banded_r{dilation}",
    )(*args)


NA_TILE_ROWS = 4
NA_WIN_ROWS = NA_TILE_ROWS + NA_KH


def _na_bias_tables(rpb, rows):
    tr, wr = NA_TILE_ROWS, NA_WIN_ROWS
    exact = functools.partial(jnp.einsum, precision=lax.Precision.HIGHEST)
    c = np.arange(GRID_W)
    cs = np.clip(c - NA_KW // 2, 0, GRID_W - NA_KW)
    col_ok = (c[None, :] >= cs[:, None]) & (c[None, :] < cs[:, None] + NA_KW)
    dc = np.clip(c[None, :] - c[:, None], -(NA_KW - 1), NA_KW - 1) + (NA_KW - 1)
    by_col = exact("hdn,qkn->hdqk", rpb.astype(F32), np.eye(2 * NA_KW - 1, dtype=np.float32)[dc])
    tabs = []
    for r0, ws in ((0, 0), (tr, 0), (rows - tr, rows - wr)):
        r = r0 + np.arange(tr)
        start = np.clip(r - NA_KH // 2, 0, rows - NA_KH)
        kr = ws + np.arange(wr)
        row_ok = (kr[None, :] >= start[:, None]) & (kr[None, :] < start[:, None] + NA_KH)
        dr = np.clip(kr[None, :] - r[:, None] + (NA_KH - 1), 0, 2 * NA_KH - 2)
        bias = exact("rad,hdqk->hrqak", np.eye(2 * NA_KH - 1, dtype=np.float32)[dr], by_col)
        ok = row_ok[:, None, :, None] & col_ok[None, :, None, :]
        tabs.append(jnp.where(ok[None], bias, NEG_INF).reshape(HEADS, tr * GRID_W, wr * GRID_W))
    return jnp.stack(tabs)


def _na_kernel(q_ref, k_ref, v_ref, bias_ref, o_ref, *, rows):
    i = pl.program_id(1)
    nt = rows // NA_TILE_ROWS
    win = NA_WIN_ROWS * GRID_W
    ws = jnp.clip(i * NA_TILE_ROWS - NA_KH // 2, 0, rows - NA_WIN_ROWS) * GRID_W
    ws = pl.multiple_of(ws, GRID_W)
    variant = jnp.where(i == 0, 0, jnp.where(i == nt - 1, 2, 1))
    kwin = k_ref[pl.ds(ws, win), :]
    vwin = v_ref[pl.ds(ws, win), :]
    q = q_ref[...] * (HEAD_DIM ** -0.5)
    head = _head_of_lane()
    acc = None
    inv_den = []
    for h in range(HEADS):
        mine = head == h
        s = lax.dot_general(q, jnp.where(mine, kwin, 0), (((1,), (1,)), ((), ())),
                            preferred_element_type=F32)
        s = s + bias_ref[variant, h]
        m = jnp.max(s, axis=1, keepdims=True)
        e = jnp.exp(s - m)
        inv_den.append(1.0 / jnp.sum(e, axis=1, keepdims=True))
        part = jnp.dot(e.astype(BF16), jnp.where(mine, vwin, 0), preferred_element_type=F32)
        acc = part if acc is None else acc + part
    o_ref[...] = (acc * _per_head_lanes(inv_den)).astype(o_ref.dtype)


def _neighborhood_attention(z3, bias):
    bsz, seq, _ = z3.shape
    rows = seq // GRID_W
    tq = NA_TILE_ROWS * GRID_W
    return pl.pallas_call(
        functools.partial(_na_kernel, rows=rows),
        grid=(bsz, rows // NA_TILE_ROWS),
        in_specs=[pl.BlockSpec((None, tq, HEAD_BLOCK), lambda b, i: (b, i, C_BLK)),
                  pl.BlockSpec((None, seq, HEAD_BLOCK), lambda b, i: (b, 0, C_BLK + 1)),
                  pl.BlockSpec((None, seq, HEAD_BLOCK), lambda b, i: (b, 0, C_BLK + 2)),
                  pl.BlockSpec(bias.shape, lambda b, i: (0, 0, 0, 0))],
        out_specs=pl.BlockSpec((None, tq, HEAD_BLOCK), lambda b, i: (b, i, 0)),
        out_shape=jax.ShapeDtypeStruct((bsz, seq, HEAD_BLOCK), BF16),
        compiler_params=_cparams(("parallel", "arbitrary")),
        name="neighborhood",
    )(z3, z3, z3, bias)


def _unfold(ref, scr):
    dilation, n, cols = ref.shape
    if dilation == 1:
        return ref[0].astype(F32)
    nblk = cols // LANES
    for j in range(dilation):
        for c in range(nblk):
            scr[c, pl.ds(j, n, stride=dilation), :] = ref[j, :, c * LANES:(c + 1) * LANES].astype(F32)
    return jnp.concatenate([scr[c] for c in range(nblk)], axis=1)


def _merge_kernel(gate_ref, ya_ref, yb_ref, yc_ref, o0_ref, o1_ref, o2_ref, l0_ref, l1_ref, l2_ref, h_ref,
                  wa_ref, wb_ref, wo_ref, g_ref, b_ref, h_out_ref, hb_out_ref, *scr):
    dot = functools.partial(jnp.dot, preferred_element_type=F32)
    l0, l1, l2 = _unfold(l0_ref, None), _unfold(l1_ref, scr[0]), _unfold(l2_ref, scr[1])
    o0, o1, o2 = _unfold(o0_ref, None), _unfold(o1_ref, scr[2]), _unfold(o2_ref, scr[3])
    lm = jnp.maximum(jnp.maximum(l0, l1), l2)
    e0, e1, e2 = jnp.exp(l0 - lm), jnp.exp(l1 - lm), jnp.exp(l2 - lm)
    yd = (e0 * o0 + e1 * o1 + e2 * o2) / (e0 + e1 + e2)
    branches = (dot(ya_ref[...], wa_ref[...]), dot(yb_ref[0], wb_ref[0]), dot(yc_ref[...], wb_ref[1]),
                dot(yd.astype(BF16), wb_ref[2]))
    merged = None
    for n in range(N_BRANCH):
        gate2 = 1.0 + jnp.tanh(gate_ref[:, n * D_MODEL:(n + 1) * D_MODEL].astype(F32))
        term = gate2 * branches[n]
        merged = term if merged is None else merged + term
    u = ALPHA * h_ref[...] + dot(merged.astype(BF16), wo_ref[...])
    y = _layer_norm(u, g_ref[...], b_ref[...])
    h_out_ref[...] = y
    hb_out_ref[...] = y.astype(BF16)


def _merge(z3, ya, yb, yc, od, ld, h3, wa, wb, wo, g, b, tm=256):
    bsz, seq, _ = h3.shape

    def rows(width):
        return pl.BlockSpec((None, tm, width), lambda b, i: (b, i, 0))

    def folded(t):
        dilation = t.shape[1]
        return pl.BlockSpec((None, dilation, tm // dilation, HEAD_BLOCK), lambda b, i: (b, 0, i, 0))

    def full(t):
        return pl.BlockSpec(t.shape, lambda b, i: (0,) * t.ndim)

    g = g.reshape(1, D_MODEL)
    b = b.reshape(1, D_MODEL)
    return pl.pallas_call(
        _merge_kernel,
        grid=(bsz, seq // tm),
        in_specs=[rows(GATE_COLS), rows(MLA_W), folded(yb), rows(HEAD_BLOCK)]
        + [folded(t) for t in od] + [folded(t) for t in ld]
        + [rows(D_MODEL), full(wa), full(wb), full(wo), full(g), full(b)],
        out_specs=[rows(D_MODEL), rows(D_MODEL)],
        out_shape=[jax.ShapeDtypeStruct((bsz, seq, D_MODEL), F32), jax.ShapeDtypeStruct((bsz, seq, D_MODEL), BF16)],
        scratch_shapes=[pltpu.VMEM((HEAD_BLOCK // LANES, tm, LANES), F32)] * 4,
        compiler_params=_cparams(("parallel", "parallel")),
        name="merge",
    )(z3, ya, yb, yc, *od, *ld, h3, wa, wb, wo, g, b)


FF_CHUNK = 1024


def _ffn_kernel(h_ref, hb_ref, p_ref, w1_ref, w2_ref, wp_ref, wg_ref, g_ref, b_ref, h_out_ref, hb_out_ref):
    dot = functools.partial(jnp.dot, preferred_element_type=F32)
    hb = hb_ref[...]
    ff = None
    for c in range(D_FF // FF_CHUNK):
        sl = slice(c * FF_CHUNK, (c + 1) * FF_CHUNK)
        a = jnp.maximum(dot(hb, w1_ref[:, sl]), 0.0)
        part = dot((a * a).astype(BF16), w2_ref[sl, :])
        ff = part if ff is None else ff + part
    ple = dot(p_ref[...].astype(BF16), wp_ref[...]) * (1.0 + jnp.tanh(dot(hb, wg_ref[...])))
    y = _layer_norm(ALPHA * h_ref[...] + ff + ple, g_ref[...], b_ref[...])
    h_out_ref[...] = y
    hb_out_ref[...] = y.astype(BF16)


def _ffn(h, hb, p2, w1, w2, wp, wg, g, b, tm=256):
    m = h.shape[0]

    def rows(width):
        return pl.BlockSpec((tm, width), lambda i: (i, 0))

    def full(t):
        return pl.BlockSpec(t.shape, lambda i: (0,) * t.ndim)

    g = g.reshape(1, D_MODEL)
    b = b.reshape(1, D_MODEL)
    return pl.pallas_call(
        _ffn_kernel,
        grid=(m // tm,),
        in_specs=[rows(D_MODEL), rows(D_MODEL), rows(PLE_DIM), full(w1), full(w2), full(wp), full(wg),
                  full(g), full(b)],
        out_specs=[rows(D_MODEL), rows(D_MODEL)],
        out_shape=[jax.ShapeDtypeStruct((m, D_MODEL), F32), jax.ShapeDtypeStruct((m, D_MODEL), BF16)],
        compiler_params=_cparams(("parallel",)),
        name="ffn",
    )(h, hb, p2, w1, w2, wp, wg, g, b)


BAND_TILE = 128
BAND_SUBTILES = 4


def kernel(x, p, ln_emb_g, ln_emb_b, rel_bias, w_in, mla_q_norm, mla_w_uq, mla_kv_norm, mla_w_ukv, swa_sink,
           na_rpb, w_branch, w_out, ln1_g, ln1_b, w_ff1, w_ff2, w_ple, w_ple_gate, ln2_g, ln2_b):
    bsz, seq, _ = x.shape
    m = bsz * seq
    rows = seq // GRID_W
    cos_t, sin_t = _rope_tables(seq)
    swa_bias = _band_bias_tables(rel_bias, 0, min(BAND_TILE, seq), SWA_HALF, 1)
    dil_bias = []
    for g, (window, r) in enumerate(DIL_CONFIGS):
        half = window // (2 * r)
        dil_bias.append(_band_bias_tables(rel_bias, SWA_HEADS + g * DIL_HEADS, min(BAND_TILE, seq // r), half, r))

    h, hb = _embed_layer_norm(x.reshape(m, D_MODEL), ln_emb_g, ln_emb_b)
    for i in range(DEPTH):
        w_main, w_dil = _prep_w_in(w_in[i])
        z3 = _matmul(hb, w_main, 1024, 768, BF16, "in_proj").reshape(bsz, seq, Z_COLS)
        hb3 = hb.reshape(bsz, seq, D_MODEL)
        q_a, k_a, v_a = _mla_prep(z3, cos_t, sin_t, mla_q_norm[i], mla_kv_norm[i],
                                  _prep_mla_weights(mla_w_uq[i], mla_w_ukv[i]))
        y_a = _mla_flash(q_a, k_a, v_a)
        (y_b,) = _banded_attention(z3.reshape(bsz, 1, seq, Z_COLS), B_BLK, swa_bias, swa_sink[i],
                                   half=SWA_HALF, tile=BAND_TILE, want_lse=False)
        y_c = _neighborhood_attention(z3, _na_bias_tables(na_rpb[i], rows))
        o_d, l_d = [], []
        for g, (window, r) in enumerate(DIL_CONFIGS):
            zd = _matmul_fold(hb3, w_dil[g], r, 1024, f"in_proj_r{r}")
            o_g, l_g = _banded_attention(zd, 0, dil_bias[g], None, half=window // (2 * r),
                                         tile=BAND_TILE, want_lse=True)
            o_d.append(o_g)
            l_d.append(l_g)
        wa = jnp.pad(w_branch[i, 0].reshape(MLA_HEADS, MLA_V, D_MODEL),
                     ((0, 0), (0, MLA_HEAD_PAD - MLA_V), (0, 0))).reshape(MLA_W, D_MODEL).astype(BF16)
        h, hb = _merge(z3, y_a, y_b, y_c, o_d, l_d, h.reshape(bsz, seq, D_MODEL),
                       wa, w_branch[i, 1:].astype(BF16), (0.5 * w_out[i]).astype(BF16), ln1_g[i], ln1_b[i])
        h, hb = _ffn(h.reshape(m, D_MODEL), hb.reshape(m, D_MODEL), p[i].reshape(m, PLE_DIM),
                     w_ff1[i].astype(BF16), w_ff2[i].astype(BF16),
                     (0.5 * w_ple[i]).astype(BF16), (0.5 * w_ple_gate[i]).astype(BF16), ln2_g[i], ln2_b[i])
    return h.reshape(bsz, seq, D_MODEL)
```

```python
import functools
import math

import jax
import jax.numpy as jnp
import numpy as np
from jax import lax
from jax.experimental import pallas as pl
from jax.experimental.pallas import tpu as pltpu

F32 = jnp.float32
BF16 = jnp.bfloat16

D_MODEL = 1024
DEPTH = 2
HEAD_DIM = 64
GRID_W = 64
NEG_INF = -1e30

MLA_HEADS = 4
MLA_NOPE = 64
MLA_ROPE = 32
MLA_V = 64
Q_LORA = 256
KV_LORA = 128
ROPE_THETA = 10000.0

SWA_HEADS = 4
SWA_KV_HEADS = 2
SWA_HALF = 128

NA_HEADS = 4
NA_KH = 8
NA_KW = 16

DIL_CONFIGS = ((128, 1), (512, 4), (2048, 16))
DIL_GROUPS = 3
DIL_HEADS = 4

T5_BUCKETS = 32
T5_MAX_DIST = 1024

N_BRANCH = 4
BRANCH_W = 256
D_FF = 4 * D_MODEL
PLE_DIM = 256

ALPHA = (2 * DEPTH) ** 0.25

LANES = 128
HEADS = 4
HEAD_BLOCK = HEADS * HEAD_DIM

GATE_COLS = N_BRANCH * D_MODEL
A_COL = GATE_COLS
A_WIDTH = 512
B_BLK = (A_COL + A_WIDTH) // HEAD_BLOCK
C_BLK = B_BLK + 3
Z_BLKS = C_BLK + 3
Z_COLS = Z_BLKS * HEAD_BLOCK
QKV_COLS = 3 * HEAD_BLOCK

MLA_HEAD_PAD = 128
MLA_W = MLA_HEADS * MLA_HEAD_PAD
LOG2E = math.log2(math.e)

VMEM_LIMIT = 56 * 1024 * 1024


def _cparams(sem):
    return pltpu.CompilerParams(dimension_semantics=sem, vmem_limit_bytes=VMEM_LIMIT)


def _layer_norm(u, g, b, eps=1e-5):
    mu = jnp.mean(u, axis=-1, keepdims=True)
    d = u - mu
    var = jnp.mean(d * d, axis=-1, keepdims=True)
    return d * lax.rsqrt(var + eps) * g + b


def _rms_norm(u, g, eps=1e-6):
    return u * lax.rsqrt(jnp.mean(u * u, axis=-1, keepdims=True) + eps) * g


def _ln_kernel(x_ref, g_ref, b_ref, h_ref, hb_ref):
    y = _layer_norm(x_ref[...], g_ref[...], b_ref[...])
    h_ref[...] = y
    hb_ref[...] = y.astype(BF16)


def _embed_layer_norm(x2, g, b, tm=512):
    m, d = x2.shape
    row = pl.BlockSpec((tm, d), lambda i: (i, 0))
    vec = pl.BlockSpec((1, d), lambda i: (0, 0))
    return pl.pallas_call(
        _ln_kernel,
        grid=(m // tm,),
        in_specs=[row, vec, vec],
        out_specs=[row, row],
        out_shape=[jax.ShapeDtypeStruct((m, d), F32), jax.ShapeDtypeStruct((m, d), BF16)],
        compiler_params=_cparams(("parallel",)),
        name="embed_ln",
    )(x2, g.reshape(1, d), b.reshape(1, d))


def _mm_kernel(a_ref, w_ref, o_ref):
    o_ref[...] = jnp.dot(a_ref[...], w_ref[...], preferred_element_type=F32).astype(o_ref.dtype)


def _matmul(a, w, tm, tn, out_dtype, name):
    m, k = a.shape
    n = w.shape[1]
    return pl.pallas_call(
        _mm_kernel,
        grid=(m // tm, n // tn),
        in_specs=[pl.BlockSpec((tm, k), lambda i, j: (i, 0)),
                  pl.BlockSpec((k, tn), lambda i, j: (0, j))],
        out_specs=pl.BlockSpec((tm, tn), lambda i, j: (i, j)),
        out_shape=jax.ShapeDtypeStruct((m, n), out_dtype),
        compiler_params=_cparams(("parallel", "arbitrary")),
        name=name,
    )(a, w)


def _prep_w_in(w):
    o = 0
    a = w[:, o:o + Q_LORA + KV_LORA + MLA_ROPE]
    o += Q_LORA + KV_LORA + MLA_ROPE
    bq = w[:, o:o + HEAD_BLOCK]
    o += HEAD_BLOCK
    kvw = SWA_KV_HEADS * HEAD_DIM
    bk = w[:, o:o + kvw]
    o += kvw
    bv = w[:, o:o + kvw]
    o += kvw
    c = w[:, o:o + 3 * HEAD_BLOCK]
    o += 3 * HEAD_BLOCK
    d = w[:, o:o + 3 * DIL_GROUPS * HEAD_BLOCK]
    o += 3 * DIL_GROUPS * HEAD_BLOCK
    gates = w[:, o:]
    rep = SWA_HEADS // SWA_KV_HEADS

    def expand(t):
        t = t.reshape(D_MODEL, SWA_KV_HEADS, 1, HEAD_DIM)
        return jnp.broadcast_to(t, (D_MODEL, SWA_KV_HEADS, rep, HEAD_DIM)).reshape(D_MODEL, HEAD_BLOCK)

    a = jnp.pad(a, ((0, 0), (0, A_WIDTH - a.shape[1])))
    d = d.reshape(D_MODEL, 3, DIL_GROUPS, HEAD_BLOCK).transpose(2, 0, 1, 3).reshape(DIL_GROUPS, D_MODEL, QKV_COLS)
    return jnp.concatenate([0.5 * gates, a, bq, expand(bk), expand(bv), c], axis=1).astype(BF16), d.astype(BF16)


def _mm_fold_kernel(a_ref, w_ref, o_ref, acc_ref, *, dilation):
    acc = jnp.dot(a_ref[...], w_ref[...], preferred_element_type=F32)
    nblk, tm, _ = acc_ref.shape
    for c in range(nblk):
        acc_ref[c] = acc[:, c * LANES:(c + 1) * LANES]
    n = tm // dilation
    for j in range(dilation):
        for c in range(nblk):
            o_ref[j, :, c * LANES:(c + 1) * LANES] = (
                acc_ref[c, pl.ds(j, n, stride=dilation), :].astype(o_ref.dtype))


def _matmul_fold(a3, w, dilation, tm, name):
    bsz, seq, k = a3.shape
    n = w.shape[1]
    return pl.pallas_call(
        functools.partial(_mm_fold_kernel, dilation=dilation),
        grid=(bsz, seq // tm),
        in_specs=[pl.BlockSpec((None, tm, k), lambda b, i: (b, i, 0)),
                  pl.BlockSpec((k, n), lambda b, i: (0, 0))],
        out_specs=pl.BlockSpec((None, dilation, tm // dilation, n), lambda b, i: (b, 0, i, 0)),
        out_shape=jax.ShapeDtypeStruct((bsz, dilation, seq // dilation, n), BF16),
        scratch_shapes=[pltpu.VMEM((n // LANES, tm, LANES), F32)],
        compiler_params=_cparams(("parallel", "parallel")),
        name=name,
    )(a3, w)


def _rope_tables(seq):
    half = MLA_ROPE // 2
    inv = ROPE_THETA ** (-jnp.arange(half, dtype=F32) / half)
    ang = jnp.arange(seq).astype(F32)[:, None] * inv[None, :]
    cos, sin = jnp.cos(ang), jnp.sin(ang)
    pad = MLA_HEAD_PAD - MLA_NOPE - MLA_ROPE
    cos_t = jnp.concatenate([jnp.ones((seq, MLA_NOPE), F32), cos, cos, jnp.zeros((seq, pad), F32)], axis=1)
    sin_t = jnp.concatenate([jnp.zeros((seq, MLA_NOPE), F32), sin, sin, jnp.zeros((seq, pad), F32)], axis=1)
    return cos_t, sin_t


def _rot_cols(w_rope):
    half = MLA_ROPE // 2
    return jnp.concatenate([-w_rope[..., half:], w_rope[..., :half]], axis=-1)


def _prep_mla_weights(w_uq, w_ukv):
    pad = MLA_HEAD_PAD - MLA_NOPE - MLA_ROPE
    wq = w_uq.reshape(Q_LORA, MLA_HEADS, MLA_NOPE + MLA_ROPE)
    wq_main = jnp.pad(wq, ((0, 0), (0, 0), (0, pad))).reshape(Q_LORA, MLA_W)
    wq_rot = jnp.concatenate([jnp.zeros((Q_LORA, MLA_HEADS, MLA_NOPE), F32), _rot_cols(wq[..., MLA_NOPE:]),
                              jnp.zeros((Q_LORA, MLA_HEADS, pad), F32)], axis=-1).reshape(Q_LORA, MLA_W)
    wkv = w_ukv.reshape(KV_LORA, MLA_HEADS, MLA_NOPE + MLA_V)
    wk_n = jnp.pad(wkv[..., :MLA_NOPE], ((0, 0), (0, 0), (0, MLA_HEAD_PAD - MLA_NOPE))).reshape(KV_LORA, MLA_W)
    wv = jnp.pad(wkv[..., MLA_NOPE:], ((0, 0), (0, 0), (0, MLA_HEAD_PAD - MLA_V))).reshape(KV_LORA, MLA_W)
    eye = jnp.eye(MLA_ROPE, dtype=F32)
    place = jnp.concatenate([jnp.zeros((MLA_ROPE, MLA_NOPE), F32), eye, jnp.zeros((MLA_ROPE, pad), F32)], axis=1)
    place_rot = jnp.concatenate([jnp.zeros((MLA_ROPE, MLA_NOPE), F32), _rot_cols(eye),
                                 jnp.zeros((MLA_ROPE, pad), F32)], axis=1)

    def lift(pm):
        pm = jnp.pad(pm, ((0, LANES - MLA_ROPE), (0, 0)))
        return jnp.tile(pm, (1, MLA_HEADS))

    wk_main = jnp.concatenate([wk_n, lift(place)], axis=0)
    wk_rot = jnp.concatenate([jnp.zeros((KV_LORA, MLA_W), F32), lift(place_rot)], axis=0)
    return tuple(t.astype(BF16) for t in (wq_main, wq_rot, wk_main, wk_rot, wv))


def _mla_prep_kernel(za_ref, cos_ref, sin_ref, gq_ref, gkv_ref, wq_ref, wqr_ref, wk_ref, wkr_ref, wv_ref,
                     q_ref, k_ref, vt_ref):
    za = za_ref[...]
    cq = _rms_norm(za[:, :Q_LORA].astype(F32), gq_ref[...]).astype(BF16)
    ckv = _rms_norm(za[:, Q_LORA:Q_LORA + KV_LORA].astype(F32), gkv_ref[...]).astype(BF16)
    cos = jnp.concatenate([cos_ref[...]] * MLA_HEADS, axis=1)
    sin = jnp.concatenate([sin_ref[...]] * MLA_HEADS, axis=1)
    dot = functools.partial(jnp.dot, preferred_element_type=F32)
    q = dot(cq, wq_ref[...]) * cos + dot(cq, wqr_ref[...]) * sin
    q_ref[...] = (q * ((MLA_NOPE + MLA_ROPE) ** -0.5 * LOG2E)).astype(BF16)
    lhs = jnp.concatenate([ckv, za[:, Q_LORA + KV_LORA:]], axis=1)
    k_ref[...] = (dot(lhs, wk_ref[...]) * cos + dot(lhs, wkr_ref[...]) * sin).astype(BF16)
    lane = lax.broadcasted_iota(jnp.int32, (1, MLA_W), 1)
    ones_col = jnp.where(lane % MLA_HEAD_PAD == MLA_V, 1.0, 0.0)
    vt_ref[...] = (dot(ckv, wv_ref[...]) + ones_col).T.astype(BF16)


def _mla_prep(z3, cos_t, sin_t, g_q, g_kv, weights, tm=512):
    bsz, seq, _ = z3.shape
    wq, wqr, wk, wkr, wv = weights

    def full(t):
        return pl.BlockSpec(t.shape, lambda b, i: (0,) * t.ndim)

    tab = pl.BlockSpec((tm, MLA_HEAD_PAD), lambda b, i: (i, 0))
    out = pl.BlockSpec((None, tm, MLA_W), lambda b, i: (b, i, 0))
    g_q = g_q.reshape(1, Q_LORA)
    g_kv = g_kv.reshape(1, KV_LORA)
    shp = jax.ShapeDtypeStruct((bsz, seq, MLA_W), BF16)
    return pl.pallas_call(
        _mla_prep_kernel,
        grid=(bsz, seq // tm),
        in_specs=[pl.BlockSpec((None, tm, A_WIDTH), lambda b, i: (b, i, A_COL // A_WIDTH)),
                  tab, tab, full(g_q), full(g_kv), full(wq), full(wqr), full(wk), full(wkr), full(wv)],
        out_specs=[out, out, pl.BlockSpec((None, MLA_W, tm), lambda b, i: (b, 0, i))],
        out_shape=[shp, shp, jax.ShapeDtypeStruct((bsz, MLA_W, seq), BF16)],
        compiler_params=_cparams(("parallel", "parallel")),
        name="mla_prep",
    )(z3, cos_t, sin_t, g_q, g_kv, wq, wqr, wk, wkr, wv)


def _mla_flash_kernel(q_ref, k_ref, vt_ref, o_ref, m_scr, acc_scr, s_scr, p_scr):
    ki = pl.program_id(2)

    @pl.when(ki == 0)
    def _():
        m_scr[...] = jnp.full(m_scr.shape, -jnp.inf, F32)
        acc_scr[...] = jnp.zeros(acc_scr.shape, F32)

    heads = [slice(h * MLA_HEAD_PAD, (h + 1) * MLA_HEAD_PAD) for h in range(MLA_HEADS)]
    for h, sl in enumerate(heads):
        s_scr[h] = lax.dot_general(k_ref[:, sl], q_ref[:, sl], (((1,), (1,)), ((), ())),
                                   preferred_element_type=F32)
    alphas = []
    for h in range(MLA_HEADS):
        s = s_scr[h]
        m_prev = m_scr[h]
        m_new = jnp.maximum(m_prev, jnp.max(s, axis=0, keepdims=True))
        p_scr[h] = jnp.exp2(s - m_new).astype(BF16)
        alphas.append(jnp.exp2(m_prev - m_new))
        m_scr[h] = m_new
    for h, sl in enumerate(heads):
        acc_scr[h] = alphas[h] * acc_scr[h] + jnp.dot(vt_ref[sl, :], p_scr[h], preferred_element_type=F32)

    @pl.when(ki == pl.num_programs(2) - 1)
    def _():
        for h, sl in enumerate(heads):
            acc = acc_scr[h]
            o_ref[:, sl] = (acc / acc[MLA_V:MLA_V + 1, :]).T.astype(o_ref.dtype)


def _mla_flash(q, k, vt, tq=1024, tk=1024):
    bsz, seq, _ = q.shape
    tq, tk = min(tq, seq), min(tk, seq)
    return pl.pallas_call(
        _mla_flash_kernel,
        grid=(bsz, seq // tq, seq // tk),
        in_specs=[pl.BlockSpec((None, tq, MLA_W), lambda b, i, j: (b, i, 0)),
                  pl.BlockSpec((None, tk, MLA_W), lambda b, i, j: (b, j, 0)),
                  pl.BlockSpec((None, MLA_W, tk), lambda b, i, j: (b, 0, j))],
        out_specs=pl.BlockSpec((None, tq, MLA_W), lambda b, i, j: (b, i, 0)),
        out_shape=jax.ShapeDtypeStruct((bsz, seq, MLA_W), BF16),
        scratch_shapes=[pltpu.VMEM((MLA_HEADS, 1, tq), F32),
                        pltpu.VMEM((MLA_HEADS, MLA_HEAD_PAD, tq), F32),
                        pltpu.VMEM((MLA_HEADS, tk, tq), F32),
                        pltpu.VMEM((MLA_HEADS, tk, tq), BF16)],
        compiler_params=_cparams(("parallel", "parallel", "arbitrary")),
        name="mla_flash",
    )(q, k, vt)


def _t5_bucket(rel):
    nb = T5_BUCKETS // 2
    max_exact = nb // 2
    n = jnp.abs(rel)
    large = max_exact + (jnp.log(jnp.maximum(n, 1).astype(F32) / max_exact)
                         / math.log(T5_MAX_DIST / max_exact) * (nb - max_exact)).astype(jnp.int32)
    large = jnp.minimum(large, nb - 1)
    return jnp.where(rel > 0, nb, 0) + jnp.where(n < max_exact, n, large)


def _band_bias_tables(rel_bias, head_lo, tile, half, dilation):
    width = tile + 2 * half
    col = jnp.arange(width)[None, :]
    rel = col - half - jnp.arange(tile)[:, None]
    onehot = jax.nn.one_hot(_t5_bucket(rel * dilation), T5_BUCKETS, dtype=F32)
    bias = jnp.einsum("qkn,nh->hqk", onehot, rel_bias[:, head_lo:head_lo + HEADS].astype(F32),
                      precision=lax.Precision.HIGHEST)
    band = jnp.abs(rel) <= half
    left_ok = col >= half
    right_ok = col < tile + half
    masks = jnp.stack([band, band & left_ok, band & right_ok, band & left_ok & right_ok])
    return jnp.where(masks[:, None], bias[None].astype(F32), NEG_INF)


def _head_of_lane():
    return lax.broadcasted_iota(jnp.int32, (1, HEAD_BLOCK), 1) // HEAD_DIM


def _per_head_lanes(cols):
    head = _head_of_lane()
    out = cols[HEADS - 1]
    for h in range(HEADS - 2, -1, -1):
        out = jnp.where(head == h, cols[h], out)
    return out


def _banded_kernel(*refs, tile, subtiles, half, length, use_sink, want_lse):
    refs = list(refs)
    sink_ref = refs.pop(0) if use_sink else None
    q_ref, k_ref, v_ref, bias_ref, o_ref = refs[:5]
    lse_ref = refs[5] if want_lse else None
    i = pl.program_id(2)
    nt = length // tile
    width = tile + 2 * half
    head = _head_of_lane()
    for u in range(subtiles):
        t = i * subtiles + u
        main = t * tile
        can_first, can_last = u == 0, u == subtiles - 1
        if can_first or can_last:
            left = jnp.maximum(main - half, 0) if can_first else main - half
            right = jnp.minimum(main + tile, length - half) if can_last else main + tile
            left, right = pl.multiple_of(left, half), pl.multiple_of(right, half)
            main = pl.multiple_of(main, tile)

            def window(ref):
                return jnp.concatenate([ref[pl.ds(left, half), :], ref[pl.ds(main, tile), :],
                                        ref[pl.ds(right, half), :]], axis=0)

            variant = 0
            if can_first:
                variant = variant + (t == 0).astype(jnp.int32)
            if can_last:
                variant = variant + 2 * (t == nt - 1).astype(jnp.int32)
        else:
            start = pl.multiple_of(main - half, half)

            def window(ref):
                return ref[pl.ds(start, width), :]

            variant = 0
        rows = slice(u * tile, (u + 1) * tile)
        kwin, vwin = window(k_ref), window(v_ref)
        q = q_ref[rows, :] * (HEAD_DIM ** -0.5)
        acc = None
        inv_den, lse = [], []
        for h in range(HEADS):
            mine = head == h
            s = lax.dot_general(q, jnp.where(mine, kwin, 0), (((1,), (1,)), ((), ())),
                                preferred_element_type=F32)
            s = s + bias_ref[variant, h]
            m = jnp.max(s, axis=1, keepdims=True)
            if use_sink:
                m = jnp.maximum(m, sink_ref[h])
            e = jnp.exp(s - m)
            den = jnp.sum(e, axis=1, keepdims=True)
            if use_sink:
                den = den + jnp.exp(sink_ref[h] - m)
            part = jnp.dot(e.astype(BF16), jnp.where(mine, vwin, 0), preferred_element_type=F32)
            acc = part if acc is None else acc + part
            inv_den.append(1.0 / den)
            if want_lse:
                lse.append(m + jnp.log(den))
        o_ref[rows, :] = (acc * _per_head_lanes(inv_den)).astype(o_ref.dtype)
        if want_lse:
            lse_ref[rows, :] = _per_head_lanes(lse)


def _banded_attention(zf, blk, bias, sink, *, half, subtiles, want_lse):
    bsz, dilation, length, _ = zf.shape
    tile = bias.shape[2]
    subtiles = min(subtiles, length // tile)
    step = tile * subtiles
    use_sink = sink is not None
    in_specs = [pl.BlockSpec((None, None, step, HEAD_BLOCK), lambda b, j, i: (b, j, i, blk)),
                pl.BlockSpec((None, None, length, HEAD_BLOCK), lambda b, j, i: (b, j, 0, blk + 1)),
                pl.BlockSpec((None, None, length, HEAD_BLOCK), lambda b, j, i: (b, j, 0, blk + 2)),
                pl.BlockSpec(bias.shape, lambda b, j, i: (0, 0, 0, 0))]
    args = [zf, zf, zf, bias]
    if use_sink:
        in_specs.insert(0, pl.BlockSpec(memory_space=pltpu.SMEM))
        args.insert(0, sink.astype(F32))
    o_spec = pl.BlockSpec((None, None, step, HEAD_BLOCK), lambda b, j, i: (b, j, i, 0))
    out_specs = [o_spec]
    out_shape = [jax.ShapeDtypeStruct((bsz, dilation, length, HEAD_BLOCK), BF16)]
    if want_lse:
        out_specs.append(o_spec)
        out_shape.append(jax.ShapeDtypeStruct((bsz, dilation, length, HEAD_BLOCK), F32))
    return pl.pallas_call(
        functools.partial(_banded_kernel, tile=tile, subtiles=subtiles, half=half, length=length,
                          use_sink=use_sink, want_lse=want_lse),
        grid=(bsz, dilation, length // step),
        in_specs=in_specs,
        out_specs=out_specs,
        out_shape=out_shape,
        compiler_params=_cparams(("parallel", "parallel", "arbitrary")),
        name=f"banded_r{dilation}",
    )(*args)


NA_TILE_ROWS = 4
NA_WIN_ROWS = NA_TILE_ROWS + NA_KH
NA_SUBTILES = 2


def _na_bias_tables(rpb, rows):
    tr, wr = NA_TILE_ROWS, NA_WIN_ROWS
    exact = functools.partial(jnp.einsum, precision=lax.Precision.HIGHEST)
    c = np.arange(GRID_W)
    cs = np.clip(c - NA_KW // 2, 0, GRID_W - NA_KW)
    col_ok = (c[None, :] >= cs[:, None]) & (c[None, :] < cs[:, None] + NA_KW)
    dc = np.clip(c[None, :] - c[:, None], -(NA_KW - 1), NA_KW - 1) + (NA_KW - 1)
    by_col = exact("hdn,qkn->hdqk", rpb.astype(F32), np.eye(2 * NA_KW - 1, dtype=np.float32)[dc])
    tabs = []
    for r0, ws in ((0, 0), (tr, 0), (rows - tr, rows - wr)):
        r = r0 + np.arange(tr)
        start = np.clip(r - NA_KH // 2, 0, rows - NA_KH)
        kr = ws + np.arange(wr)
        row_ok = (kr[None, :] >= start[:, None]) & (kr[None, :] < start[:, None] + NA_KH)
        dr = np.clip(kr[None, :] - r[:, None] + (NA_KH - 1), 0, 2 * NA_KH - 2)
        bias = exact("rad,hdqk->hrqak", np.eye(2 * NA_KH - 1, dtype=np.float32)[dr], by_col)
        ok = row_ok[:, None, :, None] & col_ok[None, :, None, :]
        tabs.append(jnp.where(ok[None], bias, NEG_INF).reshape(HEADS, tr * GRID_W, wr * GRID_W))
    return jnp.stack(tabs)


def _na_kernel(q_ref, k_ref, v_ref, bias_ref, o_ref, *, rows):
    i = pl.program_id(1)
    nt = rows // NA_TILE_ROWS
    win = NA_WIN_ROWS * GRID_W
    tq = NA_TILE_ROWS * GRID_W
    head = _head_of_lane()
    for u in range(NA_SUBTILES):
        t = i * NA_SUBTILES + u
        ws = jnp.clip(t * NA_TILE_ROWS - NA_KH // 2, 0, rows - NA_WIN_ROWS) * GRID_W
        ws = pl.multiple_of(ws, GRID_W)
        variant = jnp.where(t == 0, 0, jnp.where(t == nt - 1, 2, 1))
        kwin = k_ref[pl.ds(ws, win), :]
        vwin = v_ref[pl.ds(ws, win), :]
        q = q_ref[u * tq:(u + 1) * tq, :] * (HEAD_DIM ** -0.5)
        acc = None
        inv_den = []
        for h in range(HEADS):
            mine = head == h
            s = lax.dot_general(q, jnp.where(mine, kwin, 0), (((1,), (1,)), ((), ())),
                                preferred_element_type=F32)
            s = s + bias_ref[variant, h]
            m = jnp.max(s, axis=1, keepdims=True)
            e = jnp.exp(s - m)
            inv_den.append(1.0 / jnp.sum(e, axis=1, keepdims=True))
            part = jnp.dot(e.astype(BF16), jnp.where(mine, vwin, 0), preferred_element_type=F32)
            acc = part if acc is None else acc + part
        o_ref[u * tq:(u + 1) * tq, :] = (acc * _per_head_lanes(inv_den)).astype(o_ref.dtype)


def _neighborhood_attention(z3, bias):
    bsz, seq, _ = z3.shape
    rows = seq // GRID_W
    tq = NA_TILE_ROWS * GRID_W * NA_SUBTILES
    return pl.pallas_call(
        functools.partial(_na_kernel, rows=rows),
        grid=(bsz, rows // (NA_TILE_ROWS * NA_SUBTILES)),
        in_specs=[pl.BlockSpec((None, tq, HEAD_BLOCK), lambda b, i: (b, i, C_BLK)),
                  pl.BlockSpec((None, seq, HEAD_BLOCK), lambda b, i: (b, 0, C_BLK + 1)),
                  pl.BlockSpec((None, seq, HEAD_BLOCK), lambda b, i: (b, 0, C_BLK + 2)),
                  pl.BlockSpec(bias.shape, lambda b, i: (0, 0, 0, 0))],
        out_specs=pl.BlockSpec((None, tq, HEAD_BLOCK), lambda b, i: (b, i, 0)),
        out_shape=jax.ShapeDtypeStruct((bsz, seq, HEAD_BLOCK), BF16),
        compiler_params=_cparams(("parallel", "arbitrary")),
        name="neighborhood",
    )(z3, z3, z3, bias)


def _unfold(ref, scr):
    dilation, n, cols = ref.shape
    if dilation == 1:
        return ref[0].astype(F32)
    nblk = cols // LANES
    for j in range(dilation):
        for c in range(nblk):
            scr[c, pl.ds(j, n, stride=dilation), :] = ref[j, :, c * LANES:(c + 1) * LANES].astype(F32)
    return jnp.concatenate([scr[c] for c in range(nblk)], axis=1)


def _merge_kernel(gate_ref, ya_ref, yb_ref, yc_ref, o0_ref, o1_ref, o2_ref, l0_ref, l1_ref, l2_ref, h_ref,
                  wa_ref, wb_ref, wo_ref, g_ref, b_ref, h_out_ref, hb_out_ref, *scr):
    dot = functools.partial(jnp.dot, preferred_element_type=F32)
    l0, l1, l2 = _unfold(l0_ref, None), _unfold(l1_ref, scr[0]), _unfold(l2_ref, scr[1])
    o0, o1, o2 = _unfold(o0_ref, None), _unfold(o1_ref, scr[2]), _unfold(o2_ref, scr[3])
    lm = jnp.maximum(jnp.maximum(l0, l1), l2)
    e0, e1, e2 = jnp.exp(l0 - lm), jnp.exp(l1 - lm), jnp.exp(l2 - lm)
    yd = (e0 * o0 + e1 * o1 + e2 * o2) / (e0 + e1 + e2)
    branches = (dot(ya_ref[...], wa_ref[...]), dot(yb_ref[0], wb_ref[0]), dot(yc_ref[...], wb_ref[1]),
                dot(yd.astype(BF16), wb_ref[2]))
    merged = None
    for n in range(N_BRANCH):
        gate2 = 1.0 + jnp.tanh(gate_ref[:, n * D_MODEL:(n + 1) * D_MODEL].astype(F32))
        term = gate2 * branches[n]
        merged = term if merged is None else merged + term
    u = ALPHA * h_ref[...] + dot(merged.astype(BF16), wo_ref[...])
    y = _layer_norm(u, g_ref[...], b_ref[...])
    h_out_ref[...] = y
    hb_out_ref[...] = y.astype(BF16)


def _merge(z3, ya, yb, yc, od, ld, h3, wa, wb, wo, g, b, tm=512):
    bsz, seq, _ = h3.shape

    def rows(width):
        return pl.BlockSpec((None, tm, width), lambda b, i: (b, i, 0))

    def folded(t):
        dilation = t.shape[1]
        return pl.BlockSpec((None, dilation, tm // dilation, HEAD_BLOCK), lambda b, i: (b, 0, i, 0))

    def full(t):
        return pl.BlockSpec(t.shape, lambda b, i: (0,) * t.ndim)

    g = g.reshape(1, D_MODEL)
    b = b.reshape(1, D_MODEL)
    return pl.pallas_call(
        _merge_kernel,
        grid=(bsz, seq // tm),
        in_specs=[rows(GATE_COLS), rows(MLA_W), folded(yb), rows(HEAD_BLOCK)]
        + [folded(t) for t in od] + [folded(t) for t in ld]
        + [rows(D_MODEL), full(wa), full(wb), full(wo), full(g), full(b)],
        out_specs=[rows(D_MODEL), rows(D_MODEL)],
        out_shape=[jax.ShapeDtypeStruct((bsz, seq, D_MODEL), F32), jax.ShapeDtypeStruct((bsz, seq, D_MODEL), BF16)],
        scratch_shapes=[pltpu.VMEM((HEAD_BLOCK // LANES, tm, LANES), F32)] * 4,
        compiler_params=_cparams(("parallel", "parallel")),
        name="merge",
    )(z3, ya, yb, yc, *od, *ld, h3, wa, wb, wo, g, b)


FF_CHUNK = 1024


def _ffn_kernel(h_ref, hb_ref, p_ref, w1_ref, w2_ref, wp_ref, wg_ref, g_ref, b_ref, h_out_ref, hb_out_ref):
    dot = functools.partial(jnp.dot, preferred_element_type=F32)
    hb = hb_ref[...]
    ff = None
    for c in range(D_FF // FF_CHUNK):
        sl = slice(c * FF_CHUNK, (c + 1) * FF_CHUNK)
        a = jnp.maximum(dot(hb, w1_ref[:, sl]), 0.0)
        part = dot((a * a).astype(BF16), w2_ref[sl, :])
        ff = part if ff is None else ff + part
    ple = dot(p_ref[...].astype(BF16), wp_ref[...]) * (1.0 + jnp.tanh(dot(hb, wg_ref[...])))
    y = _layer_norm(ALPHA * h_ref[...] + ff + ple, g_ref[...], b_ref[...])
    h_out_ref[...] = y
    hb_out_ref[...] = y.astype(BF16)


def _ffn(h, hb, p2, w1, w2, wp, wg, g, b, tm=256):
    m = h.shape[0]

    def rows(width):
        return pl.BlockSpec((tm, width), lambda i: (i, 0))

    def full(t):
        return pl.BlockSpec(t.shape, lambda i: (0,) * t.ndim)

    g = g.reshape(1, D_MODEL)
    b = b.reshape(1, D_MODEL)
    return pl.pallas_call(
        _ffn_kernel,
        grid=(m // tm,),
        in_specs=[rows(D_MODEL), rows(D_MODEL), rows(PLE_DIM), full(w1), full(w2), full(wp), full(wg),
                  full(g), full(b)],
        out_specs=[rows(D_MODEL), rows(D_MODEL)],
        out_shape=[jax.ShapeDtypeStruct((m, D_MODEL), F32), jax.ShapeDtypeStruct((m, D_MODEL), BF16)],
        compiler_params=_cparams(("parallel",)),
        name="ffn",
    )(h, hb, p2, w1, w2, wp, wg, g, b)


SWA_TILE, SWA_SUBTILES = 256, 4
DIL_TILE, DIL_SUBTILES = 128, 8


def kernel(x, p, ln_emb_g, ln_emb_b, rel_bias, w_in, mla_q_norm, mla_w_uq, mla_kv_norm, mla_w_ukv, swa_sink,
           na_rpb, w_branch, w_out, ln1_g, ln1_b, w_ff1, w_ff2, w_ple, w_ple_gate, ln2_g, ln2_b):
    bsz, seq, _ = x.shape
    m = bsz * seq
    rows = seq // GRID_W
    cos_t, sin_t = _rope_tables(seq)
    swa_bias = _band_bias_tables(rel_bias, 0, min(SWA_TILE, seq), SWA_HALF, 1)
    dil_bias = []
    for g, (window, r) in enumerate(DIL_CONFIGS):
        half = window // (2 * r)
        dil_bias.append(_band_bias_tables(rel_bias, SWA_HEADS + g * DIL_HEADS, min(DIL_TILE, seq // r), half, r))

    h, hb = _embed_layer_norm(x.reshape(m, D_MODEL), ln_emb_g, ln_emb_b)
    for i in range(DEPTH):
        w_main, w_dil = _prep_w_in(w_in[i])
        z3 = _matmul(hb, w_main, 1024, 1536, BF16, "in_proj").reshape(bsz, seq, Z_COLS)
        hb3 = hb.reshape(bsz, seq, D_MODEL)
        q_a, k_a, v_a = _mla_prep(z3, cos_t, sin_t, mla_q_norm[i], mla_kv_norm[i],
                                  _prep_mla_weights(mla_w_uq[i], mla_w_ukv[i]))
        y_a = _mla_flash(q_a, k_a, v_a)
        (y_b,) = _banded_attention(z3.reshape(bsz, 1, seq, Z_COLS), B_BLK, swa_bias, swa_sink[i],
                                   half=SWA_HALF, subtiles=SWA_SUBTILES, want_lse=False)
        y_c = _neighborhood_attention(z3, _na_bias_tables(na_rpb[i], rows))
        o_d, l_d = [], []
        for g, (window, r) in enumerate(DIL_CONFIGS):
            zd = _matmul_fold(hb3, w_dil[g], r, 1024, f"in_proj_r{r}")
            o_g, l_g = _banded_attention(zd, 0, dil_bias[g], None, half=window // (2 * r),
                                         subtiles=DIL_SUBTILES, want_lse=True)
            o_d.append(o_g)
            l_d.append(l_g)
        wa = jnp.pad(w_branch[i, 0].reshape(MLA_HEADS, MLA_V, D_MODEL),
                     ((0, 0), (0, MLA_HEAD_PAD - MLA_V), (0, 0))).reshape(MLA_W, D_MODEL).astype(BF16)
        h, hb = _merge(z3, y_a, y_b, y_c, o_d, l_d, h.reshape(bsz, seq, D_MODEL),
                       wa, w_branch[i, 1:].astype(BF16), (0.5 * w_out[i]).astype(BF16), ln1_g[i], ln1_b[i])
        h, hb = _ffn(h.reshape(m, D_MODEL), hb.reshape(m, D_MODEL), p[i].reshape(m, PLE_DIM),
                     w_ff1[i].astype(BF16), w_ff2[i].astype(BF16),
                     (0.5 * w_ple[i]).astype(BF16), (0.5 * w_ple_gate[i]).astype(BF16), ln2_g[i], ln2_b[i])
    return h.reshape(bsz, seq, D_MODEL)
```

```python
import functools
import math

import jax
import jax.numpy as jnp
import numpy as np
from jax import lax
from jax.experimental import pallas as pl
from jax.experimental.pallas import tpu as pltpu

F32 = jnp.float32
BF16 = jnp.bfloat16

D_MODEL = 1024
DEPTH = 2
HEAD_DIM = 64
GRID_W = 64
NEG_INF = -1e30

MLA_HEADS = 4
MLA_NOPE = 64
MLA_ROPE = 32
MLA_V = 64
Q_LORA = 256
KV_LORA = 128
ROPE_THETA = 10000.0

SWA_HEADS = 4
SWA_KV_HEADS = 2
SWA_HALF = 128

NA_HEADS = 4
NA_KH = 8
NA_KW = 16

DIL_CONFIGS = ((128, 1), (512, 4), (2048, 16))
DIL_GROUPS = 3
DIL_HEADS = 4

T5_BUCKETS = 32
T5_MAX_DIST = 1024

N_BRANCH = 4
BRANCH_W = 256
D_FF = 4 * D_MODEL
PLE_DIM = 256

ALPHA = (2 * DEPTH) ** 0.25

LANES = 128
HEADS = 4
HEAD_BLOCK = HEADS * HEAD_DIM

GATE_COLS = N_BRANCH * D_MODEL
A_COL = GATE_COLS
A_WIDTH = 512
B_BLK = (A_COL + A_WIDTH) // HEAD_BLOCK
C_BLK = B_BLK + 3
D1_BLK = C_BLK + 3
Z_BLKS = D1_BLK + 3
Z_COLS = Z_BLKS * HEAD_BLOCK
QKV_COLS = 3 * HEAD_BLOCK
assert DIL_CONFIGS[0][1] == 1 and all(r > 1 for _, r in DIL_CONFIGS[1:])

MLA_HEAD_PAD = 128
MLA_W = MLA_HEADS * MLA_HEAD_PAD
LOG2E = math.log2(math.e)

VMEM_LIMIT = 56 * 1024 * 1024


def _cparams(sem):
    return pltpu.CompilerParams(dimension_semantics=sem, vmem_limit_bytes=VMEM_LIMIT)


def _layer_norm(u, g, b, eps=1e-5):
    mu = jnp.mean(u, axis=-1, keepdims=True)
    d = u - mu
    var = jnp.mean(d * d, axis=-1, keepdims=True)
    return d * lax.rsqrt(var + eps) * g + b


def _rms_norm(u, g, eps=1e-6):
    return u * lax.rsqrt(jnp.mean(u * u, axis=-1, keepdims=True) + eps) * g


def _ln_kernel(x_ref, g_ref, b_ref, h_ref, hb_ref):
    y = _layer_norm(x_ref[...], g_ref[...], b_ref[...])
    h_ref[...] = y
    hb_ref[...] = y.astype(BF16)


def _embed_layer_norm(x2, g, b, tm=512):
    m, d = x2.shape
    row = pl.BlockSpec((tm, d), lambda i: (i, 0))
    vec = pl.BlockSpec((1, d), lambda i: (0, 0))
    return pl.pallas_call(
        _ln_kernel,
        grid=(m // tm,),
        in_specs=[row, vec, vec],
        out_specs=[row, row],
        out_shape=[jax.ShapeDtypeStruct((m, d), F32), jax.ShapeDtypeStruct((m, d), BF16)],
        compiler_params=_cparams(("parallel",)),
        name="embed_ln",
    )(x2, g.reshape(1, d), b.reshape(1, d))


def _mm_kernel(a_ref, w_ref, o_ref):
    o_ref[...] = jnp.dot(a_ref[...], w_ref[...], preferred_element_type=F32).astype(o_ref.dtype)


def _matmul(a, w, tm, tn, out_dtype, name):
    m, k = a.shape
    n = w.shape[1]
    return pl.pallas_call(
        _mm_kernel,
        grid=(m // tm, n // tn),
        in_specs=[pl.BlockSpec((tm, k), lambda i, j: (i, 0)),
                  pl.BlockSpec((k, tn), lambda i, j: (0, j))],
        out_specs=pl.BlockSpec((tm, tn), lambda i, j: (i, j)),
        out_shape=jax.ShapeDtypeStruct((m, n), out_dtype),
        compiler_params=_cparams(("parallel", "arbitrary")),
        name=name,
    )(a, w)


def _prep_w_in(w):
    o = 0
    a = w[:, o:o + Q_LORA + KV_LORA + MLA_ROPE]
    o += Q_LORA + KV_LORA + MLA_ROPE
    bq = w[:, o:o + HEAD_BLOCK]
    o += HEAD_BLOCK
    kvw = SWA_KV_HEADS * HEAD_DIM
    bk = w[:, o:o + kvw]
    o += kvw
    bv = w[:, o:o + kvw]
    o += kvw
    c = w[:, o:o + 3 * HEAD_BLOCK]
    o += 3 * HEAD_BLOCK
    d = w[:, o:o + 3 * DIL_GROUPS * HEAD_BLOCK]
    o += 3 * DIL_GROUPS * HEAD_BLOCK
    gates = w[:, o:]
    rep = SWA_HEADS // SWA_KV_HEADS

    def expand(t):
        t = t.reshape(D_MODEL, SWA_KV_HEADS, 1, HEAD_DIM)
        return jnp.broadcast_to(t, (D_MODEL, SWA_KV_HEADS, rep, HEAD_DIM)).reshape(D_MODEL, HEAD_BLOCK)

    a = jnp.pad(a, ((0, 0), (0, A_WIDTH - a.shape[1])))
    d = d.reshape(D_MODEL, 3, DIL_GROUPS, HEAD_BLOCK).transpose(2, 0, 1, 3).reshape(DIL_GROUPS, D_MODEL, QKV_COLS)
    main = jnp.concatenate([0.5 * gates, a, bq, expand(bk), expand(bv), c, d[0]], axis=1)
    return main.astype(BF16), d[1:].astype(BF16)


def _mm_fold_kernel(a_ref, w_ref, o_ref, acc_ref, *, dilation):
    acc = jnp.dot(a_ref[...], w_ref[...], preferred_element_type=F32)
    nblk, tm, _ = acc_ref.shape
    for c in range(nblk):
        acc_ref[c] = acc[:, c * LANES:(c + 1) * LANES]
    n = tm // dilation
    for j in range(dilation):
        for c in range(nblk):
            o_ref[j, :, c * LANES:(c + 1) * LANES] = (
                acc_ref[c, pl.ds(j, n, stride=dilation), :].astype(o_ref.dtype))


def _matmul_fold(a3, w, dilation, tm, name):
    bsz, seq, k = a3.shape
    n = w.shape[1]
    return pl.pallas_call(
        functools.partial(_mm_fold_kernel, dilation=dilation),
        grid=(bsz, seq // tm),
        in_specs=[pl.BlockSpec((None, tm, k), lambda b, i: (b, i, 0)),
                  pl.BlockSpec((k, n), lambda b, i: (0, 0))],
        out_specs=pl.BlockSpec((None, dilation, tm // dilation, n), lambda b, i: (b, 0, i, 0)),
        out_shape=jax.ShapeDtypeStruct((bsz, dilation, seq // dilation, n), BF16),
        scratch_shapes=[pltpu.VMEM((n // LANES, tm, LANES), F32)],
        compiler_params=_cparams(("parallel", "parallel")),
        name=name,
    )(a3, w)


def _rope_tables(seq):
    half = MLA_ROPE // 2
    inv = ROPE_THETA ** (-jnp.arange(half, dtype=F32) / half)
    ang = jnp.arange(seq).astype(F32)[:, None] * inv[None, :]
    cos, sin = jnp.cos(ang), jnp.sin(ang)
    pad = MLA_HEAD_PAD - MLA_NOPE - MLA_ROPE
    cos_t = jnp.concatenate([jnp.ones((seq, MLA_NOPE), F32), cos, cos, jnp.zeros((seq, pad), F32)], axis=1)
    sin_t = jnp.concatenate([jnp.zeros((seq, MLA_NOPE), F32), sin, sin, jnp.zeros((seq, pad), F32)], axis=1)
    return cos_t, sin_t


def _rot_cols(w_rope):
    half = MLA_ROPE // 2
    return jnp.concatenate([-w_rope[..., half:], w_rope[..., :half]], axis=-1)


def _prep_mla_weights(w_uq, w_ukv):
    pad = MLA_HEAD_PAD - MLA_NOPE - MLA_ROPE
    wq = w_uq.reshape(Q_LORA, MLA_HEADS, MLA_NOPE + MLA_ROPE)
    wq_main = jnp.pad(wq, ((0, 0), (0, 0), (0, pad))).reshape(Q_LORA, MLA_W)
    wq_rot = jnp.concatenate([jnp.zeros((Q_LORA, MLA_HEADS, MLA_NOPE), F32), _rot_cols(wq[..., MLA_NOPE:]),
                              jnp.zeros((Q_LORA, MLA_HEADS, pad), F32)], axis=-1).reshape(Q_LORA, MLA_W)
    wkv = w_ukv.reshape(KV_LORA, MLA_HEADS, MLA_NOPE + MLA_V)
    wk_n = jnp.pad(wkv[..., :MLA_NOPE], ((0, 0), (0, 0), (0, MLA_HEAD_PAD - MLA_NOPE))).reshape(KV_LORA, MLA_W)
    wv = jnp.pad(wkv[..., MLA_NOPE:], ((0, 0), (0, 0), (0, MLA_HEAD_PAD - MLA_V))).reshape(KV_LORA, MLA_W)
    eye = jnp.eye(MLA_ROPE, dtype=F32)
    place = jnp.concatenate([jnp.zeros((MLA_ROPE, MLA_NOPE), F32), eye, jnp.zeros((MLA_ROPE, pad), F32)], axis=1)
    place_rot = jnp.concatenate([jnp.zeros((MLA_ROPE, MLA_NOPE), F32), _rot_cols(eye),
                                 jnp.zeros((MLA_ROPE, pad), F32)], axis=1)

    def lift(pm):
        pm = jnp.pad(pm, ((0, LANES - MLA_ROPE), (0, 0)))
        return jnp.tile(pm, (1, MLA_HEADS))

    wk_main = jnp.concatenate([wk_n, lift(place)], axis=0)
    wk_rot = jnp.concatenate([jnp.zeros((KV_LORA, MLA_W), F32), lift(place_rot)], axis=0)
    return tuple(t.astype(BF16) for t in (wq_main, wq_rot, wk_main, wk_rot, wv))


def _mla_prep_kernel(za_ref, cos_ref, sin_ref, gq_ref, gkv_ref, wq_ref, wqr_ref, wk_ref, wkr_ref, wv_ref,
                     q_ref, k_ref, vt_ref):
    za = za_ref[...]
    cq = _rms_norm(za[:, :Q_LORA].astype(F32), gq_ref[...]).astype(BF16)
    ckv = _rms_norm(za[:, Q_LORA:Q_LORA + KV_LORA].astype(F32), gkv_ref[...]).astype(BF16)
    cos = jnp.concatenate([cos_ref[...]] * MLA_HEADS, axis=1)
    sin = jnp.concatenate([sin_ref[...]] * MLA_HEADS, axis=1)
    dot = functools.partial(jnp.dot, preferred_element_type=F32)
    q = dot(cq, wq_ref[...]) * cos + dot(cq, wqr_ref[...]) * sin
    q_ref[...] = (q * ((MLA_NOPE + MLA_ROPE) ** -0.5 * LOG2E)).astype(BF16)
    lhs = jnp.concatenate([ckv, za[:, Q_LORA + KV_LORA:]], axis=1)
    k_ref[...] = (dot(lhs, wk_ref[...]) * cos + dot(lhs, wkr_ref[...]) * sin).astype(BF16)
    lane = lax.broadcasted_iota(jnp.int32, (1, MLA_W), 1)
    ones_col = jnp.where(lane % MLA_HEAD_PAD == MLA_V, 1.0, 0.0)
    vt_ref[...] = (dot(ckv, wv_ref[...]) + ones_col).T.astype(BF16)


def _mla_prep(z3, cos_t, sin_t, g_q, g_kv, weights, tm=512):
    bsz, seq, _ = z3.shape
    wq, wqr, wk, wkr, wv = weights

    def full(t):
        return pl.BlockSpec(t.shape, lambda b, i: (0,) * t.ndim)

    tab = pl.BlockSpec((tm, MLA_HEAD_PAD), lambda b, i: (i, 0))
    out = pl.BlockSpec((None, tm, MLA_W), lambda b, i: (b, i, 0))
    g_q = g_q.reshape(1, Q_LORA)
    g_kv = g_kv.reshape(1, KV_LORA)
    shp = jax.ShapeDtypeStruct((bsz, seq, MLA_W), BF16)
    return pl.pallas_call(
        _mla_prep_kernel,
        grid=(bsz, seq // tm),
        in_specs=[pl.BlockSpec((None, tm, A_WIDTH), lambda b, i: (b, i, A_COL // A_WIDTH)),
                  tab, tab, full(g_q), full(g_kv), full(wq), full(wqr), full(wk), full(wkr), full(wv)],
        out_specs=[out, out, pl.BlockSpec((None, MLA_W, tm), lambda b, i: (b, 0, i))],
        out_shape=[shp, shp, jax.ShapeDtypeStruct((bsz, MLA_W, seq), BF16)],
        compiler_params=_cparams(("parallel", "parallel")),
        name="mla_prep",
    )(z3, cos_t, sin_t, g_q, g_kv, wq, wqr, wk, wkr, wv)


def _mla_flash_kernel(q_ref, k_ref, vt_ref, o_ref, m_scr, acc_scr, s_scr, p_scr):
    ki = pl.program_id(2)

    @pl.when(ki == 0)
    def _():
        m_scr[...] = jnp.full(m_scr.shape, -jnp.inf, F32)
        acc_scr[...] = jnp.zeros(acc_scr.shape, F32)

    heads = [slice(h * MLA_HEAD_PAD, (h + 1) * MLA_HEAD_PAD) for h in range(MLA_HEADS)]
    for h, sl in enumerate(heads):
        s_scr[h] = lax.dot_general(k_ref[:, sl], q_ref[:, sl], (((1,), (1,)), ((), ())),
                                   preferred_element_type=F32)
    alphas = []
    for h in range(MLA_HEADS):
        s = s_scr[h]
        m_prev = m_scr[h]
        m_new = jnp.maximum(m_prev, jnp.max(s, axis=0, keepdims=True))
        p_scr[h] = jnp.exp2(s - m_new).astype(BF16)
        alphas.append(jnp.exp2(m_prev - m_new))
        m_scr[h] = m_new
    for h, sl in enumerate(heads):
        acc_scr[h] = alphas[h] * acc_scr[h] + jnp.dot(vt_ref[sl, :], p_scr[h], preferred_element_type=F32)

    @pl.when(ki == pl.num_programs(2) - 1)
    def _():
        for h, sl in enumerate(heads):
            acc = acc_scr[h]
            o_ref[:, sl] = (acc / acc[MLA_V:MLA_V + 1, :]).T.astype(o_ref.dtype)


def _mla_flash(q, k, vt, tq=1024, tk=1024):
    bsz, seq, _ = q.shape
    tq, tk = min(tq, seq), min(tk, seq)
    return pl.pallas_call(
        _mla_flash_kernel,
        grid=(bsz, seq // tq, seq // tk),
        in_specs=[pl.BlockSpec((None, tq, MLA_W), lambda b, i, j: (b, i, 0)),
                  pl.BlockSpec((None, tk, MLA_W), lambda b, i, j: (b, j, 0)),
                  pl.BlockSpec((None, MLA_W, tk), lambda b, i, j: (b, 0, j))],
        out_specs=pl.BlockSpec((None, tq, MLA_W), lambda b, i, j: (b, i, 0)),
        out_shape=jax.ShapeDtypeStruct((bsz, seq, MLA_W), BF16),
        scratch_shapes=[pltpu.VMEM((MLA_HEADS, 1, tq), F32),
                        pltpu.VMEM((MLA_HEADS, MLA_HEAD_PAD, tq), F32),
                        pltpu.VMEM((MLA_HEADS, tk, tq), F32),
                        pltpu.VMEM((MLA_HEADS, tk, tq), BF16)],
        compiler_params=_cparams(("parallel", "parallel", "arbitrary")),
        name="mla_flash",
    )(q, k, vt)


def _t5_bucket(rel):
    nb = T5_BUCKETS // 2
    max_exact = nb // 2
    n = jnp.abs(rel)
    large = max_exact + (jnp.log(jnp.maximum(n, 1).astype(F32) / max_exact)
                         / math.log(T5_MAX_DIST / max_exact) * (nb - max_exact)).astype(jnp.int32)
    large = jnp.minimum(large, nb - 1)
    return jnp.where(rel > 0, nb, 0) + jnp.where(n < max_exact, n, large)


def _band_bias_tables(rel_bias, head_lo, tile, half, dilation):
    width = tile + 2 * half
    col = jnp.arange(width)[None, :]
    rel = col - half - jnp.arange(tile)[:, None]
    onehot = jax.nn.one_hot(_t5_bucket(rel * dilation), T5_BUCKETS, dtype=F32)
    bias = jnp.einsum("qkn,nh->hqk", onehot, rel_bias[:, head_lo:head_lo + HEADS].astype(F32),
                      precision=lax.Precision.HIGHEST)
    band = jnp.abs(rel) <= half
    left_ok = col >= half
    right_ok = col < tile + half
    masks = jnp.stack([band, band & left_ok, band & right_ok, band & left_ok & right_ok])
    return jnp.where(masks[:, None], bias[None].astype(F32), NEG_INF)


def _head_of_lane():
    return lax.broadcasted_iota(jnp.int32, (1, HEAD_BLOCK), 1) // HEAD_DIM


def _per_head_lanes(cols):
    head = _head_of_lane()
    out = cols[HEADS - 1]
    for h in range(HEADS - 2, -1, -1):
        out = jnp.where(head == h, cols[h], out)
    return out


def _packed_heads_attention(q, kwin, vwin, bias, sink_rows):
    t = q.shape[0]
    head = _head_of_lane()
    qs = jnp.concatenate([jnp.where(head == h, q, 0) for h in range(HEADS)], axis=0)
    s = lax.dot_general(qs, kwin, (((1,), (1,)), ((), ())), preferred_element_type=F32)
    s = s + bias.reshape(HEADS * t, bias.shape[-1])
    if sink_rows is not None:
        m = jnp.max(jnp.concatenate([jnp.maximum(s[:, :LANES], sink_rows), s[:, LANES:]], axis=1),
                    axis=1, keepdims=True)
    else:
        m = jnp.max(s, axis=1, keepdims=True)
    e = jnp.exp(s - m)
    if sink_rows is not None:
        lane = lax.broadcasted_iota(jnp.int32, (1, LANES), 1)
        e_sink = jnp.where(lane == 0, jnp.exp(sink_rows - m), 0.0)
        den = jnp.sum(jnp.concatenate([e[:, :LANES] + e_sink, e[:, LANES:]], axis=1), axis=1, keepdims=True)
    else:
        den = jnp.sum(e, axis=1, keepdims=True)
    o = jnp.dot(e.astype(BF16), vwin, preferred_element_type=F32) * (1.0 / den)
    out = o[(HEADS - 1) * t:]
    for h in range(HEADS - 2, -1, -1):
        out = jnp.where(head == h, o[h * t:(h + 1) * t], out)
    return out, m, den


def _banded_kernel(*refs, tile, subtiles, half, length, use_sink, want_lse):
    refs = list(refs)
    sink_ref = refs.pop(0) if use_sink else None
    q_ref, k_ref, v_ref, bias_ref, o_ref = refs[:5]
    lse_ref = refs[5] if want_lse else None
    i = pl.program_id(2)
    nt = length // tile
    width = tile + 2 * half
    head = _head_of_lane()
    for u in range(subtiles):
        t = i * subtiles + u
        main = t * tile
        can_first, can_last = u == 0, u == subtiles - 1
        if can_first or can_last:
            left = jnp.maximum(main - half, 0) if can_first else main - half
            right = jnp.minimum(main + tile, length - half) if can_last else main + tile
            left, right = pl.multiple_of(left, half), pl.multiple_of(right, half)
            main = pl.multiple_of(main, tile)

            def window(ref):
                return jnp.concatenate([ref[pl.ds(left, half), :], ref[pl.ds(main, tile), :],
                                        ref[pl.ds(right, half), :]], axis=0)

            variant = 0
            if can_first:
                variant = variant + (t == 0).astype(jnp.int32)
            if can_last:
                variant = variant + 2 * (t == nt - 1).astype(jnp.int32)
        else:
            start = pl.multiple_of(main - half, half)

            def window(ref):
                return ref[pl.ds(start, width), :]

            variant = 0
        rows = slice(u * tile, (u + 1) * tile)
        kwin, vwin = window(k_ref), window(v_ref)
        q = q_ref[rows, :] * (HEAD_DIM ** -0.5)
        sink_rows = sink_ref[...] if use_sink else None
        out, m, den = _packed_heads_attention(q, kwin, vwin, bias_ref[variant], sink_rows)
        o_ref[rows, :] = out.astype(o_ref.dtype)
        if want_lse:
            lse = m + jnp.log(den)
            lse_ref[rows, :] = _per_head_lanes([lse[h * tile:(h + 1) * tile] for h in range(HEADS)])


def _banded_attention(zf, blk, bias, sink, *, half, subtiles, want_lse):
    bsz, dilation, length, _ = zf.shape
    tile = bias.shape[2]
    subtiles = min(subtiles, length // tile)
    step = tile * subtiles
    use_sink = sink is not None
    in_specs = [pl.BlockSpec((None, None, step, HEAD_BLOCK), lambda b, j, i: (b, j, i, blk)),
                pl.BlockSpec((None, None, length, HEAD_BLOCK), lambda b, j, i: (b, j, 0, blk + 1)),
                pl.BlockSpec((None, None, length, HEAD_BLOCK), lambda b, j, i: (b, j, 0, blk + 2)),
                pl.BlockSpec(bias.shape, lambda b, j, i: (0, 0, 0, 0))]
    args = [zf, zf, zf, bias]
    if use_sink:
        in_specs.insert(0, pl.BlockSpec((HEADS * tile, LANES), lambda b, j, i: (0, 0)))
        args.insert(0, jnp.broadcast_to(jnp.repeat(sink.astype(F32), tile)[:, None], (HEADS * tile, LANES)))
    o_spec = pl.BlockSpec((None, None, step, HEAD_BLOCK), lambda b, j, i: (b, j, i, 0))
    out_specs = [o_spec]
    out_shape = [jax.ShapeDtypeStruct((bsz, dilation, length, HEAD_BLOCK), BF16)]
    if want_lse:
        out_specs.append(o_spec)
        out_shape.append(jax.ShapeDtypeStruct((bsz, dilation, length, HEAD_BLOCK), F32))
    return pl.pallas_call(
        functools.partial(_banded_kernel, tile=tile, subtiles=subtiles, half=half, length=length,
                          use_sink=use_sink, want_lse=want_lse),
        grid=(bsz, dilation, length // step),
        in_specs=in_specs,
        out_specs=out_specs,
        out_shape=out_shape,
        compiler_params=_cparams(("parallel", "parallel", "arbitrary")),
        name=f"banded_r{dilation}",
    )(*args)


NA_TILE_ROWS = 4
NA_WIN_ROWS = NA_TILE_ROWS + NA_KH
NA_SUBTILES = 2


def _na_bias_tables(rpb, rows):
    tr, wr = NA_TILE_ROWS, NA_WIN_ROWS
    exact = functools.partial(jnp.einsum, precision=lax.Precision.HIGHEST)
    c = np.arange(GRID_W)
    cs = np.clip(c - NA_KW // 2, 0, GRID_W - NA_KW)
    col_ok = (c[None, :] >= cs[:, None]) & (c[None, :] < cs[:, None] + NA_KW)
    dc = np.clip(c[None, :] - c[:, None], -(NA_KW - 1), NA_KW - 1) + (NA_KW - 1)
    by_col = exact("hdn,qkn->hdqk", rpb.astype(F32), np.eye(2 * NA_KW - 1, dtype=np.float32)[dc])
    tabs = []
    for r0, ws in ((0, 0), (tr, 0), (rows - tr, rows - wr)):
        r = r0 + np.arange(tr)
        start = np.clip(r - NA_KH // 2, 0, rows - NA_KH)
        kr = ws + np.arange(wr)
        row_ok = (kr[None, :] >= start[:, None]) & (kr[None, :] < start[:, None] + NA_KH)
        dr = np.clip(kr[None, :] - r[:, None] + (NA_KH - 1), 0, 2 * NA_KH - 2)
        bias = exact("rad,hdqk->hrqak", np.eye(2 * NA_KH - 1, dtype=np.float32)[dr], by_col)
        ok = row_ok[:, None, :, None] & col_ok[None, :, None, :]
        tabs.append(jnp.where(ok[None], bias, NEG_INF).reshape(HEADS, tr * GRID_W, wr * GRID_W))
    return jnp.stack(tabs)


def _na_kernel(q_ref, k_ref, v_ref, bias_ref, o_ref, *, rows):
    i = pl.program_id(1)
    nt = rows // NA_TILE_ROWS
    win = NA_WIN_ROWS * GRID_W
    tq = NA_TILE_ROWS * GRID_W
    head = _head_of_lane()
    for u in range(NA_SUBTILES):
        t = i * NA_SUBTILES + u
        ws = jnp.clip(t * NA_TILE_ROWS - NA_KH // 2, 0, rows - NA_WIN_ROWS) * GRID_W
        ws = pl.multiple_of(ws, GRID_W)
        variant = jnp.where(t == 0, 0, jnp.where(t == nt - 1, 2, 1))
        kwin = k_ref[pl.ds(ws, win), :]
        vwin = v_ref[pl.ds(ws, win), :]
        q = q_ref[u * tq:(u + 1) * tq, :] * (HEAD_DIM ** -0.5)
        acc = None
        inv_den = []
        for h in range(HEADS):
            mine = head == h
            s = lax.dot_general(q, jnp.where(mine, kwin, 0), (((1,), (1,)), ((), ())),
                                preferred_element_type=F32)
            s = s + bias_ref[variant, h]
            m = jnp.max(s, axis=1, keepdims=True)
            e = jnp.exp(s - m)
            inv_den.append(1.0 / jnp.sum(e, axis=1, keepdims=True))
            part = jnp.dot(e.astype(BF16), jnp.where(mine, vwin, 0), preferred_element_type=F32)
            acc = part if acc is None else acc + part
        o_ref[u * tq:(u + 1) * tq, :] = (acc * _per_head_lanes(inv_den)).astype(o_ref.dtype)


def _neighborhood_attention(z3, bias):
    bsz, seq, _ = z3.shape
    rows = seq // GRID_W
    tq = NA_TILE_ROWS * GRID_W * NA_SUBTILES
    return pl.pallas_call(
        functools.partial(_na_kernel, rows=rows),
        grid=(bsz, rows // (NA_TILE_ROWS * NA_SUBTILES)),
        in_specs=[pl.BlockSpec((None, tq, HEAD_BLOCK), lambda b, i: (b, i, C_BLK)),
                  pl.BlockSpec((None, seq, HEAD_BLOCK), lambda b, i: (b, 0, C_BLK + 1)),
                  pl.BlockSpec((None, seq, HEAD_BLOCK), lambda b, i: (b, 0, C_BLK + 2)),
                  pl.BlockSpec(bias.shape, lambda b, i: (0, 0, 0, 0))],
        out_specs=pl.BlockSpec((None, tq, HEAD_BLOCK), lambda b, i: (b, i, 0)),
        out_shape=jax.ShapeDtypeStruct((bsz, seq, HEAD_BLOCK), BF16),
        compiler_params=_cparams(("parallel", "arbitrary")),
        name="neighborhood",
    )(z3, z3, z3, bias)


def _unfold(ref, scr):
    dilation, n, cols = ref.shape
    if dilation == 1:
        return ref[0].astype(F32)
    nblk = cols // LANES
    for j in range(dilation):
        for c in range(nblk):
            scr[c, pl.ds(j, n, stride=dilation), :] = ref[j, :, c * LANES:(c + 1) * LANES].astype(F32)
    return jnp.concatenate([scr[c] for c in range(nblk)], axis=1)


def _merge_kernel(gate_ref, ya_ref, yb_ref, yc_ref, o0_ref, o1_ref, o2_ref, l0_ref, l1_ref, l2_ref, h_ref,
                  wa_ref, wb_ref, wo_ref, g_ref, b_ref, h_out_ref, hb_out_ref, *scr):
    dot = functools.partial(jnp.dot, preferred_element_type=F32)
    l0, l1, l2 = _unfold(l0_ref, None), _unfold(l1_ref, scr[0]), _unfold(l2_ref, scr[1])
    o0, o1, o2 = _unfold(o0_ref, None), _unfold(o1_ref, scr[2]), _unfold(o2_ref, scr[3])
    lm = jnp.maximum(jnp.maximum(l0, l1), l2)
    e0, e1, e2 = jnp.exp(l0 - lm), jnp.exp(l1 - lm), jnp.exp(l2 - lm)
    yd = (e0 * o0 + e1 * o1 + e2 * o2) / (e0 + e1 + e2)
    branches = (dot(ya_ref[...], wa_ref[...]), dot(yb_ref[0], wb_ref[0]), dot(yc_ref[...], wb_ref[1]),
                dot(yd.astype(BF16), wb_ref[2]))
    merged = None
    for n in range(N_BRANCH):
        gate2 = 1.0 + jnp.tanh(gate_ref[:, n * D_MODEL:(n + 1) * D_MODEL].astype(F32))
        term = gate2 * branches[n]
        merged = term if merged is None else merged + term
    u = ALPHA * h_ref[...] + dot(merged.astype(BF16), wo_ref[...])
    y = _layer_norm(u, g_ref[...], b_ref[...])
    h_out_ref[...] = y
    hb_out_ref[...] = y.astype(BF16)


def _merge(z3, ya, yb, yc, od, ld, h3, wa, wb, wo, g, b, tm=512):
    bsz, seq, _ = h3.shape

    def rows(width):
        return pl.BlockSpec((None, tm, width), lambda b, i: (b, i, 0))

    def folded(t):
        dilation = t.shape[1]
        return pl.BlockSpec((None, dilation, tm // dilation, HEAD_BLOCK), lambda b, i: (b, 0, i, 0))

    def full(t):
        return pl.BlockSpec(t.shape, lambda b, i: (0,) * t.ndim)

    g = g.reshape(1, D_MODEL)
    b = b.reshape(1, D_MODEL)
    return pl.pallas_call(
        _merge_kernel,
        grid=(bsz, seq // tm),
        in_specs=[rows(GATE_COLS), rows(MLA_W), folded(yb), rows(HEAD_BLOCK)]
        + [folded(t) for t in od] + [folded(t) for t in ld]
        + [rows(D_MODEL), full(wa), full(wb), full(wo), full(g), full(b)],
        out_specs=[rows(D_MODEL), rows(D_MODEL)],
        out_shape=[jax.ShapeDtypeStruct((bsz, seq, D_MODEL), F32), jax.ShapeDtypeStruct((bsz, seq, D_MODEL), BF16)],
        scratch_shapes=[pltpu.VMEM((HEAD_BLOCK // LANES, tm, LANES), F32)] * 4,
        compiler_params=_cparams(("parallel", "parallel")),
        name="merge",
    )(z3, ya, yb, yc, *od, *ld, h3, wa, wb, wo, g, b)


FF_CHUNK = 1024


def _ffn_kernel(h_ref, hb_ref, p_ref, w1_ref, w2_ref, wp_ref, wg_ref, g_ref, b_ref, h_out_ref, hb_out_ref):
    dot = functools.partial(jnp.dot, preferred_element_type=F32)
    hb = hb_ref[...]
    ff = None
    for c in range(D_FF // FF_CHUNK):
        sl = slice(c * FF_CHUNK, (c + 1) * FF_CHUNK)
        a = jnp.maximum(dot(hb, w1_ref[:, sl]), 0.0)
        part = dot((a * a).astype(BF16), w2_ref[sl, :])
        ff = part if ff is None else ff + part
    ple = dot(p_ref[...].astype(BF16), wp_ref[...]) * (1.0 + jnp.tanh(dot(hb, wg_ref[...])))
    y = _layer_norm(ALPHA * h_ref[...] + ff + ple, g_ref[...], b_ref[...])
    h_out_ref[...] = y
    hb_out_ref[...] = y.astype(BF16)


def _ffn(h, hb, p2, w1, w2, wp, wg, g, b, tm=256):
    m = h.shape[0]

    def rows(width):
        return pl.BlockSpec((tm, width), lambda i: (i, 0))

    def full(t):
        return pl.BlockSpec(t.shape, lambda i: (0,) * t.ndim)

    g = g.reshape(1, D_MODEL)
    b = b.reshape(1, D_MODEL)
    return pl.pallas_call(
        _ffn_kernel,
        grid=(m // tm,),
        in_specs=[rows(D_MODEL), rows(D_MODEL), rows(PLE_DIM), full(w1), full(w2), full(wp), full(wg),
                  full(g), full(b)],
        out_specs=[rows(D_MODEL), rows(D_MODEL)],
        out_shape=[jax.ShapeDtypeStruct((m, D_MODEL), F32), jax.ShapeDtypeStruct((m, D_MODEL), BF16)],
        compiler_params=_cparams(("parallel",)),
        name="ffn",
    )(h, hb, p2, w1, w2, wp, wg, g, b)


SWA_TILE, SWA_SUBTILES = 256, 4
DIL_TILE, DIL_SUBTILES = 128, 8


def kernel(x, p, ln_emb_g, ln_emb_b, rel_bias, w_in, mla_q_norm, mla_w_uq, mla_kv_norm, mla_w_ukv, swa_sink,
           na_rpb, w_branch, w_out, ln1_g, ln1_b, w_ff1, w_ff2, w_ple, w_ple_gate, ln2_g, ln2_b):
    bsz, seq, _ = x.shape
    m = bsz * seq
    rows = seq // GRID_W
    cos_t, sin_t = _rope_tables(seq)
    swa_bias = _band_bias_tables(rel_bias, 0, min(SWA_TILE, seq), SWA_HALF, 1)
    dil_bias = []
    for g, (window, r) in enumerate(DIL_CONFIGS):
        half = window // (2 * r)
        dil_bias.append(_band_bias_tables(rel_bias, SWA_HEADS + g * DIL_HEADS, min(DIL_TILE, seq // r), half, r))

    h, hb = _embed_layer_norm(x.reshape(m, D_MODEL), ln_emb_g, ln_emb_b)
    for i in range(DEPTH):
        w_main, w_dil = _prep_w_in(w_in[i])
        z3 = _matmul(hb, w_main, 1024, 2304, BF16, "in_proj").reshape(bsz, seq, Z_COLS)
        z4 = z3.reshape(bsz, 1, seq, Z_COLS)
        hb3 = hb.reshape(bsz, seq, D_MODEL)
        q_a, k_a, v_a = _mla_prep(z3, cos_t, sin_t, mla_q_norm[i], mla_kv_norm[i],
                                  _prep_mla_weights(mla_w_uq[i], mla_w_ukv[i]))
        y_a = _mla_flash(q_a, k_a, v_a)
        (y_b,) = _banded_attention(z4, B_BLK, swa_bias, swa_sink[i],
                                   half=SWA_HALF, subtiles=SWA_SUBTILES, want_lse=False)
        y_c = _neighborhood_attention(z3, _na_bias_tables(na_rpb[i], rows))
        o_d, l_d = [], []
        for g, (window, r) in enumerate(DIL_CONFIGS):
            if r == 1:
                zd, blk = z4, D1_BLK
            else:
                zd, blk = _matmul_fold(hb3, w_dil[g - 1], r, 1024, f"in_proj_r{r}"), 0
            o_g, l_g = _banded_attention(zd, blk, dil_bias[g], None, half=window // (2 * r),
                                         subtiles=DIL_SUBTILES, want_lse=True)
            o_d.append(o_g)
            l_d.append(l_g)
        wa = jnp.pad(w_branch[i, 0].reshape(MLA_HEADS, MLA_V, D_MODEL),
                     ((0, 0), (0, MLA_HEAD_PAD - MLA_V), (0, 0))).reshape(MLA_W, D_MODEL).astype(BF16)
        h, hb = _merge(z3, y_a, y_b, y_c, o_d, l_d, h.reshape(bsz, seq, D_MODEL),
                       wa, w_branch[i, 1:].astype(BF16), (0.5 * w_out[i]).astype(BF16), ln1_g[i], ln1_b[i])
        h, hb = _ffn(h.reshape(m, D_MODEL), hb.reshape(m, D_MODEL), p[i].reshape(m, PLE_DIM),
                     w_ff1[i].astype(BF16), w_ff2[i].astype(BF16),
                     (0.5 * w_ple[i]).astype(BF16), (0.5 * w_ple_gate[i]).astype(BF16), ln2_g[i], ln2_b[i])
    return h.reshape(bsz, seq, D_MODEL)
```

```python
import functools
import math

import jax
import jax.numpy as jnp
import numpy as np
from jax import lax
from jax.experimental import pallas as pl
from jax.experimental.pallas import tpu as pltpu

F32 = jnp.float32
BF16 = jnp.bfloat16

D_MODEL = 1024
DEPTH = 2
HEAD_DIM = 64
GRID_W = 64
NEG_INF = -1e30

MLA_HEADS = 4
MLA_NOPE = 64
MLA_ROPE = 32
MLA_V = 64
Q_LORA = 256
KV_LORA = 128
ROPE_THETA = 10000.0

SWA_HEADS = 4
SWA_KV_HEADS = 2
SWA_HALF = 128

NA_HEADS = 4
NA_KH = 8
NA_KW = 16

DIL_CONFIGS = ((128, 1), (512, 4), (2048, 16))
DIL_GROUPS = 3
DIL_HEADS = 4

T5_BUCKETS = 32
T5_MAX_DIST = 1024

N_BRANCH = 4
BRANCH_W = 256
D_FF = 4 * D_MODEL
PLE_DIM = 256

ALPHA = (2 * DEPTH) ** 0.25

LANES = 128
HEADS = 4
HEAD_BLOCK = HEADS * HEAD_DIM

GATE_COLS = N_BRANCH * D_MODEL
A_COL = GATE_COLS
A_WIDTH = 512
B_BLK = (A_COL + A_WIDTH) // HEAD_BLOCK
C_BLK = B_BLK + 3
D1_BLK = C_BLK + 3
Z_BLKS = D1_BLK + 3
Z_COLS = Z_BLKS * HEAD_BLOCK
QKV_COLS = 3 * HEAD_BLOCK
assert DIL_CONFIGS[0][1] == 1 and all(r > 1 for _, r in DIL_CONFIGS[1:])

MLA_HEAD_PAD = 128
MLA_W = MLA_HEADS * MLA_HEAD_PAD
LOG2E = math.log2(math.e)
QK_SCALE = HEAD_DIM ** -0.5 * LOG2E

VMEM_LIMIT = 56 * 1024 * 1024


def _cparams(sem):
    return pltpu.CompilerParams(dimension_semantics=sem, vmem_limit_bytes=VMEM_LIMIT)


def _layer_norm(u, g, b, eps=1e-5):
    mu = jnp.mean(u, axis=-1, keepdims=True)
    d = u - mu
    var = jnp.mean(d * d, axis=-1, keepdims=True)
    return d * lax.rsqrt(var + eps) * g + b


def _rms_norm(u, g, eps=1e-6):
    return u * lax.rsqrt(jnp.mean(u * u, axis=-1, keepdims=True) + eps) * g


def _ln_kernel(x_ref, g_ref, b_ref, h_ref, hb_ref):
    y = _layer_norm(x_ref[...], g_ref[...], b_ref[...])
    h_ref[...] = y
    hb_ref[...] = y.astype(BF16)


def _embed_layer_norm(x2, g, b, tm=512):
    m, d = x2.shape
    row = pl.BlockSpec((tm, d), lambda i: (i, 0))
    vec = pl.BlockSpec((1, d), lambda i: (0, 0))
    return pl.pallas_call(
        _ln_kernel,
        grid=(m // tm,),
        in_specs=[row, vec, vec],
        out_specs=[row, row],
        out_shape=[jax.ShapeDtypeStruct((m, d), F32), jax.ShapeDtypeStruct((m, d), BF16)],
        compiler_params=_cparams(("parallel",)),
        name="embed_ln",
    )(x2, g.reshape(1, d), b.reshape(1, d))


def _mm_kernel(a_ref, w_ref, o_ref):
    o_ref[...] = jnp.dot(a_ref[...], w_ref[...], preferred_element_type=F32).astype(o_ref.dtype)


def _matmul(a, w, tm, tn, out_dtype, name):
    m, k = a.shape
    n = w.shape[1]
    return pl.pallas_call(
        _mm_kernel,
        grid=(m // tm, n // tn),
        in_specs=[pl.BlockSpec((tm, k), lambda i, j: (i, 0)),
                  pl.BlockSpec((k, tn), lambda i, j: (0, j))],
        out_specs=pl.BlockSpec((tm, tn), lambda i, j: (i, j)),
        out_shape=jax.ShapeDtypeStruct((m, n), out_dtype),
        compiler_params=_cparams(("parallel", "arbitrary")),
        name=name,
    )(a, w)


def _prep_w_in(w):
    o = 0
    a = w[:, o:o + Q_LORA + KV_LORA + MLA_ROPE]
    o += Q_LORA + KV_LORA + MLA_ROPE
    bq = w[:, o:o + HEAD_BLOCK]
    o += HEAD_BLOCK
    kvw = SWA_KV_HEADS * HEAD_DIM
    bk = w[:, o:o + kvw]
    o += kvw
    bv = w[:, o:o + kvw]
    o += kvw
    c = w[:, o:o + 3 * HEAD_BLOCK]
    o += 3 * HEAD_BLOCK
    d = w[:, o:o + 3 * DIL_GROUPS * HEAD_BLOCK]
    o += 3 * DIL_GROUPS * HEAD_BLOCK
    gates = w[:, o:]
    rep = SWA_HEADS // SWA_KV_HEADS

    def expand(t):
        t = t.reshape(D_MODEL, SWA_KV_HEADS, 1, HEAD_DIM)
        return jnp.broadcast_to(t, (D_MODEL, SWA_KV_HEADS, rep, HEAD_DIM)).reshape(D_MODEL, HEAD_BLOCK)

    a = jnp.pad(a, ((0, 0), (0, A_WIDTH - a.shape[1])))
    qkv_scale = jnp.concatenate([jnp.full((HEAD_BLOCK,), QK_SCALE, F32), jnp.ones((2 * HEAD_BLOCK,), F32)])
    c = c * qkv_scale
    d = d.reshape(D_MODEL, 3, DIL_GROUPS, HEAD_BLOCK).transpose(2, 0, 1, 3).reshape(DIL_GROUPS, D_MODEL, QKV_COLS)
    d = d * qkv_scale
    main = jnp.concatenate([0.5 * gates, a, QK_SCALE * bq, expand(bk), expand(bv), c, d[0]], axis=1)
    return main.astype(BF16), d[1:].astype(BF16)


def _mm_fold_kernel(a_ref, w_ref, o_ref, acc_ref, *, dilation):
    acc = jnp.dot(a_ref[...], w_ref[...], preferred_element_type=F32)
    nblk, tm, _ = acc_ref.shape
    for c in range(nblk):
        acc_ref[c] = acc[:, c * LANES:(c + 1) * LANES]
    n = tm // dilation
    for j in range(dilation):
        for c in range(nblk):
            o_ref[j, :, c * LANES:(c + 1) * LANES] = (
                acc_ref[c, pl.ds(j, n, stride=dilation), :].astype(o_ref.dtype))


def _matmul_fold(a3, w, dilation, tm, name):
    bsz, seq, k = a3.shape
    n = w.shape[1]
    return pl.pallas_call(
        functools.partial(_mm_fold_kernel, dilation=dilation),
        grid=(bsz, seq // tm),
        in_specs=[pl.BlockSpec((None, tm, k), lambda b, i: (b, i, 0)),
                  pl.BlockSpec((k, n), lambda b, i: (0, 0))],
        out_specs=pl.BlockSpec((None, dilation, tm // dilation, n), lambda b, i: (b, 0, i, 0)),
        out_shape=jax.ShapeDtypeStruct((bsz, dilation, seq // dilation, n), BF16),
        scratch_shapes=[pltpu.VMEM((n // LANES, tm, LANES), F32)],
        compiler_params=_cparams(("parallel", "parallel")),
        name=name,
    )(a3, w)


def _rope_tables(seq):
    half = MLA_ROPE // 2
    inv = ROPE_THETA ** (-jnp.arange(half, dtype=F32) / half)
    ang = jnp.arange(seq).astype(F32)[:, None] * inv[None, :]
    cos, sin = jnp.cos(ang), jnp.sin(ang)
    pad = MLA_HEAD_PAD - MLA_NOPE - MLA_ROPE
    cos_t = jnp.concatenate([jnp.ones((seq, MLA_NOPE), F32), cos, cos, jnp.zeros((seq, pad), F32)], axis=1)
    sin_t = jnp.concatenate([jnp.zeros((seq, MLA_NOPE), F32), sin, sin, jnp.zeros((seq, pad), F32)], axis=1)
    return cos_t, sin_t


def _rot_cols(w_rope):
    half = MLA_ROPE // 2
    return jnp.concatenate([-w_rope[..., half:], w_rope[..., :half]], axis=-1)


def _prep_mla_weights(w_uq, w_ukv):
    pad = MLA_HEAD_PAD - MLA_NOPE - MLA_ROPE
    wq = w_uq.reshape(Q_LORA, MLA_HEADS, MLA_NOPE + MLA_ROPE)
    wq_main = jnp.pad(wq, ((0, 0), (0, 0), (0, pad))).reshape(Q_LORA, MLA_W)
    wq_rot = jnp.concatenate([jnp.zeros((Q_LORA, MLA_HEADS, MLA_NOPE), F32), _rot_cols(wq[..., MLA_NOPE:]),
                              jnp.zeros((Q_LORA, MLA_HEADS, pad), F32)], axis=-1).reshape(Q_LORA, MLA_W)
    wkv = w_ukv.reshape(KV_LORA, MLA_HEADS, MLA_NOPE + MLA_V)
    wk_n = jnp.pad(wkv[..., :MLA_NOPE], ((0, 0), (0, 0), (0, MLA_HEAD_PAD - MLA_NOPE))).reshape(KV_LORA, MLA_W)
    wv = jnp.pad(wkv[..., MLA_NOPE:], ((0, 0), (0, 0), (0, MLA_HEAD_PAD - MLA_V))).reshape(KV_LORA, MLA_W)
    eye = jnp.eye(MLA_ROPE, dtype=F32)
    place = jnp.concatenate([jnp.zeros((MLA_ROPE, MLA_NOPE), F32), eye, jnp.zeros((MLA_ROPE, pad), F32)], axis=1)
    place_rot = jnp.concatenate([jnp.zeros((MLA_ROPE, MLA_NOPE), F32), _rot_cols(eye),
                                 jnp.zeros((MLA_ROPE, pad), F32)], axis=1)

    def lift(pm):
        pm = jnp.pad(pm, ((0, LANES - MLA_ROPE), (0, 0)))
        return jnp.tile(pm, (1, MLA_HEADS))

    wk_main = jnp.concatenate([wk_n, lift(place)], axis=0)
    wk_rot = jnp.concatenate([jnp.zeros((KV_LORA, MLA_W), F32), lift(place_rot)], axis=0)
    return tuple(t.astype(BF16) for t in (wq_main, wq_rot, wk_main, wk_rot, wv))


def _mla_prep_kernel(za_ref, cos_ref, sin_ref, gq_ref, gkv_ref, wq_ref, wqr_ref, wk_ref, wkr_ref, wv_ref,
                     q_ref, k_ref, vt_ref):
    za = za_ref[...]
    cq = _rms_norm(za[:, :Q_LORA].astype(F32), gq_ref[...]).astype(BF16)
    ckv = _rms_norm(za[:, Q_LORA:Q_LORA + KV_LORA].astype(F32), gkv_ref[...]).astype(BF16)
    cos = jnp.concatenate([cos_ref[...]] * MLA_HEADS, axis=1)
    sin = jnp.concatenate([sin_ref[...]] * MLA_HEADS, axis=1)
    dot = functools.partial(jnp.dot, preferred_element_type=F32)
    q = dot(cq, wq_ref[...]) * cos + dot(cq, wqr_ref[...]) * sin
    q_ref[...] = (q * ((MLA_NOPE + MLA_ROPE) ** -0.5 * LOG2E)).astype(BF16)
    lhs = jnp.concatenate([ckv, za[:, Q_LORA + KV_LORA:]], axis=1)
    k_ref[...] = (dot(lhs, wk_ref[...]) * cos + dot(lhs, wkr_ref[...]) * sin).astype(BF16)
    lane = lax.broadcasted_iota(jnp.int32, (1, MLA_W), 1)
    ones_col = jnp.where(lane % MLA_HEAD_PAD == MLA_V, 1.0, 0.0)
    vt_ref[...] = (dot(ckv, wv_ref[...]) + ones_col).T.astype(BF16)


def _mla_prep(z3, cos_t, sin_t, g_q, g_kv, weights, tm=1024):
    bsz, seq, _ = z3.shape
    wq, wqr, wk, wkr, wv = weights

    def full(t):
        return pl.BlockSpec(t.shape, lambda b, i: (0,) * t.ndim)

    tab = pl.BlockSpec((tm, MLA_HEAD_PAD), lambda b, i: (i, 0))
    out = pl.BlockSpec((None, tm, MLA_W), lambda b, i: (b, i, 0))
    g_q = g_q.reshape(1, Q_LORA)
    g_kv = g_kv.reshape(1, KV_LORA)
    shp = jax.ShapeDtypeStruct((bsz, seq, MLA_W), BF16)
    return pl.pallas_call(
        _mla_prep_kernel,
        grid=(bsz, seq // tm),
        in_specs=[pl.BlockSpec((None, tm, A_WIDTH), lambda b, i: (b, i, A_COL // A_WIDTH)),
                  tab, tab, full(g_q), full(g_kv), full(wq), full(wqr), full(wk), full(wkr), full(wv)],
        out_specs=[out, out, pl.BlockSpec((None, MLA_W, tm), lambda b, i: (b, 0, i))],
        out_shape=[shp, shp, jax.ShapeDtypeStruct((bsz, MLA_W, seq), BF16)],
        compiler_params=_cparams(("parallel", "parallel")),
        name="mla_prep",
    )(z3, cos_t, sin_t, g_q, g_kv, wq, wqr, wk, wkr, wv)


def _mla_flash_kernel(q_ref, k_ref, vt_ref, o_ref, m_scr, acc_scr, s_scr, p_scr):
    ki = pl.program_id(2)

    @pl.when(ki == 0)
    def _():
        m_scr[...] = jnp.full(m_scr.shape, -jnp.inf, F32)
        acc_scr[...] = jnp.zeros(acc_scr.shape, F32)

    heads = [slice(h * MLA_HEAD_PAD, (h + 1) * MLA_HEAD_PAD) for h in range(MLA_HEADS)]
    for h, sl in enumerate(heads):
        s_scr[h] = lax.dot_general(k_ref[:, sl], q_ref[:, sl], (((1,), (1,)), ((), ())),
                                   preferred_element_type=F32)
    alphas = []
    for h in range(MLA_HEADS):
        s = s_scr[h]
        m_prev = m_scr[h]
        m_new = jnp.maximum(m_prev, jnp.max(s, axis=0, keepdims=True))
        p_scr[h] = jnp.exp2(s - m_new).astype(BF16)
        alphas.append(jnp.exp2(m_prev - m_new))
        m_scr[h] = m_new
    for h, sl in enumerate(heads):
        acc_scr[h] = alphas[h] * acc_scr[h] + jnp.dot(vt_ref[sl, :], p_scr[h], preferred_element_type=F32)

    @pl.when(ki == pl.num_programs(2) - 1)
    def _():
        for h, sl in enumerate(heads):
            acc = acc_scr[h]
            o_ref[:, sl] = (acc / acc[MLA_V:MLA_V + 1, :]).T.astype(o_ref.dtype)


def _mla_flash(q, k, vt, tq=512, tk=2048):
    bsz, seq, _ = q.shape
    tq, tk = min(tq, seq), min(tk, seq)
    return pl.pallas_call(
        _mla_flash_kernel,
        grid=(bsz, seq // tq, seq // tk),
        in_specs=[pl.BlockSpec((None, tq, MLA_W), lambda b, i, j: (b, i, 0)),
                  pl.BlockSpec((None, tk, MLA_W), lambda b, i, j: (b, j, 0)),
                  pl.BlockSpec((None, MLA_W, tk), lambda b, i, j: (b, 0, j))],
        out_specs=pl.BlockSpec((None, tq, MLA_W), lambda b, i, j: (b, i, 0)),
        out_shape=jax.ShapeDtypeStruct((bsz, seq, MLA_W), BF16),
        scratch_shapes=[pltpu.VMEM((MLA_HEADS, 1, tq), F32),
                        pltpu.VMEM((MLA_HEADS, MLA_HEAD_PAD, tq), F32),
                        pltpu.VMEM((MLA_HEADS, tk, tq), F32),
                        pltpu.VMEM((MLA_HEADS, tk, tq), BF16)],
        compiler_params=_cparams(("parallel", "parallel", "arbitrary")),
        name="mla_flash",
    )(q, k, vt)


def _t5_bucket(rel):
    nb = T5_BUCKETS // 2
    max_exact = nb // 2
    n = jnp.abs(rel)
    large = max_exact + (jnp.log(jnp.maximum(n, 1).astype(F32) / max_exact)
                         / math.log(T5_MAX_DIST / max_exact) * (nb - max_exact)).astype(jnp.int32)
    large = jnp.minimum(large, nb - 1)
    return jnp.where(rel > 0, nb, 0) + jnp.where(n < max_exact, n, large)


def _band_bias_tables(rel_bias, head_lo, tile, half, dilation):
    width = tile + 2 * half
    col = jnp.arange(width)[None, :]
    rel = col - half - jnp.arange(tile)[:, None]
    onehot = jax.nn.one_hot(_t5_bucket(rel * dilation), T5_BUCKETS, dtype=F32)
    bias = jnp.einsum("qkn,nh->hqk", onehot, rel_bias[:, head_lo:head_lo + HEADS].astype(F32),
                      precision=lax.Precision.HIGHEST)
    band = jnp.abs(rel) <= half
    left_ok = col >= half
    right_ok = col < tile + half
    masks = jnp.stack([band, band & left_ok, band & right_ok, band & left_ok & right_ok])
    return jnp.where(masks[:, None], LOG2E * bias[None].astype(F32), NEG_INF)


def _head_of_lane():
    return lax.broadcasted_iota(jnp.int32, (1, HEAD_BLOCK), 1) // HEAD_DIM


def _per_head_lanes(cols):
    head = _head_of_lane()
    out = cols[HEADS - 1]
    for h in range(HEADS - 2, -1, -1):
        out = jnp.where(head == h, cols[h], out)
    return out


def _packed_heads_attention(q, kwin, vwin, bias, sink_rows):
    t = q.shape[0]
    head = _head_of_lane()
    qs = jnp.concatenate([jnp.where(head == h, q, 0) for h in range(HEADS)], axis=0)
    s = lax.dot_general(qs, kwin, (((1,), (1,)), ((), ())), preferred_element_type=F32)
    s = s + bias.reshape(HEADS * t, bias.shape[-1])
    if sink_rows is not None:
        m = jnp.max(jnp.concatenate([jnp.maximum(s[:, :LANES], sink_rows), s[:, LANES:]], axis=1),
                    axis=1, keepdims=True)
    else:
        m = jnp.max(s, axis=1, keepdims=True)
    e = jnp.exp2(s - m)
    if sink_rows is not None:
        lane = lax.broadcasted_iota(jnp.int32, (1, LANES), 1)
        e_sink = jnp.where(lane == 0, jnp.exp2(sink_rows - m), 0.0)
        den = jnp.sum(jnp.concatenate([e[:, :LANES] + e_sink, e[:, LANES:]], axis=1), axis=1, keepdims=True)
    else:
        den = jnp.sum(e, axis=1, keepdims=True)
    o = jnp.dot(e.astype(BF16), vwin, preferred_element_type=F32) * (1.0 / den)
    out = o[(HEADS - 1) * t:]
    for h in range(HEADS - 2, -1, -1):
        out = jnp.where(head == h, o[h * t:(h + 1) * t], out)
    return out, m, den


def _banded_kernel(*refs, tile, subtiles, half, length, use_sink, want_lse):
    refs = list(refs)
    sink_ref = refs.pop(0) if use_sink else None
    q_ref, k_ref, v_ref, bias_ref, o_ref = refs[:5]
    lse_ref = refs[5] if want_lse else None
    i = pl.program_id(2)
    nt = length // tile
    width = tile + 2 * half
    head = _head_of_lane()
    for u in range(subtiles):
        t = i * subtiles + u
        main = t * tile
        can_first, can_last = u == 0, u == subtiles - 1
        if can_first or can_last:
            left = jnp.maximum(main - half, 0) if can_first else main - half
            right = jnp.minimum(main + tile, length - half) if can_last else main + tile
            left, right = pl.multiple_of(left, half), pl.multiple_of(right, half)
            main = pl.multiple_of(main, tile)

            def window(ref):
                return jnp.concatenate([ref[pl.ds(left, half), :], ref[pl.ds(main, tile), :],
                                        ref[pl.ds(right, half), :]], axis=0)

            variant = 0
            if can_first:
                variant = variant + (t == 0).astype(jnp.int32)
            if can_last:
                variant = variant + 2 * (t == nt - 1).astype(jnp.int32)
        else:
            start = pl.multiple_of(main - half, half)

            def window(ref):
                return ref[pl.ds(start, width), :]

            variant = 0
        rows = slice(u * tile, (u + 1) * tile)
        kwin, vwin = window(k_ref), window(v_ref)
        q = q_ref[rows, :]
        sink_rows = sink_ref[...] if use_sink else None
        out, m, den = _packed_heads_attention(q, kwin, vwin, bias_ref[variant], sink_rows)
        o_ref[rows, :] = out.astype(o_ref.dtype)
        if want_lse:
            lse = m + jnp.log2(den)
            lse_ref[rows, :] = _per_head_lanes([lse[h * tile:(h + 1) * tile] for h in range(HEADS)])


def _banded_attention(zf, blk, bias, sink, *, half, subtiles, want_lse):
    bsz, dilation, length, _ = zf.shape
    tile = bias.shape[2]
    subtiles = min(subtiles, length // tile)
    step = tile * subtiles
    use_sink = sink is not None
    in_specs = [pl.BlockSpec((None, None, step, HEAD_BLOCK), lambda b, j, i: (b, j, i, blk)),
                pl.BlockSpec((None, None, length, HEAD_BLOCK), lambda b, j, i: (b, j, 0, blk + 1)),
                pl.BlockSpec((None, None, length, HEAD_BLOCK), lambda b, j, i: (b, j, 0, blk + 2)),
                pl.BlockSpec(bias.shape, lambda b, j, i: (0, 0, 0, 0))]
    args = [zf, zf, zf, bias]
    if use_sink:
        in_specs.insert(0, pl.BlockSpec((HEADS * tile, LANES), lambda b, j, i: (0, 0)))
        args.insert(0, jnp.broadcast_to(jnp.repeat(LOG2E * sink.astype(F32), tile)[:, None], (HEADS * tile, LANES)))
    o_spec = pl.BlockSpec((None, None, step, HEAD_BLOCK), lambda b, j, i: (b, j, i, 0))
    out_specs = [o_spec]
    out_shape = [jax.ShapeDtypeStruct((bsz, dilation, length, HEAD_BLOCK), BF16)]
    if want_lse:
        out_specs.append(o_spec)
        out_shape.append(jax.ShapeDtypeStruct((bsz, dilation, length, HEAD_BLOCK), F32))
    return pl.pallas_call(
        functools.partial(_banded_kernel, tile=tile, subtiles=subtiles, half=half, length=length,
                          use_sink=use_sink, want_lse=want_lse),
        grid=(bsz, dilation, length // step),
        in_specs=in_specs,
        out_specs=out_specs,
        out_shape=out_shape,
        compiler_params=_cparams(("parallel", "parallel", "arbitrary")),
        name=f"banded_r{dilation}",
    )(*args)


NA_TILE_ROWS = 4
NA_WIN_ROWS = NA_TILE_ROWS + NA_KH
NA_SUBTILES = 4


def _na_bias_tables(rpb, rows):
    tr, wr = NA_TILE_ROWS, NA_WIN_ROWS
    exact = functools.partial(jnp.einsum, precision=lax.Precision.HIGHEST)
    c = np.arange(GRID_W)
    cs = np.clip(c - NA_KW // 2, 0, GRID_W - NA_KW)
    col_ok = (c[None, :] >= cs[:, None]) & (c[None, :] < cs[:, None] + NA_KW)
    dc = np.clip(c[None, :] - c[:, None], -(NA_KW - 1), NA_KW - 1) + (NA_KW - 1)
    by_col = exact("hdn,qkn->hdqk", rpb.astype(F32), np.eye(2 * NA_KW - 1, dtype=np.float32)[dc])
    tabs = []
    for r0, ws in ((0, 0), (tr, 0), (rows - tr, rows - wr)):
        r = r0 + np.arange(tr)
        start = np.clip(r - NA_KH // 2, 0, rows - NA_KH)
        kr = ws + np.arange(wr)
        row_ok = (kr[None, :] >= start[:, None]) & (kr[None, :] < start[:, None] + NA_KH)
        dr = np.clip(kr[None, :] - r[:, None] + (NA_KH - 1), 0, 2 * NA_KH - 2)
        bias = exact("rad,hdqk->hrqak", np.eye(2 * NA_KH - 1, dtype=np.float32)[dr], by_col)
        ok = row_ok[:, None, :, None] & col_ok[None, :, None, :]
        tabs.append(jnp.where(ok[None], LOG2E * bias, NEG_INF).reshape(HEADS, tr * GRID_W, wr * GRID_W))
    return jnp.stack(tabs)


def _na_kernel(q_ref, k_ref, v_ref, bias_ref, o_ref, *, rows):
    i = pl.program_id(1)
    nt = rows // NA_TILE_ROWS
    win = NA_WIN_ROWS * GRID_W
    tq = NA_TILE_ROWS * GRID_W
    head = _head_of_lane()
    for u in range(NA_SUBTILES):
        t = i * NA_SUBTILES + u
        ws = jnp.clip(t * NA_TILE_ROWS - NA_KH // 2, 0, rows - NA_WIN_ROWS) * GRID_W
        ws = pl.multiple_of(ws, GRID_W)
        variant = jnp.where(t == 0, 0, jnp.where(t == nt - 1, 2, 1))
        kwin = k_ref[pl.ds(ws, win), :]
        vwin = v_ref[pl.ds(ws, win), :]
        q = q_ref[u * tq:(u + 1) * tq, :]
        acc = None
        inv_den = []
        for h in range(HEADS):
            mine = head == h
            s = lax.dot_general(q, jnp.where(mine, kwin, 0), (((1,), (1,)), ((), ())),
                                preferred_element_type=F32)
            s = s + bias_ref[variant, h]
            m = jnp.max(s, axis=1, keepdims=True)
            e = jnp.exp2(s - m)
            inv_den.append(1.0 / jnp.sum(e, axis=1, keepdims=True))
            part = jnp.dot(e.astype(BF16), jnp.where(mine, vwin, 0), preferred_element_type=F32)
            acc = part if acc is None else acc + part
        o_ref[u * tq:(u + 1) * tq, :] = (acc * _per_head_lanes(inv_den)).astype(o_ref.dtype)


def _neighborhood_attention(z3, bias):
    bsz, seq, _ = z3.shape
    rows = seq // GRID_W
    tq = NA_TILE_ROWS * GRID_W * NA_SUBTILES
    return pl.pallas_call(
        functools.partial(_na_kernel, rows=rows),
        grid=(bsz, rows // (NA_TILE_ROWS * NA_SUBTILES)),
        in_specs=[pl.BlockSpec((None, tq, HEAD_BLOCK), lambda b, i: (b, i, C_BLK)),
                  pl.BlockSpec((None, seq, HEAD_BLOCK), lambda b, i: (b, 0, C_BLK + 1)),
                  pl.BlockSpec((None, seq, HEAD_BLOCK), lambda b, i: (b, 0, C_BLK + 2)),
                  pl.BlockSpec(bias.shape, lambda b, i: (0, 0, 0, 0))],
        out_specs=pl.BlockSpec((None, tq, HEAD_BLOCK), lambda b, i: (b, i, 0)),
        out_shape=jax.ShapeDtypeStruct((bsz, seq, HEAD_BLOCK), BF16),
        compiler_params=_cparams(("parallel", "arbitrary")),
        name="neighborhood",
    )(z3, z3, z3, bias)


def _unfold(ref, scr):
    dilation, n, cols = ref.shape
    if dilation == 1:
        return ref[0].astype(F32)
    nblk = cols // LANES
    for j in range(dilation):
        for c in range(nblk):
            scr[c, pl.ds(j, n, stride=dilation), :] = ref[j, :, c * LANES:(c + 1) * LANES].astype(F32)
    return jnp.concatenate([scr[c] for c in range(nblk)], axis=1)


def _merge_kernel(gate_ref, ya_ref, yb_ref, yc_ref, o0_ref, o1_ref, o2_ref, l0_ref, l1_ref, l2_ref, h_ref,
                  wa_ref, wb_ref, wo_ref, g_ref, b_ref, h_out_ref, hb_out_ref, *scr):
    dot = functools.partial(jnp.dot, preferred_element_type=F32)
    l0, l1, l2 = _unfold(l0_ref, None), _unfold(l1_ref, scr[0]), _unfold(l2_ref, scr[1])
    o0, o1, o2 = _unfold(o0_ref, None), _unfold(o1_ref, scr[2]), _unfold(o2_ref, scr[3])
    lm = jnp.maximum(jnp.maximum(l0, l1), l2)
    e0, e1, e2 = jnp.exp2(l0 - lm), jnp.exp2(l1 - lm), jnp.exp2(l2 - lm)
    yd = (e0 * o0 + e1 * o1 + e2 * o2) / (e0 + e1 + e2)
    branches = (dot(ya_ref[...], wa_ref[...]), dot(yb_ref[0], wb_ref[0]), dot(yc_ref[...], wb_ref[1]),
                dot(yd.astype(BF16), wb_ref[2]))
    merged = None
    for n in range(N_BRANCH):
        gate2 = 1.0 + jnp.tanh(gate_ref[:, n * D_MODEL:(n + 1) * D_MODEL].astype(F32))
        term = gate2 * branches[n]
        merged = term if merged is None else merged + term
    u = ALPHA * h_ref[...] + dot(merged.astype(BF16), wo_ref[...])
    y = _layer_norm(u, g_ref[...], b_ref[...])
    h_out_ref[...] = y
    hb_out_ref[...] = y.astype(BF16)


def _merge(z3, ya, yb, yc, od, ld, h3, wa, wb, wo, g, b, tm=512):
    bsz, seq, _ = h3.shape

    def rows(width):
        return pl.BlockSpec((None, tm, width), lambda b, i: (b, i, 0))

    def folded(t):
        dilation = t.shape[1]
        return pl.BlockSpec((None, dilation, tm // dilation, HEAD_BLOCK), lambda b, i: (b, 0, i, 0))

    def full(t):
        return pl.BlockSpec(t.shape, lambda b, i: (0,) * t.ndim)

    g = g.reshape(1, D_MODEL)
    b = b.reshape(1, D_MODEL)
    return pl.pallas_call(
        _merge_kernel,
        grid=(bsz, seq // tm),
        in_specs=[rows(GATE_COLS), rows(MLA_W), folded(yb), rows(HEAD_BLOCK)]
        + [folded(t) for t in od] + [folded(t) for t in ld]
        + [rows(D_MODEL), full(wa), full(wb), full(wo), full(g), full(b)],
        out_specs=[rows(D_MODEL), rows(D_MODEL)],
        out_shape=[jax.ShapeDtypeStruct((bsz, seq, D_MODEL), F32), jax.ShapeDtypeStruct((bsz, seq, D_MODEL), BF16)],
        scratch_shapes=[pltpu.VMEM((HEAD_BLOCK // LANES, tm, LANES), F32)] * 4,
        compiler_params=_cparams(("parallel", "parallel")),
        name="merge",
    )(z3, ya, yb, yc, *od, *ld, h3, wa, wb, wo, g, b)


FF_CHUNK = 1024


def _ffn_kernel(h_ref, hb_ref, p_ref, w1_ref, w2_ref, wp_ref, wg_ref, g_ref, b_ref, h_out_ref, hb_out_ref):
    dot = functools.partial(jnp.dot, preferred_element_type=F32)
    hb = hb_ref[...]
    ff = None
    for c in range(D_FF // FF_CHUNK):
        sl = slice(c * FF_CHUNK, (c + 1) * FF_CHUNK)
        a = jnp.maximum(dot(hb, w1_ref[:, sl]), 0.0)
        part = dot((a * a).astype(BF16), w2_ref[sl, :])
        ff = part if ff is None else ff + part
    ple = dot(p_ref[...].astype(BF16), wp_ref[...]) * (1.0 + jnp.tanh(dot(hb, wg_ref[...])))
    y = _layer_norm(ALPHA * h_ref[...] + ff + ple, g_ref[...], b_ref[...])
    h_out_ref[...] = y
    hb_out_ref[...] = y.astype(BF16)


def _ffn(h, hb, p2, w1, w2, wp, wg, g, b, tm=256):
    m = h.shape[0]

    def rows(width):
        return pl.BlockSpec((tm, width), lambda i: (i, 0))

    def full(t):
        return pl.BlockSpec(t.shape, lambda i: (0,) * t.ndim)

    g = g.reshape(1, D_MODEL)
    b = b.reshape(1, D_MODEL)
    return pl.pallas_call(
        _ffn_kernel,
        grid=(m // tm,),
        in_specs=[rows(D_MODEL), rows(D_MODEL), rows(PLE_DIM), full(w1), full(w2), full(wp), full(wg),
                  full(g), full(b)],
        out_specs=[rows(D_MODEL), rows(D_MODEL)],
        out_shape=[jax.ShapeDtypeStruct((m, D_MODEL), F32), jax.ShapeDtypeStruct((m, D_MODEL), BF16)],
        compiler_params=_cparams(("parallel",)),
        name="ffn",
    )(h, hb, p2, w1, w2, wp, wg, g, b)


SWA_TILE, SWA_SUBTILES = 256, 4
DIL_TILE, DIL_SUBTILES = 128, 8


def kernel(x, p, ln_emb_g, ln_emb_b, rel_bias, w_in, mla_q_norm, mla_w_uq, mla_kv_norm, mla_w_ukv, swa_sink,
           na_rpb, w_branch, w_out, ln1_g, ln1_b, w_ff1, w_ff2, w_ple, w_ple_gate, ln2_g, ln2_b):
    bsz, seq, _ = x.shape
    m = bsz * seq
    rows = seq // GRID_W
    cos_t, sin_t = _rope_tables(seq)
    swa_bias = _band_bias_tables(rel_bias, 0, min(SWA_TILE, seq), SWA_HALF, 1)
    dil_bias = []
    for g, (window, r) in enumerate(DIL_CONFIGS):
        half = window // (2 * r)
        dil_bias.append(_band_bias_tables(rel_bias, SWA_HEADS + g * DIL_HEADS, min(DIL_TILE, seq // r), half, r))

    h, hb = _embed_layer_norm(x.reshape(m, D_MODEL), ln_emb_g, ln_emb_b)
    for i in range(DEPTH):
        w_main, w_dil = _prep_w_in(w_in[i])
        z3 = _matmul(hb, w_main, 1024, 2304, BF16, "in_proj").reshape(bsz, seq, Z_COLS)
        z4 = z3.reshape(bsz, 1, seq, Z_COLS)
        hb3 = hb.reshape(bsz, seq, D_MODEL)
        q_a, k_a, v_a = _mla_prep(z3, cos_t, sin_t, mla_q_norm[i], mla_kv_norm[i],
                                  _prep_mla_weights(mla_w_uq[i], mla_w_ukv[i]))
        y_a = _mla_flash(q_a, k_a, v_a)
        (y_b,) = _banded_attention(z4, B_BLK, swa_bias, swa_sink[i],
                                   half=SWA_HALF, subtiles=SWA_SUBTILES, want_lse=False)
        y_c = _neighborhood_attention(z3, _na_bias_tables(na_rpb[i], rows))
        o_d, l_d = [], []
        for g, (window, r) in enumerate(DIL_CONFIGS):
            if r == 1:
                zd, blk = z4, D1_BLK
            else:
                zd, blk = _matmul_fold(hb3, w_dil[g - 1], r, 1024, f"in_proj_r{r}"), 0
            o_g, l_g = _banded_attention(zd, blk, dil_bias[g], None, half=window // (2 * r),
                                         subtiles=DIL_SUBTILES, want_lse=True)
            o_d.append(o_g)
            l_d.append(l_g)
        wa = jnp.pad(w_branch[i, 0].reshape(MLA_HEADS, MLA_V, D_MODEL),
                     ((0, 0), (0, MLA_HEAD_PAD - MLA_V), (0, 0))).reshape(MLA_W, D_MODEL).astype(BF16)
        h, hb = _merge(z3, y_a, y_b, y_c, o_d, l_d, h.reshape(bsz, seq, D_MODEL),
                       wa, w_branch[i, 1:].astype(BF16), (0.5 * w_out[i]).astype(BF16), ln1_g[i], ln1_b[i])
        h, hb = _ffn(h.reshape(m, D_MODEL), hb.reshape(m, D_MODEL), p[i].reshape(m, PLE_DIM),
                     w_ff1[i].astype(BF16), w_ff2[i].astype(BF16),
                     (0.5 * w_ple[i]).astype(BF16), (0.5 * w_ple_gate[i]).astype(BF16), ln2_g[i], ln2_b[i])
    return h.reshape(bsz, seq, D_MODEL)
```

```python
import functools
import math

import jax
import jax.numpy as jnp
import numpy as np
from jax import lax
from jax.experimental import pallas as pl
from jax.experimental.pallas import tpu as pltpu

F32 = jnp.float32
BF16 = jnp.bfloat16

D_MODEL = 1024
DEPTH = 2
HEAD_DIM = 64
GRID_W = 64
NEG_INF = -1e30

MLA_HEADS = 4
MLA_NOPE = 64
MLA_ROPE = 32
MLA_V = 64
Q_LORA = 256
KV_LORA = 128
ROPE_THETA = 10000.0

SWA_HEADS = 4
SWA_KV_HEADS = 2
SWA_HALF = 128

NA_HEADS = 4
NA_KH = 8
NA_KW = 16

DIL_CONFIGS = ((128, 1), (512, 4), (2048, 16))
DIL_GROUPS = 3
DIL_HEADS = 4

T5_BUCKETS = 32
T5_MAX_DIST = 1024

N_BRANCH = 4
BRANCH_W = 256
D_FF = 4 * D_MODEL
PLE_DIM = 256

ALPHA = (2 * DEPTH) ** 0.25

LANES = 128
HEADS = 4
HEAD_BLOCK = HEADS * HEAD_DIM

GATE_COLS = N_BRANCH * D_MODEL
A_COL = GATE_COLS
A_WIDTH = 512
B_BLK = (A_COL + A_WIDTH) // HEAD_BLOCK
C_BLK = B_BLK + 3
D1_BLK = C_BLK + 3
Z_BLKS = D1_BLK + 3
Z_COLS = Z_BLKS * HEAD_BLOCK
QKV_COLS = 3 * HEAD_BLOCK
assert DIL_CONFIGS[0][1] == 1 and all(r > 1 for _, r in DIL_CONFIGS[1:])

MLA_HEAD_PAD = 128
MLA_W = MLA_HEADS * MLA_HEAD_PAD
LOG2E = math.log2(math.e)
QK_SCALE = HEAD_DIM ** -0.5 * LOG2E

VMEM_LIMIT = 56 * 1024 * 1024


def _cparams(sem):
    return pltpu.CompilerParams(dimension_semantics=sem, vmem_limit_bytes=VMEM_LIMIT)


def _layer_norm(u, g, b, eps=1e-5):
    mu = jnp.mean(u, axis=-1, keepdims=True)
    d = u - mu
    var = jnp.mean(d * d, axis=-1, keepdims=True)
    return d * lax.rsqrt(var + eps) * g + b


def _rms_norm(u, g, eps=1e-6):
    return u * lax.rsqrt(jnp.mean(u * u, axis=-1, keepdims=True) + eps) * g


def _ln_kernel(x_ref, g_ref, b_ref, h_ref, hb_ref):
    y = _layer_norm(x_ref[...], g_ref[...], b_ref[...])
    h_ref[...] = y
    hb_ref[...] = y.astype(BF16)


def _embed_layer_norm(x2, g, b, tm=512):
    m, d = x2.shape
    row = pl.BlockSpec((tm, d), lambda i: (i, 0))
    vec = pl.BlockSpec((1, d), lambda i: (0, 0))
    return pl.pallas_call(
        _ln_kernel,
        grid=(m // tm,),
        in_specs=[row, vec, vec],
        out_specs=[row, row],
        out_shape=[jax.ShapeDtypeStruct((m, d), F32), jax.ShapeDtypeStruct((m, d), BF16)],
        compiler_params=_cparams(("parallel",)),
        name="embed_ln",
    )(x2, g.reshape(1, d), b.reshape(1, d))


def _mm_kernel(a_ref, w_ref, o_ref):
    o_ref[...] = jnp.dot(a_ref[...], w_ref[...], preferred_element_type=F32).astype(o_ref.dtype)


def _layer_block(arr, layer):
    zeros = (0,) * (arr.ndim - 1)
    return pl.BlockSpec((None,) + arr.shape[1:], lambda *_: (layer,) + zeros)


def _matmul(a, w, layer, tm, tn, out_dtype, name):
    m, k = a.shape
    n = w.shape[2]
    return pl.pallas_call(
        _mm_kernel,
        grid=(m // tm, n // tn),
        in_specs=[pl.BlockSpec((tm, k), lambda i, j: (i, 0)),
                  pl.BlockSpec((None, k, tn), lambda i, j: (layer, 0, j))],
        out_specs=pl.BlockSpec((tm, tn), lambda i, j: (i, j)),
        out_shape=jax.ShapeDtypeStruct((m, n), out_dtype),
        compiler_params=_cparams(("parallel", "arbitrary")),
        name=name,
    )(a, w)


def _prep_w_in(w):
    nl = w.shape[0]
    o = 0
    a = w[..., o:o + Q_LORA + KV_LORA + MLA_ROPE]
    o += Q_LORA + KV_LORA + MLA_ROPE
    bq = w[..., o:o + HEAD_BLOCK]
    o += HEAD_BLOCK
    kvw = SWA_KV_HEADS * HEAD_DIM
    bk = w[..., o:o + kvw]
    o += kvw
    bv = w[..., o:o + kvw]
    o += kvw
    c = w[..., o:o + 3 * HEAD_BLOCK]
    o += 3 * HEAD_BLOCK
    d = w[..., o:o + 3 * DIL_GROUPS * HEAD_BLOCK]
    o += 3 * DIL_GROUPS * HEAD_BLOCK
    gates = w[..., o:]
    rep = SWA_HEADS // SWA_KV_HEADS

    def expand(t):
        t = t.reshape(nl, D_MODEL, SWA_KV_HEADS, 1, HEAD_DIM)
        return jnp.broadcast_to(t, (nl, D_MODEL, SWA_KV_HEADS, rep, HEAD_DIM)).reshape(nl, D_MODEL, HEAD_BLOCK)

    a = jnp.pad(a, ((0, 0), (0, 0), (0, A_WIDTH - a.shape[-1])))
    qkv_scale = jnp.concatenate([jnp.full((HEAD_BLOCK,), QK_SCALE, F32), jnp.ones((2 * HEAD_BLOCK,), F32)])
    c = c * qkv_scale
    d = d.reshape(nl, D_MODEL, 3, DIL_GROUPS, HEAD_BLOCK).transpose(0, 3, 1, 2, 4)
    d = d.reshape(nl, DIL_GROUPS, D_MODEL, QKV_COLS) * qkv_scale
    main = jnp.concatenate([0.5 * gates, a, QK_SCALE * bq, expand(bk), expand(bv), c, d[:, 0]], axis=-1)
    return main.astype(BF16), d[:, 1:].astype(BF16)


FOLD_STRIDE = 4


def _mm_fold_kernel(a_ref, w_ref, o_ref, acc_ref, tmp_ref, *, dilation):
    acc = jnp.dot(a_ref[...], w_ref[...], preferred_element_type=F32)
    nblk, tm, _ = acc_ref.shape
    for c in range(nblk):
        acc_ref[c] = acc[:, c * LANES:(c + 1) * LANES]
    if dilation <= FOLD_STRIDE:
        n = tm // dilation
        for j in range(dilation):
            for c in range(nblk):
                o_ref[j, :, c * LANES:(c + 1) * LANES] = (
                    acc_ref[c, pl.ds(j, n, stride=dilation), :].astype(o_ref.dtype))
    else:
        outer = dilation // FOLD_STRIDE
        n1, n = tm // FOLD_STRIDE, tm // dilation
        for c in range(nblk):
            for j1 in range(FOLD_STRIDE):
                tmp_ref[c, j1 * n1:(j1 + 1) * n1, :] = acc_ref[c, pl.ds(j1, n1, stride=FOLD_STRIDE), :]
        for j2 in range(outer):
            for j1 in range(FOLD_STRIDE):
                for c in range(nblk):
                    o_ref[FOLD_STRIDE * j2 + j1, :, c * LANES:(c + 1) * LANES] = (
                        tmp_ref[c, pl.ds(j1 * n1 + j2, n, stride=outer), :].astype(o_ref.dtype))


def _matmul_fold(a3, w, layer, group, dilation, tm, name):
    bsz, seq, k = a3.shape
    n = w.shape[-1]
    assert dilation <= FOLD_STRIDE or (dilation % FOLD_STRIDE == 0 and dilation // FOLD_STRIDE <= FOLD_STRIDE)
    slab = pltpu.VMEM((n // LANES, tm, LANES), F32)
    return pl.pallas_call(
        functools.partial(_mm_fold_kernel, dilation=dilation),
        grid=(bsz, seq // tm),
        in_specs=[pl.BlockSpec((None, tm, k), lambda b, i: (b, i, 0)),
                  pl.BlockSpec((None, None, k, n), lambda b, i: (layer, group, 0, 0))],
        out_specs=pl.BlockSpec((None, dilation, tm // dilation, n), lambda b, i: (b, 0, i, 0)),
        out_shape=jax.ShapeDtypeStruct((bsz, dilation, seq // dilation, n), BF16),
        scratch_shapes=[slab, slab],
        compiler_params=_cparams(("parallel", "parallel")),
        name=name,
    )(a3, w)


def _rope_tables(seq):
    half = MLA_ROPE // 2
    inv = ROPE_THETA ** (-jnp.arange(half, dtype=F32) / half)
    ang = jnp.arange(seq).astype(F32)[:, None] * inv[None, :]
    cos, sin = jnp.cos(ang), jnp.sin(ang)
    pad = MLA_HEAD_PAD - MLA_NOPE - MLA_ROPE
    cos_t = jnp.concatenate([jnp.ones((seq, MLA_NOPE), F32), cos, cos, jnp.zeros((seq, pad), F32)], axis=1)
    sin_t = jnp.concatenate([jnp.zeros((seq, MLA_NOPE), F32), sin, sin, jnp.zeros((seq, pad), F32)], axis=1)
    return cos_t, sin_t


def _rot_cols(w_rope):
    half = MLA_ROPE // 2
    return jnp.concatenate([-w_rope[..., half:], w_rope[..., :half]], axis=-1)


def _prep_mla_weights(w_uq, w_ukv):
    nl = w_uq.shape[0]
    pad = MLA_HEAD_PAD - MLA_NOPE - MLA_ROPE
    lead = ((0, 0), (0, 0), (0, 0))
    wq = w_uq.reshape(nl, Q_LORA, MLA_HEADS, MLA_NOPE + MLA_ROPE)
    wq_main = jnp.pad(wq, lead + ((0, pad),)).reshape(nl, Q_LORA, MLA_W)
    wq_rot = jnp.concatenate([jnp.zeros((nl, Q_LORA, MLA_HEADS, MLA_NOPE), F32), _rot_cols(wq[..., MLA_NOPE:]),
                              jnp.zeros((nl, Q_LORA, MLA_HEADS, pad), F32)], axis=-1).reshape(nl, Q_LORA, MLA_W)
    wkv = w_ukv.reshape(nl, KV_LORA, MLA_HEADS, MLA_NOPE + MLA_V)
    wk_n = jnp.pad(wkv[..., :MLA_NOPE], lead + ((0, MLA_HEAD_PAD - MLA_NOPE),)).reshape(nl, KV_LORA, MLA_W)
    wv = jnp.pad(wkv[..., MLA_NOPE:], lead + ((0, MLA_HEAD_PAD - MLA_V),)).reshape(nl, KV_LORA, MLA_W)
    eye = jnp.eye(MLA_ROPE, dtype=F32)
    place = jnp.concatenate([jnp.zeros((MLA_ROPE, MLA_NOPE), F32), eye, jnp.zeros((MLA_ROPE, pad), F32)], axis=1)
    place_rot = jnp.concatenate([jnp.zeros((MLA_ROPE, MLA_NOPE), F32), _rot_cols(eye),
                                 jnp.zeros((MLA_ROPE, pad), F32)], axis=1)

    def lift(pm):
        pm = jnp.pad(pm, ((0, LANES - MLA_ROPE), (0, 0)))
        return jnp.tile(pm, (1, MLA_HEADS))

    def per_layer(t):
        return jnp.broadcast_to(t, (nl,) + t.shape)

    wk_main = jnp.concatenate([wk_n, per_layer(lift(place))], axis=1)
    wk_rot = per_layer(jnp.concatenate([jnp.zeros((KV_LORA, MLA_W), F32), lift(place_rot)], axis=0))
    return tuple(t.astype(BF16) for t in (wq_main, wq_rot, wk_main, wk_rot, wv))


def _mla_prep_kernel(za_ref, cos_ref, sin_ref, gq_ref, gkv_ref, wq_ref, wqr_ref, wk_ref, wkr_ref, wv_ref,
                     q_ref, k_ref, vt_ref):
    za = za_ref[...]
    cq = _rms_norm(za[:, :Q_LORA].astype(F32), gq_ref[...]).astype(BF16)
    ckv = _rms_norm(za[:, Q_LORA:Q_LORA + KV_LORA].astype(F32), gkv_ref[...]).astype(BF16)
    cos = jnp.concatenate([cos_ref[...]] * MLA_HEADS, axis=1)
    sin = jnp.concatenate([sin_ref[...]] * MLA_HEADS, axis=1)
    dot = functools.partial(jnp.dot, preferred_element_type=F32)
    q = dot(cq, wq_ref[...]) * cos + dot(cq, wqr_ref[...]) * sin
    q_ref[...] = (q * ((MLA_NOPE + MLA_ROPE) ** -0.5 * LOG2E)).astype(BF16)
    lhs = jnp.concatenate([ckv, za[:, Q_LORA + KV_LORA:]], axis=1)
    k_ref[...] = (dot(lhs, wk_ref[...]) * cos + dot(lhs, wkr_ref[...]) * sin).astype(BF16)
    lane = lax.broadcasted_iota(jnp.int32, (1, MLA_W), 1)
    ones_col = jnp.where(lane % MLA_HEAD_PAD == MLA_V, 1.0, 0.0)
    vt_ref[...] = (dot(ckv, wv_ref[...]) + ones_col).T.astype(BF16)


def _mla_prep(z3, cos_t, sin_t, g_q, g_kv, weights, layer, tm=1024):
    bsz, seq, _ = z3.shape
    tm = min(tm, seq)
    params = (g_q, g_kv) + tuple(weights)
    tab = pl.BlockSpec((tm, MLA_HEAD_PAD), lambda b, i: (i, 0))
    out = pl.BlockSpec((None, tm, MLA_W), lambda b, i: (b, i, 0))
    shp = jax.ShapeDtypeStruct((bsz, seq, MLA_W), BF16)
    return pl.pallas_call(
        _mla_prep_kernel,
        grid=(bsz, seq // tm),
        in_specs=[pl.BlockSpec((None, tm, A_WIDTH), lambda b, i: (b, i, A_COL // A_WIDTH)), tab, tab]
        + [_layer_block(t, layer) for t in params],
        out_specs=[out, out, pl.BlockSpec((None, MLA_W, tm), lambda b, i: (b, 0, i))],
        out_shape=[shp, shp, jax.ShapeDtypeStruct((bsz, MLA_W, seq), BF16)],
        compiler_params=_cparams(("parallel", "parallel")),
        name="mla_prep",
    )(z3, cos_t, sin_t, *params)


def _mla_flash_kernel(q_ref, k_ref, vt_ref, o_ref, m_scr, acc_scr, s_scr, p_scr):
    ki = pl.program_id(2)

    @pl.when(ki == 0)
    def _():
        m_scr[...] = jnp.full(m_scr.shape, -jnp.inf, F32)
        acc_scr[...] = jnp.zeros(acc_scr.shape, F32)

    heads = [slice(h * MLA_HEAD_PAD, (h + 1) * MLA_HEAD_PAD) for h in range(MLA_HEADS)]
    for h, sl in enumerate(heads):
        s_scr[h] = lax.dot_general(k_ref[:, sl], q_ref[:, sl], (((1,), (1,)), ((), ())),
                                   preferred_element_type=F32)
    alphas = []
    for h in range(MLA_HEADS):
        s = s_scr[h]
        m_prev = m_scr[h]
        m_new = jnp.maximum(m_prev, jnp.max(s, axis=0, keepdims=True))
        p_scr[h] = jnp.exp2(s - m_new).astype(BF16)
        alphas.append(jnp.exp2(m_prev - m_new))
        m_scr[h] = m_new
    for h, sl in enumerate(heads):
        acc_scr[h] = alphas[h] * acc_scr[h] + jnp.dot(vt_ref[sl, :], p_scr[h], preferred_element_type=F32)

    @pl.when(ki == pl.num_programs(2) - 1)
    def _():
        for h, sl in enumerate(heads):
            acc = acc_scr[h]
            o_ref[:, sl] = (acc / acc[MLA_V:MLA_V + 1, :]).T.astype(o_ref.dtype)


def _mla_flash(q, k, vt, tq=512, tk=2048):
    bsz, seq, _ = q.shape
    tq, tk = min(tq, seq), min(tk, seq)
    return pl.pallas_call(
        _mla_flash_kernel,
        grid=(bsz, seq // tq, seq // tk),
        in_specs=[pl.BlockSpec((None, tq, MLA_W), lambda b, i, j: (b, i, 0)),
                  pl.BlockSpec((None, tk, MLA_W), lambda b, i, j: (b, j, 0)),
                  pl.BlockSpec((None, MLA_W, tk), lambda b, i, j: (b, 0, j))],
        out_specs=pl.BlockSpec((None, tq, MLA_W), lambda b, i, j: (b, i, 0)),
        out_shape=jax.ShapeDtypeStruct((bsz, seq, MLA_W), BF16),
        scratch_shapes=[pltpu.VMEM((MLA_HEADS, 1, tq), F32),
                        pltpu.VMEM((MLA_HEADS, MLA_HEAD_PAD, tq), F32),
                        pltpu.VMEM((MLA_HEADS, tk, tq), F32),
                        pltpu.VMEM((MLA_HEADS, tk, tq), BF16)],
        compiler_params=_cparams(("parallel", "parallel", "arbitrary")),
        name="mla_flash",
    )(q, k, vt)


def _t5_bucket(rel):
    nb = T5_BUCKETS // 2
    max_exact = nb // 2
    n = jnp.abs(rel)
    large = max_exact + (jnp.log(jnp.maximum(n, 1).astype(F32) / max_exact)
                         / math.log(T5_MAX_DIST / max_exact) * (nb - max_exact)).astype(jnp.int32)
    large = jnp.minimum(large, nb - 1)
    return jnp.where(rel > 0, nb, 0) + jnp.where(n < max_exact, n, large)


def _band_bias_tables(rel_bias, head_lo, tile, half, dilation):
    width = tile + 2 * half
    col = jnp.arange(width)[None, :]
    rel = col - half - jnp.arange(tile)[:, None]
    onehot = jax.nn.one_hot(_t5_bucket(rel * dilation), T5_BUCKETS, dtype=F32)
    bias = jnp.einsum("qkn,nh->hqk", onehot, rel_bias[:, head_lo:head_lo + HEADS].astype(F32),
                      precision=lax.Precision.HIGHEST)
    band = jnp.abs(rel) <= half
    left_ok = col >= half
    right_ok = col < tile + half
    masks = jnp.stack([band, band & left_ok, band & right_ok, band & left_ok & right_ok])
    return jnp.where(masks[:, None], LOG2E * bias[None].astype(F32), NEG_INF)


def _head_of_lane():
    return lax.broadcasted_iota(jnp.int32, (1, HEAD_BLOCK), 1) // HEAD_DIM


def _per_head_lanes(cols):
    head = _head_of_lane()
    out = cols[HEADS - 1]
    for h in range(HEADS - 2, -1, -1):
        out = jnp.where(head == h, cols[h], out)
    return out


def _packed_heads_attention(q, kwin, vwin, bias, sink_rows):
    t = q.shape[0]
    head = _head_of_lane()
    qs = jnp.concatenate([jnp.where(head == h, q, 0) for h in range(HEADS)], axis=0)
    s = lax.dot_general(qs, kwin, (((1,), (1,)), ((), ())), preferred_element_type=F32)
    s = s + bias.reshape(HEADS * t, bias.shape[-1])
    if sink_rows is not None:
        m = jnp.max(jnp.concatenate([jnp.maximum(s[:, :LANES], sink_rows), s[:, LANES:]], axis=1),
                    axis=1, keepdims=True)
    else:
        m = jnp.max(s, axis=1, keepdims=True)
    e = jnp.exp2(s - m)
    if sink_rows is not None:
        lane = lax.broadcasted_iota(jnp.int32, (1, LANES), 1)
        e_sink = jnp.where(lane == 0, jnp.exp2(sink_rows - m), 0.0)
        den = jnp.sum(jnp.concatenate([e[:, :LANES] + e_sink, e[:, LANES:]], axis=1), axis=1, keepdims=True)
    else:
        den = jnp.sum(e, axis=1, keepdims=True)
    o = jnp.dot(e.astype(BF16), vwin, preferred_element_type=F32) * (1.0 / den)
    out = o[(HEADS - 1) * t:]
    for h in range(HEADS - 2, -1, -1):
        out = jnp.where(head == h, o[h * t:(h + 1) * t], out)
    return out, m, den


def _banded_kernel(*refs, tile, subtiles, half, length, use_sink, want_lse):
    refs = list(refs)
    sink_ref = refs.pop(0) if use_sink else None
    q_ref, k_ref, v_ref, bias_ref, o_ref = refs[:5]
    lse_ref = refs[5] if want_lse else None
    i = pl.program_id(2)
    nt = length // tile
    width = tile + 2 * half
    head = _head_of_lane()
    for u in range(subtiles):
        t = i * subtiles + u
        main = t * tile
        can_first, can_last = u == 0, u == subtiles - 1
        if can_first or can_last:
            left = jnp.maximum(main - half, 0) if can_first else main - half
            right = jnp.minimum(main + tile, length - half) if can_last else main + tile
            left, right = pl.multiple_of(left, half), pl.multiple_of(right, half)
            main = pl.multiple_of(main, tile)

            def window(ref):
                return jnp.concatenate([ref[pl.ds(left, half), :], ref[pl.ds(main, tile), :],
                                        ref[pl.ds(right, half), :]], axis=0)

            variant = 0
            if can_first:
                variant = variant + (t == 0).astype(jnp.int32)
            if can_last:
                variant = variant + 2 * (t == nt - 1).astype(jnp.int32)
        else:
            start = pl.multiple_of(main - half, half)

            def window(ref):
                return ref[pl.ds(start, width), :]

            variant = 0
        rows = slice(u * tile, (u + 1) * tile)
        kwin, vwin = window(k_ref), window(v_ref)
        q = q_ref[rows, :]
        sink_rows = sink_ref[...] if use_sink else None
        out, m, den = _packed_heads_attention(q, kwin, vwin, bias_ref[variant], sink_rows)
        o_ref[rows, :] = out.astype(o_ref.dtype)
        if want_lse:
            lse = m + jnp.log2(den)
            lse_ref[rows, :] = _per_head_lanes([lse[h * tile:(h + 1) * tile] for h in range(HEADS)])


def _sink_rows(sink, tile):
    rows = jnp.repeat(LOG2E * sink.astype(F32), tile, axis=1)
    return jnp.broadcast_to(rows[:, :, None], rows.shape + (LANES,))


def _banded_attention(zf, blk, bias, sink_rows, layer, *, half, subtiles, want_lse):
    bsz, dilation, length, _ = zf.shape
    tile = bias.shape[2]
    subtiles = min(subtiles, length // tile)
    step = tile * subtiles
    use_sink = sink_rows is not None
    in_specs = [pl.BlockSpec((None, None, step, HEAD_BLOCK), lambda b, j, i: (b, j, i, blk)),
                pl.BlockSpec((None, None, length, HEAD_BLOCK), lambda b, j, i: (b, j, 0, blk + 1)),
                pl.BlockSpec((None, None, length, HEAD_BLOCK), lambda b, j, i: (b, j, 0, blk + 2)),
                pl.BlockSpec(bias.shape, lambda b, j, i: (0, 0, 0, 0))]
    args = [zf, zf, zf, bias]
    if use_sink:
        in_specs.insert(0, _layer_block(sink_rows, layer))
        args.insert(0, sink_rows)
    o_spec = pl.BlockSpec((None, None, step, HEAD_BLOCK), lambda b, j, i: (b, j, i, 0))
    out_specs = [o_spec]
    out_shape = [jax.ShapeDtypeStruct((bsz, dilation, length, HEAD_BLOCK), BF16)]
    if want_lse:
        out_specs.append(o_spec)
        out_shape.append(jax.ShapeDtypeStruct((bsz, dilation, length, HEAD_BLOCK), F32))
    return pl.pallas_call(
        functools.partial(_banded_kernel, tile=tile, subtiles=subtiles, half=half, length=length,
                          use_sink=use_sink, want_lse=want_lse),
        grid=(bsz, dilation, length // step),
        in_specs=in_specs,
        out_specs=out_specs,
        out_shape=out_shape,
        compiler_params=_cparams(("parallel", "parallel", "arbitrary")),
        name=f"banded_r{dilation}",
    )(*args)


NA_TILE_ROWS = 4
NA_WIN_ROWS = NA_TILE_ROWS + NA_KH
NA_SUBTILES = 4


def _na_bias_tables(rpb, rows):
    nl = rpb.shape[0]
    tr, wr = NA_TILE_ROWS, NA_WIN_ROWS
    exact = functools.partial(jnp.einsum, precision=lax.Precision.HIGHEST)
    c = np.arange(GRID_W)
    cs = np.clip(c - NA_KW // 2, 0, GRID_W - NA_KW)
    col_ok = (c[None, :] >= cs[:, None]) & (c[None, :] < cs[:, None] + NA_KW)
    dc = np.clip(c[None, :] - c[:, None], -(NA_KW - 1), NA_KW - 1) + (NA_KW - 1)
    by_col = exact("lhdn,qkn->lhdqk", rpb.astype(F32), np.eye(2 * NA_KW - 1, dtype=np.float32)[dc])
    tabs = []
    for r0, ws in ((0, 0), (tr, 0), (rows - tr, rows - wr)):
        r = r0 + np.arange(tr)
        start = np.clip(r - NA_KH // 2, 0, rows - NA_KH)
        kr = ws + np.arange(wr)
        row_ok = (kr[None, :] >= start[:, None]) & (kr[None, :] < start[:, None] + NA_KH)
        dr = np.clip(kr[None, :] - r[:, None] + (NA_KH - 1), 0, 2 * NA_KH - 2)
        bias = exact("rad,lhdqk->lhrqak", np.eye(2 * NA_KH - 1, dtype=np.float32)[dr], by_col)
        ok = row_ok[:, None, :, None] & col_ok[None, :, None, :]
        tabs.append(jnp.where(ok, LOG2E * bias, NEG_INF).reshape(nl, HEADS, tr * GRID_W, wr * GRID_W))
    return jnp.stack(tabs, axis=1)


def _na_kernel(q_ref, k_ref, v_ref, bias_ref, o_ref, *, rows):
    i = pl.program_id(1)
    nt = rows // NA_TILE_ROWS
    win = NA_WIN_ROWS * GRID_W
    tq = NA_TILE_ROWS * GRID_W
    head = _head_of_lane()
    for u in range(NA_SUBTILES):
        t = i * NA_SUBTILES + u
        ws = jnp.clip(t * NA_TILE_ROWS - NA_KH // 2, 0, rows - NA_WIN_ROWS) * GRID_W
        ws = pl.multiple_of(ws, GRID_W)
        variant = jnp.where(t == 0, 0, jnp.where(t == nt - 1, 2, 1))
        kwin = k_ref[pl.ds(ws, win), :]
        vwin = v_ref[pl.ds(ws, win), :]
        q = q_ref[u * tq:(u + 1) * tq, :]
        acc = None
        inv_den = []
        for h in range(HEADS):
            mine = head == h
            s = lax.dot_general(q, jnp.where(mine, kwin, 0), (((1,), (1,)), ((), ())),
                                preferred_element_type=F32)
            s = s + bias_ref[variant, h]
            m = jnp.max(s, axis=1, keepdims=True)
            e = jnp.exp2(s - m)
            inv_den.append(1.0 / jnp.sum(e, axis=1, keepdims=True))
            part = jnp.dot(e.astype(BF16), jnp.where(mine, vwin, 0), preferred_element_type=F32)
            acc = part if acc is None else acc + part
        o_ref[u * tq:(u + 1) * tq, :] = (acc * _per_head_lanes(inv_den)).astype(o_ref.dtype)


def _neighborhood_attention(z3, bias, layer):
    bsz, seq, _ = z3.shape
    rows = seq // GRID_W
    tq = NA_TILE_ROWS * GRID_W * NA_SUBTILES
    return pl.pallas_call(
        functools.partial(_na_kernel, rows=rows),
        grid=(bsz, rows // (NA_TILE_ROWS * NA_SUBTILES)),
        in_specs=[pl.BlockSpec((None, tq, HEAD_BLOCK), lambda b, i: (b, i, C_BLK)),
                  pl.BlockSpec((None, seq, HEAD_BLOCK), lambda b, i: (b, 0, C_BLK + 1)),
                  pl.BlockSpec((None, seq, HEAD_BLOCK), lambda b, i: (b, 0, C_BLK + 2)),
                  _layer_block(bias, layer)],
        out_specs=pl.BlockSpec((None, tq, HEAD_BLOCK), lambda b, i: (b, i, 0)),
        out_shape=jax.ShapeDtypeStruct((bsz, seq, HEAD_BLOCK), BF16),
        compiler_params=_cparams(("parallel", "arbitrary")),
        name="neighborhood",
    )(z3, z3, z3, bias)


def _unfold(ref, scr):
    dilation, n, cols = ref.shape
    if dilation == 1:
        return ref[0].astype(F32)
    nblk = cols // LANES
    for j in range(dilation):
        for c in range(nblk):
            scr[c, pl.ds(j, n, stride=dilation), :] = ref[j, :, c * LANES:(c + 1) * LANES].astype(F32)
    return jnp.concatenate([scr[c] for c in range(nblk)], axis=1)


def _merge_kernel(gate_ref, ya_ref, yb_ref, yc_ref, o0_ref, o1_ref, o2_ref, l0_ref, l1_ref, l2_ref, h_ref,
                  wa_ref, wb_ref, wo_ref, g_ref, b_ref, h_out_ref, hb_out_ref, *scr):
    dot = functools.partial(jnp.dot, preferred_element_type=F32)
    l0, l1, l2 = _unfold(l0_ref, None), _unfold(l1_ref, scr[0]), _unfold(l2_ref, scr[1])
    o0, o1, o2 = _unfold(o0_ref, None), _unfold(o1_ref, scr[2]), _unfold(o2_ref, scr[3])
    lm = jnp.maximum(jnp.maximum(l0, l1), l2)
    e0, e1, e2 = jnp.exp2(l0 - lm), jnp.exp2(l1 - lm), jnp.exp2(l2 - lm)
    yd = (e0 * o0 + e1 * o1 + e2 * o2) / (e0 + e1 + e2)
    branches = (dot(ya_ref[...], wa_ref[...]), dot(yb_ref[0], wb_ref[0]), dot(yc_ref[...], wb_ref[1]),
                dot(yd.astype(BF16), wb_ref[2]))
    merged = None
    for n in range(N_BRANCH):
        gate2 = 1.0 + jnp.tanh(gate_ref[:, n * D_MODEL:(n + 1) * D_MODEL].astype(F32))
        term = gate2 * branches[n]
        merged = term if merged is None else merged + term
    u = ALPHA * h_ref[...] + dot(merged.astype(BF16), wo_ref[...])
    y = _layer_norm(u, g_ref[...], b_ref[...])
    h_out_ref[...] = y
    hb_out_ref[...] = y.astype(BF16)


def _merge(z3, ya, yb, yc, od, ld, h3, params, layer, tm=512):
    bsz, seq, _ = h3.shape

    def rows(width):
        return pl.BlockSpec((None, tm, width), lambda b, i: (b, i, 0))

    def folded(t):
        dilation = t.shape[1]
        return pl.BlockSpec((None, dilation, tm // dilation, HEAD_BLOCK), lambda b, i: (b, 0, i, 0))

    return pl.pallas_call(
        _merge_kernel,
        grid=(bsz, seq // tm),
        in_specs=[rows(GATE_COLS), rows(MLA_W), folded(yb), rows(HEAD_BLOCK)]
        + [folded(t) for t in od] + [folded(t) for t in ld]
        + [rows(D_MODEL)] + [_layer_block(t, layer) for t in params],
        out_specs=[rows(D_MODEL), rows(D_MODEL)],
        out_shape=[jax.ShapeDtypeStruct((bsz, seq, D_MODEL), F32), jax.ShapeDtypeStruct((bsz, seq, D_MODEL), BF16)],
        scratch_shapes=[pltpu.VMEM((HEAD_BLOCK // LANES, tm, LANES), F32)] * 4,
        compiler_params=_cparams(("parallel", "parallel")),
        name="merge",
    )(z3, ya, yb, yc, *od, *ld, h3, *params)


FF_CHUNK = 1024


def _ffn_kernel(h_ref, hb_ref, p_ref, w1_ref, w2_ref, wp_ref, wg_ref, g_ref, b_ref, h_out_ref, hb_out_ref):
    dot = functools.partial(jnp.dot, preferred_element_type=F32)
    hb = hb_ref[...]
    ff = None
    for c in range(D_FF // FF_CHUNK):
        sl = slice(c * FF_CHUNK, (c + 1) * FF_CHUNK)
        a = jnp.maximum(dot(hb, w1_ref[:, sl]), 0.0)
        part = dot((a * a).astype(BF16), w2_ref[sl, :])
        ff = part if ff is None else ff + part
    ple = dot(p_ref[...].astype(BF16), wp_ref[...]) * (1.0 + jnp.tanh(dot(hb, wg_ref[...])))
    y = _layer_norm(ALPHA * h_ref[...] + ff + ple, g_ref[...], b_ref[...])
    h_out_ref[...] = y
    hb_out_ref[...] = y.astype(BF16)


def _ffn(h, hb, p3, params, layer, tm=256):
    m = h.shape[0]

    def rows(width):
        return pl.BlockSpec((tm, width), lambda i: (i, 0))

    return pl.pallas_call(
        _ffn_kernel,
        grid=(m // tm,),
        in_specs=[rows(D_MODEL), rows(D_MODEL), pl.BlockSpec((None, tm, PLE_DIM), lambda i: (layer, i, 0))]
        + [_layer_block(t, layer) for t in params],
        out_specs=[rows(D_MODEL), rows(D_MODEL)],
        out_shape=[jax.ShapeDtypeStruct((m, D_MODEL), F32), jax.ShapeDtypeStruct((m, D_MODEL), BF16)],
        compiler_params=_cparams(("parallel",)),
        name="ffn",
    )(h, hb, p3, *params)


SWA_TILE, SWA_SUBTILES = 256, 4
DIL_TILE, DIL_SUBTILES = 128, 8


def kernel(x, p, ln_emb_g, ln_emb_b, rel_bias, w_in, mla_q_norm, mla_w_uq, mla_kv_norm, mla_w_ukv, swa_sink,
           na_rpb, w_branch, w_out, ln1_g, ln1_b, w_ff1, w_ff2, w_ple, w_ple_gate, ln2_g, ln2_b):
    bsz, seq, _ = x.shape
    m = bsz * seq
    rows = seq // GRID_W
    cos_t, sin_t = _rope_tables(seq)
    swa_bias = _band_bias_tables(rel_bias, 0, min(SWA_TILE, seq), SWA_HALF, 1)
    dil_bias = []
    for g, (window, r) in enumerate(DIL_CONFIGS):
        half = window // (2 * r)
        dil_bias.append(_band_bias_tables(rel_bias, SWA_HEADS + g * DIL_HEADS, min(DIL_TILE, seq // r), half, r))

    nl = w_in.shape[0]
    w_main, w_dil = _prep_w_in(w_in)
    mla_weights = _prep_mla_weights(mla_w_uq, mla_w_ukv)
    g_q, g_kv = mla_q_norm.reshape(nl, 1, Q_LORA), mla_kv_norm.reshape(nl, 1, KV_LORA)
    swa_sinks = _sink_rows(swa_sink, swa_bias.shape[2])
    na_bias = _na_bias_tables(na_rpb, rows)
    wa = jnp.pad(w_branch[:, 0].reshape(nl, MLA_HEADS, MLA_V, D_MODEL),
                 ((0, 0), (0, 0), (0, MLA_HEAD_PAD - MLA_V), (0, 0))).reshape(nl, MLA_W, D_MODEL).astype(BF16)
    merge_params = (wa, w_branch[:, 1:].astype(BF16), (0.5 * w_out).astype(BF16),
                    ln1_g.reshape(nl, 1, D_MODEL), ln1_b.reshape(nl, 1, D_MODEL))
    ffn_params = (w_ff1.astype(BF16), w_ff2.astype(BF16), (0.5 * w_ple).astype(BF16),
                  (0.5 * w_ple_gate).astype(BF16), ln2_g.reshape(nl, 1, D_MODEL), ln2_b.reshape(nl, 1, D_MODEL))
    p3 = p.reshape(nl, m, PLE_DIM)

    h, hb = _embed_layer_norm(x.reshape(m, D_MODEL), ln_emb_g, ln_emb_b)
    for i in range(DEPTH):
        z3 = _matmul(hb, w_main, i, 1024, 2304, BF16, "in_proj").reshape(bsz, seq, Z_COLS)
        z4 = z3.reshape(bsz, 1, seq, Z_COLS)
        hb3 = hb.reshape(bsz, seq, D_MODEL)
        q_a, k_a, vt_a = _mla_prep(z3, cos_t, sin_t, g_q, g_kv, mla_weights, i)
        y_a = _mla_flash(q_a, k_a, vt_a)
        (y_b,) = _banded_attention(z4, B_BLK, swa_bias, swa_sinks, i,
                                   half=SWA_HALF, subtiles=SWA_SUBTILES, want_lse=False)
        y_c = _neighborhood_attention(z3, na_bias, i)
        o_d, l_d = [], []
        for g, (window, r) in enumerate(DIL_CONFIGS):
            if r == 1:
                zd, blk = z4, D1_BLK
            else:
                zd, blk = _matmul_fold(hb3, w_dil, i, g - 1, r, 1024, f"in_proj_r{r}"), 0
            o_g, l_g = _banded_attention(zd, blk, dil_bias[g], None, i, half=window // (2 * r),
                                         subtiles=DIL_SUBTILES, want_lse=True)
            o_d.append(o_g)
            l_d.append(l_g)
        h, hb = _merge(z3, y_a, y_b, y_c, o_d, l_d, h.reshape(bsz, seq, D_MODEL), merge_params, i)
        h, hb = _ffn(h.reshape(m, D_MODEL), hb.reshape(m, D_MODEL), p3, ffn_params, i)
    return h.reshape(bsz, seq, D_MODEL)
```

```python
import functools
import math

import jax
import jax.numpy as jnp
import numpy as np
from jax import lax
from jax.experimental import pallas as pl
from jax.experimental.pallas import tpu as pltpu

F32 = jnp.float32
BF16 = jnp.bfloat16

D_MODEL = 1024
DEPTH = 2
HEAD_DIM = 64
GRID_W = 64
NEG_INF = -1e30

MLA_HEADS = 4
MLA_NOPE = 64
MLA_ROPE = 32
MLA_V = 64
Q_LORA = 256
KV_LORA = 128
ROPE_THETA = 10000.0

SWA_HEADS = 4
SWA_KV_HEADS = 2
SWA_HALF = 128

NA_HEADS = 4
NA_KH = 8
NA_KW = 16

DIL_CONFIGS = ((128, 1), (512, 4), (2048, 16))
DIL_GROUPS = 3
DIL_HEADS = 4

T5_BUCKETS = 32
T5_MAX_DIST = 1024

N_BRANCH = 4
BRANCH_W = 256
D_FF = 4 * D_MODEL
PLE_DIM = 256

ALPHA = (2 * DEPTH) ** 0.25

LANES = 128
HEADS = 4
HEAD_BLOCK = HEADS * HEAD_DIM

GATE_COLS = N_BRANCH * D_MODEL
A_COL = GATE_COLS
A_WIDTH = 512
B_BLK = (A_COL + A_WIDTH) // HEAD_BLOCK
C_BLK = B_BLK + 3
D1_BLK = C_BLK + 3
Z_BLKS = D1_BLK + 3
Z_COLS = Z_BLKS * HEAD_BLOCK
QKV_COLS = 3 * HEAD_BLOCK
assert DIL_CONFIGS[0][1] == 1 and all(r > 1 for _, r in DIL_CONFIGS[1:])

MLA_HEAD_PAD = 128
MLA_W = MLA_HEADS * MLA_HEAD_PAD
LOG2E = math.log2(math.e)
QK_SCALE = HEAD_DIM ** -0.5 * LOG2E

VMEM_LIMIT = 56 * 1024 * 1024


def _cparams(sem):
    return pltpu.CompilerParams(dimension_semantics=sem, vmem_limit_bytes=VMEM_LIMIT)


def _layer_norm(u, g, b, eps=1e-5):
    mu = jnp.mean(u, axis=-1, keepdims=True)
    d = u - mu
    var = jnp.mean(d * d, axis=-1, keepdims=True)
    return d * lax.rsqrt(var + eps) * g + b


def _rms_norm(u, g, eps=1e-6):
    return u * lax.rsqrt(jnp.mean(u * u, axis=-1, keepdims=True) + eps) * g


def _ln_kernel(x_ref, g_ref, b_ref, hb_ref):
    hb_ref[...] = _layer_norm(x_ref[...], g_ref[...], b_ref[...]).astype(BF16)


def _embed_layer_norm(x2, g, b, tm=512):
    m, d = x2.shape
    row = pl.BlockSpec((tm, d), lambda i: (i, 0))
    vec = pl.BlockSpec((1, d), lambda i: (0, 0))
    return pl.pallas_call(
        _ln_kernel,
        grid=(m // tm,),
        in_specs=[row, vec, vec],
        out_specs=row,
        out_shape=jax.ShapeDtypeStruct((m, d), BF16),
        compiler_params=_cparams(("parallel",)),
        name="embed_ln",
    )(x2, g, b)


def _mm_kernel(a_ref, w_ref, o_ref):
    o_ref[...] = jnp.dot(a_ref[...], w_ref[...], preferred_element_type=F32).astype(o_ref.dtype)


def _layer_block(arr, layer):
    zeros = (0,) * (arr.ndim - 1)
    return pl.BlockSpec((None,) + arr.shape[1:], lambda *_: (layer,) + zeros)


def _matmul(a, w, layer, tm, tn, out_dtype, name):
    m, k = a.shape
    n = w.shape[2]
    return pl.pallas_call(
        _mm_kernel,
        grid=(m // tm, n // tn),
        in_specs=[pl.BlockSpec((tm, k), lambda i, j: (i, 0)),
                  pl.BlockSpec((None, k, tn), lambda i, j: (layer, 0, j))],
        out_specs=pl.BlockSpec((tm, tn), lambda i, j: (i, j)),
        out_shape=jax.ShapeDtypeStruct((m, n), out_dtype),
        compiler_params=_cparams(("parallel", "arbitrary")),
        name=name,
    )(a, w)


def _prep_w_in(w):
    nl = w.shape[0]
    o = 0
    a = w[..., o:o + Q_LORA + KV_LORA + MLA_ROPE]
    o += Q_LORA + KV_LORA + MLA_ROPE
    bq = w[..., o:o + HEAD_BLOCK]
    o += HEAD_BLOCK
    kvw = SWA_KV_HEADS * HEAD_DIM
    bk = w[..., o:o + kvw]
    o += kvw
    bv = w[..., o:o + kvw]
    o += kvw
    c = w[..., o:o + 3 * HEAD_BLOCK]
    o += 3 * HEAD_BLOCK
    d = w[..., o:o + 3 * DIL_GROUPS * HEAD_BLOCK]
    o += 3 * DIL_GROUPS * HEAD_BLOCK
    gates = w[..., o:]
    rep = SWA_HEADS // SWA_KV_HEADS

    def expand(t):
        t = t.reshape(nl, D_MODEL, SWA_KV_HEADS, 1, HEAD_DIM)
        return jnp.broadcast_to(t, (nl, D_MODEL, SWA_KV_HEADS, rep, HEAD_DIM)).reshape(nl, D_MODEL, HEAD_BLOCK)

    a = jnp.pad(a, ((0, 0), (0, 0), (0, A_WIDTH - a.shape[-1])))
    qkv_scale = jnp.concatenate([jnp.full((HEAD_BLOCK,), QK_SCALE, F32), jnp.ones((2 * HEAD_BLOCK,), F32)])
    c = c * qkv_scale
    d = d.reshape(nl, D_MODEL, 3, DIL_GROUPS, HEAD_BLOCK).transpose(0, 3, 1, 2, 4)
    d = d.reshape(nl, DIL_GROUPS, D_MODEL, QKV_COLS) * qkv_scale
    main = jnp.concatenate([0.5 * gates, a, QK_SCALE * bq, expand(bk), expand(bv), c, d[:, 0]], axis=-1)
    return main.astype(BF16), d[:, 1:].astype(BF16)


FOLD_STRIDE = 4


def _mm_fold_kernel(a_ref, w_ref, o_ref, acc_ref, tmp_ref, *, dilation):
    acc = jnp.dot(a_ref[...], w_ref[...], preferred_element_type=F32)
    nblk, tm, _ = acc_ref.shape
    for c in range(nblk):
        acc_ref[c] = acc[:, c * LANES:(c + 1) * LANES]
    if dilation <= FOLD_STRIDE:
        n = tm // dilation
        for j in range(dilation):
            for c in range(nblk):
                o_ref[j, :, c * LANES:(c + 1) * LANES] = (
                    acc_ref[c, pl.ds(j, n, stride=dilation), :].astype(o_ref.dtype))
    else:
        outer = dilation // FOLD_STRIDE
        n1, n = tm // FOLD_STRIDE, tm // dilation
        for c in range(nblk):
            for j1 in range(FOLD_STRIDE):
                tmp_ref[c, j1 * n1:(j1 + 1) * n1, :] = acc_ref[c, pl.ds(j1, n1, stride=FOLD_STRIDE), :]
        for j2 in range(outer):
            for j1 in range(FOLD_STRIDE):
                for c in range(nblk):
                    o_ref[FOLD_STRIDE * j2 + j1, :, c * LANES:(c + 1) * LANES] = (
                        tmp_ref[c, pl.ds(j1 * n1 + j2, n, stride=outer), :].astype(o_ref.dtype))


def _matmul_fold(a3, w, layer, group, dilation, tm, name):
    bsz, seq, k = a3.shape
    n = w.shape[-1]
    assert dilation <= FOLD_STRIDE or (dilation % FOLD_STRIDE == 0 and dilation // FOLD_STRIDE <= FOLD_STRIDE)
    slab = pltpu.VMEM((n // LANES, tm, LANES), F32)
    return pl.pallas_call(
        functools.partial(_mm_fold_kernel, dilation=dilation),
        grid=(bsz, seq // tm),
        in_specs=[pl.BlockSpec((None, tm, k), lambda b, i: (b, i, 0)),
                  pl.BlockSpec((None, None, k, n), lambda b, i: (layer, group, 0, 0))],
        out_specs=pl.BlockSpec((None, dilation, tm // dilation, n), lambda b, i: (b, 0, i, 0)),
        out_shape=jax.ShapeDtypeStruct((bsz, dilation, seq // dilation, n), BF16),
        scratch_shapes=[slab, slab],
        compiler_params=_cparams(("parallel", "parallel")),
        name=name,
    )(a3, w)


def _rope_tables(seq):
    half = MLA_ROPE // 2
    inv = ROPE_THETA ** (-jnp.arange(half, dtype=F32) / half)
    ang = jnp.arange(seq).astype(F32)[:, None] * inv[None, :]
    cos, sin = jnp.cos(ang), jnp.sin(ang)
    pad = MLA_HEAD_PAD - MLA_NOPE - MLA_ROPE
    cos_t = jnp.concatenate([jnp.ones((seq, MLA_NOPE), F32), cos, cos, jnp.zeros((seq, pad), F32)], axis=1)
    sin_t = jnp.concatenate([jnp.zeros((seq, MLA_NOPE), F32), sin, sin, jnp.zeros((seq, pad), F32)], axis=1)
    return cos_t, sin_t


def _rot_cols(w_rope):
    half = MLA_ROPE // 2
    return jnp.concatenate([-w_rope[..., half:], w_rope[..., :half]], axis=-1)


def _prep_mla_weights(w_uq, w_ukv):
    nl = w_uq.shape[0]
    pad = MLA_HEAD_PAD - MLA_NOPE - MLA_ROPE
    lead = ((0, 0), (0, 0), (0, 0))
    wq = w_uq.reshape(nl, Q_LORA, MLA_HEADS, MLA_NOPE + MLA_ROPE)
    wq_main = jnp.pad(wq, lead + ((0, pad),)).reshape(nl, Q_LORA, MLA_W)
    wq_rot = jnp.concatenate([jnp.zeros((nl, Q_LORA, MLA_HEADS, MLA_NOPE), F32), _rot_cols(wq[..., MLA_NOPE:]),
                              jnp.zeros((nl, Q_LORA, MLA_HEADS, pad), F32)], axis=-1).reshape(nl, Q_LORA, MLA_W)
    wkv = w_ukv.reshape(nl, KV_LORA, MLA_HEADS, MLA_NOPE + MLA_V)
    wk_n = jnp.pad(wkv[..., :MLA_NOPE], lead + ((0, MLA_HEAD_PAD - MLA_NOPE),)).reshape(nl, KV_LORA, MLA_W)
    wv = jnp.pad(wkv[..., MLA_NOPE:], lead + ((0, MLA_HEAD_PAD - MLA_V),)).reshape(nl, KV_LORA, MLA_W)
    eye = jnp.eye(MLA_ROPE, dtype=F32)
    place = jnp.concatenate([jnp.zeros((MLA_ROPE, MLA_NOPE), F32), eye, jnp.zeros((MLA_ROPE, pad), F32)], axis=1)
    place_rot = jnp.concatenate([jnp.zeros((MLA_ROPE, MLA_NOPE), F32), _rot_cols(eye),
                                 jnp.zeros((MLA_ROPE, pad), F32)], axis=1)

    def lift(pm):
        pm = jnp.pad(pm, ((0, LANES - MLA_ROPE), (0, 0)))
        return jnp.tile(pm, (1, MLA_HEADS))

    def per_layer(t):
        return jnp.broadcast_to(t, (nl,) + t.shape)

    wk_main = jnp.concatenate([wk_n, per_layer(lift(place))], axis=1)
    wk_rot = per_layer(jnp.concatenate([jnp.zeros((KV_LORA, MLA_W), F32), lift(place_rot)], axis=0))
    return tuple(t.astype(BF16) for t in (wq_main, wq_rot, wk_main, wk_rot, wv))


def _mla_prep_kernel(za_ref, cos_ref, sin_ref, gq_ref, gkv_ref, wq_ref, wqr_ref, wk_ref, wkr_ref, wv_ref,
                     q_ref, k_ref, vt_ref):
    za = za_ref[...]
    cq = _rms_norm(za[:, :Q_LORA].astype(F32), gq_ref[...]).astype(BF16)
    ckv = _rms_norm(za[:, Q_LORA:Q_LORA + KV_LORA].astype(F32), gkv_ref[...]).astype(BF16)
    cos = jnp.concatenate([cos_ref[...]] * MLA_HEADS, axis=1)
    sin = jnp.concatenate([sin_ref[...]] * MLA_HEADS, axis=1)
    dot = functools.partial(jnp.dot, preferred_element_type=F32)
    q = dot(cq, wq_ref[...]) * cos + dot(cq, wqr_ref[...]) * sin
    q_ref[...] = (q * ((MLA_NOPE + MLA_ROPE) ** -0.5 * LOG2E)).astype(BF16)
    lhs = jnp.concatenate([ckv, za[:, Q_LORA + KV_LORA:]], axis=1)
    k_ref[...] = (dot(lhs, wk_ref[...]) * cos + dot(lhs, wkr_ref[...]) * sin).astype(BF16)
    lane = lax.broadcasted_iota(jnp.int32, (1, MLA_W), 1)
    ones_col = jnp.where(lane % MLA_HEAD_PAD == MLA_V, 1.0, 0.0)
    vt_ref[...] = (dot(ckv, wv_ref[...]) + ones_col).T.astype(BF16)


def _mla_prep(z3, cos_t, sin_t, g_q, g_kv, weights, layer, tm=1024):
    bsz, seq, _ = z3.shape
    tm = min(tm, seq)
    params = (g_q, g_kv) + tuple(weights)
    tab = pl.BlockSpec((tm, MLA_HEAD_PAD), lambda b, i: (i, 0))
    out = pl.BlockSpec((None, tm, MLA_W), lambda b, i: (b, i, 0))
    shp = jax.ShapeDtypeStruct((bsz, seq, MLA_W), BF16)
    return pl.pallas_call(
        _mla_prep_kernel,
        grid=(bsz, seq // tm),
        in_specs=[pl.BlockSpec((None, tm, A_WIDTH), lambda b, i: (b, i, A_COL // A_WIDTH)), tab, tab]
        + [_layer_block(t, layer) for t in params],
        out_specs=[out, out, pl.BlockSpec((None, MLA_W, tm), lambda b, i: (b, 0, i))],
        out_shape=[shp, shp, jax.ShapeDtypeStruct((bsz, MLA_W, seq), BF16)],
        compiler_params=_cparams(("parallel", "parallel")),
        name="mla_prep",
    )(z3, cos_t, sin_t, *params)


def _mla_flash_kernel(q_ref, k_ref, vt_ref, o_ref, m_scr, acc_scr, s_scr, p_scr):
    ki = pl.program_id(2)

    @pl.when(ki == 0)
    def _():
        m_scr[...] = jnp.full(m_scr.shape, -jnp.inf, F32)
        acc_scr[...] = jnp.zeros(acc_scr.shape, F32)

    heads = [slice(h * MLA_HEAD_PAD, (h + 1) * MLA_HEAD_PAD) for h in range(MLA_HEADS)]
    for h, sl in enumerate(heads):
        s_scr[h] = lax.dot_general(k_ref[:, sl], q_ref[:, sl], (((1,), (1,)), ((), ())),
                                   preferred_element_type=F32)
    alphas = []
    for h in range(MLA_HEADS):
        s = s_scr[h]
        m_prev = m_scr[h]
        m_new = jnp.maximum(m_prev, jnp.max(s, axis=0, keepdims=True))
        p_scr[h] = jnp.exp2(s - m_new).astype(BF16)
        alphas.append(jnp.exp2(m_prev - m_new))
        m_scr[h] = m_new
    for h, sl in enumerate(heads):
        acc_scr[h] = alphas[h] * acc_scr[h] + jnp.dot(vt_ref[sl, :], p_scr[h], preferred_element_type=F32)

    @pl.when(ki == pl.num_programs(2) - 1)
    def _():
        for h, sl in enumerate(heads):
            acc = acc_scr[h]
            o_ref[:, sl] = (acc / acc[MLA_V:MLA_V + 1, :]).T.astype(o_ref.dtype)


def _mla_flash(q, k, vt, tq=512, tk=2048):
    bsz, seq, _ = q.shape
    tq, tk = min(tq, seq), min(tk, seq)
    return pl.pallas_call(
        _mla_flash_kernel,
        grid=(bsz, seq // tq, seq // tk),
        in_specs=[pl.BlockSpec((None, tq, MLA_W), lambda b, i, j: (b, i, 0)),
                  pl.BlockSpec((None, tk, MLA_W), lambda b, i, j: (b, j, 0)),
                  pl.BlockSpec((None, MLA_W, tk), lambda b, i, j: (b, 0, j))],
        out_specs=pl.BlockSpec((None, tq, MLA_W), lambda b, i, j: (b, i, 0)),
        out_shape=jax.ShapeDtypeStruct((bsz, seq, MLA_W), BF16),
        scratch_shapes=[pltpu.VMEM((MLA_HEADS, 1, tq), F32),
                        pltpu.VMEM((MLA_HEADS, MLA_HEAD_PAD, tq), F32),
                        pltpu.VMEM((MLA_HEADS, tk, tq), F32),
                        pltpu.VMEM((MLA_HEADS, tk, tq), BF16)],
        compiler_params=_cparams(("parallel", "parallel", "arbitrary")),
        name="mla_flash",
    )(q, k, vt)


def _t5_bucket(rel):
    nb = T5_BUCKETS // 2
    max_exact = nb // 2
    n = jnp.abs(rel)
    large = max_exact + (jnp.log(jnp.maximum(n, 1).astype(F32) / max_exact)
                         / math.log(T5_MAX_DIST / max_exact) * (nb - max_exact)).astype(jnp.int32)
    large = jnp.minimum(large, nb - 1)
    return jnp.where(rel > 0, nb, 0) + jnp.where(n < max_exact, n, large)


def _band_bias_tables(rel_bias, head_lo, tile, half, dilation):
    width = tile + 2 * half
    col = jnp.arange(width)[None, :]
    rel = col - half - jnp.arange(tile)[:, None]
    onehot = jax.nn.one_hot(_t5_bucket(rel * dilation), T5_BUCKETS, dtype=F32)
    bias = jnp.einsum("qkn,nh->hqk", onehot, rel_bias[:, head_lo:head_lo + HEADS].astype(F32),
                      precision=lax.Precision.HIGHEST)
    band = jnp.abs(rel) <= half
    left_ok = col >= half
    right_ok = col < tile + half
    masks = jnp.stack([band, band & left_ok, band & right_ok, band & left_ok & right_ok])
    return jnp.where(masks[:, None], LOG2E * bias[None].astype(F32), NEG_INF)


def _head_of_lane():
    return lax.broadcasted_iota(jnp.int32, (1, HEAD_BLOCK), 1) // HEAD_DIM


def _per_head_lanes(cols):
    head = _head_of_lane()
    out = cols[HEADS - 1]
    for h in range(HEADS - 2, -1, -1):
        out = jnp.where(head == h, cols[h], out)
    return out


def _packed_heads_attention(q, kwin, vwin, bias, sink_rows):
    t = q.shape[0]
    head = _head_of_lane()
    qs = jnp.concatenate([jnp.where(head == h, q, 0) for h in range(HEADS)], axis=0)
    s = lax.dot_general(qs, kwin, (((1,), (1,)), ((), ())), preferred_element_type=F32)
    s = s + bias.reshape(HEADS * t, bias.shape[-1])
    if sink_rows is not None:
        m = jnp.max(jnp.concatenate([jnp.maximum(s[:, :LANES], sink_rows), s[:, LANES:]], axis=1),
                    axis=1, keepdims=True)
    else:
        m = jnp.max(s, axis=1, keepdims=True)
    e = jnp.exp2(s - m)
    if sink_rows is not None:
        lane = lax.broadcasted_iota(jnp.int32, (1, LANES), 1)
        e_sink = jnp.where(lane == 0, jnp.exp2(sink_rows - m), 0.0)
        den = jnp.sum(jnp.concatenate([e[:, :LANES] + e_sink, e[:, LANES:]], axis=1), axis=1, keepdims=True)
    else:
        den = jnp.sum(e, axis=1, keepdims=True)
    o = jnp.dot(e.astype(BF16), vwin, preferred_element_type=F32) * (1.0 / den)
    out = o[(HEADS - 1) * t:]
    for h in range(HEADS - 2, -1, -1):
        out = jnp.where(head == h, o[h * t:(h + 1) * t], out)
    return out, m, den


def _banded_kernel(*refs, tile, subtiles, half, length, use_sink, want_lse):
    refs = list(refs)
    sink_ref = refs.pop(0) if use_sink else None
    q_ref, k_ref, v_ref, bias_ref, o_ref = refs[:5]
    lse_ref = refs[5] if want_lse else None
    i = pl.program_id(2)
    nt = length // tile
    width = tile + 2 * half
    head = _head_of_lane()
    for u in range(subtiles):
        t = i * subtiles + u
        main = t * tile
        can_first, can_last = u == 0, u == subtiles - 1
        if can_first or can_last:
            left = jnp.maximum(main - half, 0) if can_first else main - half
            right = jnp.minimum(main + tile, length - half) if can_last else main + tile
            left, right = pl.multiple_of(left, half), pl.multiple_of(right, half)
            main = pl.multiple_of(main, tile)

            def window(ref):
                return jnp.concatenate([ref[pl.ds(left, half), :], ref[pl.ds(main, tile), :],
                                        ref[pl.ds(right, half), :]], axis=0)

            variant = 0
            if can_first:
                variant = variant + (t == 0).astype(jnp.int32)
            if can_last:
                variant = variant + 2 * (t == nt - 1).astype(jnp.int32)
        else:
            start = pl.multiple_of(main - half, half)

            def window(ref):
                return ref[pl.ds(start, width), :]

            variant = 0
        rows = slice(u * tile, (u + 1) * tile)
        kwin, vwin = window(k_ref), window(v_ref)
        q = q_ref[rows, :]
        sink_rows = sink_ref[...] if use_sink else None
        out, m, den = _packed_heads_attention(q, kwin, vwin, bias_ref[variant], sink_rows)
        o_ref[rows, :] = out.astype(o_ref.dtype)
        if want_lse:
            lse = m + jnp.log2(den)
            lse_ref[rows, :] = _per_head_lanes([lse[h * tile:(h + 1) * tile] for h in range(HEADS)])


def _sink_rows(sink, tile):
    rows = jnp.repeat(LOG2E * sink.astype(F32), tile, axis=1)
    return jnp.broadcast_to(rows[:, :, None], rows.shape + (LANES,))


def _banded_attention(zf, blk, bias, sink_rows, layer, *, half, subtiles, want_lse):
    bsz, dilation, length, _ = zf.shape
    tile = bias.shape[2]
    subtiles = min(subtiles, length // tile)
    step = tile * subtiles
    use_sink = sink_rows is not None
    in_specs = [pl.BlockSpec((None, None, step, HEAD_BLOCK), lambda b, j, i: (b, j, i, blk)),
                pl.BlockSpec((None, None, length, HEAD_BLOCK), lambda b, j, i: (b, j, 0, blk + 1)),
                pl.BlockSpec((None, None, length, HEAD_BLOCK), lambda b, j, i: (b, j, 0, blk + 2)),
                pl.BlockSpec(bias.shape, lambda b, j, i: (0, 0, 0, 0))]
    args = [zf, zf, zf, bias]
    if use_sink:
        in_specs.insert(0, _layer_block(sink_rows, layer))
        args.insert(0, sink_rows)
    o_spec = pl.BlockSpec((None, None, step, HEAD_BLOCK), lambda b, j, i: (b, j, i, 0))
    out_specs = [o_spec]
    out_shape = [jax.ShapeDtypeStruct((bsz, dilation, length, HEAD_BLOCK), BF16)]
    if want_lse:
        out_specs.append(o_spec)
        out_shape.append(jax.ShapeDtypeStruct((bsz, dilation, length, HEAD_BLOCK), F32))
    return pl.pallas_call(
        functools.partial(_banded_kernel, tile=tile, subtiles=subtiles, half=half, length=length,
                          use_sink=use_sink, want_lse=want_lse),
        grid=(bsz, dilation, length // step),
        in_specs=in_specs,
        out_specs=out_specs,
        out_shape=out_shape,
        compiler_params=_cparams(("parallel", "parallel", "arbitrary")),
        name=f"banded_r{dilation}",
    )(*args)


NA_TILE_ROWS = 4
NA_WIN_ROWS = NA_TILE_ROWS + NA_KH
NA_SUBTILES = 4


def _na_bias_tables(rpb, rows):
    nl = rpb.shape[0]
    tr, wr = NA_TILE_ROWS, NA_WIN_ROWS
    exact = functools.partial(jnp.einsum, precision=lax.Precision.HIGHEST)
    c = np.arange(GRID_W)
    cs = np.clip(c - NA_KW // 2, 0, GRID_W - NA_KW)
    col_ok = (c[None, :] >= cs[:, None]) & (c[None, :] < cs[:, None] + NA_KW)
    dc = np.clip(c[None, :] - c[:, None], -(NA_KW - 1), NA_KW - 1) + (NA_KW - 1)
    by_col = exact("lhdn,qkn->lhdqk", rpb.astype(F32), np.eye(2 * NA_KW - 1, dtype=np.float32)[dc])
    tabs = []
    for r0, ws in ((0, 0), (tr, 0), (rows - tr, rows - wr)):
        r = r0 + np.arange(tr)
        start = np.clip(r - NA_KH // 2, 0, rows - NA_KH)
        kr = ws + np.arange(wr)
        row_ok = (kr[None, :] >= start[:, None]) & (kr[None, :] < start[:, None] + NA_KH)
        dr = np.clip(kr[None, :] - r[:, None] + (NA_KH - 1), 0, 2 * NA_KH - 2)
        bias = exact("rad,lhdqk->lhrqak", np.eye(2 * NA_KH - 1, dtype=np.float32)[dr], by_col)
        ok = row_ok[:, None, :, None] & col_ok[None, :, None, :]
        tabs.append(jnp.where(ok, LOG2E * bias, NEG_INF).reshape(nl, HEADS, tr * GRID_W, wr * GRID_W))
    return jnp.stack(tabs, axis=1)


def _na_kernel(q_ref, k_ref, v_ref, bias_ref, o_ref, *, rows):
    i = pl.program_id(1)
    nt = rows // NA_TILE_ROWS
    win = NA_WIN_ROWS * GRID_W
    tq = NA_TILE_ROWS * GRID_W
    head = _head_of_lane()
    for u in range(NA_SUBTILES):
        t = i * NA_SUBTILES + u
        ws = jnp.clip(t * NA_TILE_ROWS - NA_KH // 2, 0, rows - NA_WIN_ROWS) * GRID_W
        ws = pl.multiple_of(ws, GRID_W)
        variant = jnp.where(t == 0, 0, jnp.where(t == nt - 1, 2, 1))
        kwin = k_ref[pl.ds(ws, win), :]
        vwin = v_ref[pl.ds(ws, win), :]
        q = q_ref[u * tq:(u + 1) * tq, :]
        acc = None
        inv_den = []
        for h in range(HEADS):
            mine = head == h
            s = lax.dot_general(q, jnp.where(mine, kwin, 0), (((1,), (1,)), ((), ())),
                                preferred_element_type=F32)
            s = s + bias_ref[variant, h]
            m = jnp.max(s, axis=1, keepdims=True)
            e = jnp.exp2(s - m)
            inv_den.append(1.0 / jnp.sum(e, axis=1, keepdims=True))
            part = jnp.dot(e.astype(BF16), jnp.where(mine, vwin, 0), preferred_element_type=F32)
            acc = part if acc is None else acc + part
        o_ref[u * tq:(u + 1) * tq, :] = (acc * _per_head_lanes(inv_den)).astype(o_ref.dtype)


def _neighborhood_attention(z3, bias, layer):
    bsz, seq, _ = z3.shape
    rows = seq // GRID_W
    tq = NA_TILE_ROWS * GRID_W * NA_SUBTILES
    return pl.pallas_call(
        functools.partial(_na_kernel, rows=rows),
        grid=(bsz, rows // (NA_TILE_ROWS * NA_SUBTILES)),
        in_specs=[pl.BlockSpec((None, tq, HEAD_BLOCK), lambda b, i: (b, i, C_BLK)),
                  pl.BlockSpec((None, seq, HEAD_BLOCK), lambda b, i: (b, 0, C_BLK + 1)),
                  pl.BlockSpec((None, seq, HEAD_BLOCK), lambda b, i: (b, 0, C_BLK + 2)),
                  _layer_block(bias, layer)],
        out_specs=pl.BlockSpec((None, tq, HEAD_BLOCK), lambda b, i: (b, i, 0)),
        out_shape=jax.ShapeDtypeStruct((bsz, seq, HEAD_BLOCK), BF16),
        compiler_params=_cparams(("parallel", "arbitrary")),
        name="neighborhood",
    )(z3, z3, z3, bias)


def _unfold(ref, scr):
    dilation, n, cols = ref.shape
    if dilation == 1:
        return ref[0].astype(F32)
    nblk = cols // LANES
    for j in range(dilation):
        for c in range(nblk):
            scr[c, pl.ds(j, n, stride=dilation), :] = ref[j, :, c * LANES:(c + 1) * LANES].astype(F32)
    return jnp.concatenate([scr[c] for c in range(nblk)], axis=1)


def _merge_rows(gate_ref, ya_ref, yb_ref, yc_ref, o0_ref, o1_ref, o2_ref, l0_ref, l1_ref, l2_ref, h_ref,
                eg_ref, eb_ref, wa_ref, wb_ref, wo_ref, g_ref, b_ref, scr, first_layer):
    dot = functools.partial(jnp.dot, preferred_element_type=F32)
    l0, l1, l2 = _unfold(l0_ref, None), _unfold(l1_ref, scr[0]), _unfold(l2_ref, scr[1])
    o0, o1, o2 = _unfold(o0_ref, None), _unfold(o1_ref, scr[2]), _unfold(o2_ref, scr[3])
    lm = jnp.maximum(jnp.maximum(l0, l1), l2)
    e0, e1, e2 = jnp.exp2(l0 - lm), jnp.exp2(l1 - lm), jnp.exp2(l2 - lm)
    yd = (e0 * o0 + e1 * o1 + e2 * o2) / (e0 + e1 + e2)
    branches = (dot(ya_ref[...], wa_ref[...]), dot(yb_ref[0], wb_ref[0]), dot(yc_ref[...], wb_ref[1]),
                dot(yd.astype(BF16), wb_ref[2]))
    merged = None
    for n in range(N_BRANCH):
        gate2 = 1.0 + jnp.tanh(gate_ref[:, n * D_MODEL:(n + 1) * D_MODEL].astype(F32))
        term = gate2 * branches[n]
        merged = term if merged is None else merged + term
    h = h_ref[...]
    if first_layer:
        h = _layer_norm(h, eg_ref[...], eb_ref[...])
    u = ALPHA * h + dot(merged.astype(BF16), wo_ref[...])
    return _layer_norm(u, g_ref[...], b_ref[...])


FF_CHUNK = 1024
N_MERGE_REFS = 18


def _ffn_rows(h, p_ref, w1_ref, w2_ref, wp_ref, wg_ref, g_ref, b_ref):
    dot = functools.partial(jnp.dot, preferred_element_type=F32)
    hb = h.astype(BF16)
    ff = None
    for c in range(D_FF // FF_CHUNK):
        sl = slice(c * FF_CHUNK, (c + 1) * FF_CHUNK)
        a = jnp.maximum(dot(hb, w1_ref[:, sl]), 0.0)
        part = dot((a * a).astype(BF16), w2_ref[sl, :])
        ff = part if ff is None else ff + part
    ple = dot(p_ref[...].astype(BF16), wp_ref[...]) * (1.0 + jnp.tanh(dot(hb, wg_ref[...])))
    return _layer_norm(ALPHA * h + ff + ple, g_ref[...], b_ref[...])


def _layer_tail_kernel(*refs, first_layer):
    merge_refs, rest = refs[:N_MERGE_REFS], refs[N_MERGE_REFS:]
    ffn_refs, (h_out_ref, hb_out_ref), scr = rest[:7], rest[7:9], rest[9:]
    h1 = _merge_rows(*merge_refs, scr, first_layer)
    y = _ffn_rows(h1, *ffn_refs)
    h_out_ref[...] = y
    hb_out_ref[...] = y.astype(BF16)


def _layer_tail(z3, ya, yb, yc, od, ld, h3, embed_params, p4, merge_params, ffn_params, layer, tm=256):
    bsz, seq, _ = h3.shape

    def rows(width):
        return pl.BlockSpec((None, tm, width), lambda b, i: (b, i, 0))

    def folded(t):
        dilation = t.shape[1]
        return pl.BlockSpec((None, dilation, tm // dilation, HEAD_BLOCK), lambda b, i: (b, 0, i, 0))

    def weights(t):
        zeros = (0,) * (t.ndim - 1)
        return pl.BlockSpec((None,) + t.shape[1:], lambda b, i: (layer,) + zeros, pipeline_mode=pl.Buffered(1))

    in_specs = ([rows(GATE_COLS), rows(MLA_W), folded(yb), rows(HEAD_BLOCK)]
                + [folded(t) for t in od] + [folded(t) for t in ld] + [rows(D_MODEL)]
                + [pl.BlockSpec(t.shape, lambda b, i: (0, 0)) for t in embed_params]
                + [weights(t) for t in merge_params]
                + [pl.BlockSpec((None, None, tm, PLE_DIM), lambda b, i: (layer, b, i, 0))]
                + [weights(t) for t in ffn_params])
    assert len(in_specs) == N_MERGE_REFS + 7
    return pl.pallas_call(
        functools.partial(_layer_tail_kernel, first_layer=layer == 0),
        grid=(bsz, seq // tm),
        in_specs=in_specs,
        out_specs=[rows(D_MODEL), rows(D_MODEL)],
        out_shape=[jax.ShapeDtypeStruct((bsz, seq, D_MODEL), F32), jax.ShapeDtypeStruct((bsz, seq, D_MODEL), BF16)],
        scratch_shapes=[pltpu.VMEM((HEAD_BLOCK // LANES, tm, LANES), F32)] * 4,
        compiler_params=_cparams(("parallel", "parallel")),
        name="layer_tail",
    )(z3, ya, yb, yc, *od, *ld, h3, *embed_params, *merge_params, p4, *ffn_params)


SWA_TILE, SWA_SUBTILES = 256, 4
DIL_TILE, DIL_SUBTILES = 128, 8


def kernel(x, p, ln_emb_g, ln_emb_b, rel_bias, w_in, mla_q_norm, mla_w_uq, mla_kv_norm, mla_w_ukv, swa_sink,
           na_rpb, w_branch, w_out, ln1_g, ln1_b, w_ff1, w_ff2, w_ple, w_ple_gate, ln2_g, ln2_b):
    bsz, seq, _ = x.shape
    m = bsz * seq
    rows = seq // GRID_W
    cos_t, sin_t = _rope_tables(seq)
    swa_bias = _band_bias_tables(rel_bias, 0, min(SWA_TILE, seq), SWA_HALF, 1)
    dil_bias = []
    for g, (window, r) in enumerate(DIL_CONFIGS):
        half = window // (2 * r)
        dil_bias.append(_band_bias_tables(rel_bias, SWA_HEADS + g * DIL_HEADS, min(DIL_TILE, seq // r), half, r))

    nl = w_in.shape[0]
    w_main, w_dil = _prep_w_in(w_in)
    mla_weights = _prep_mla_weights(mla_w_uq, mla_w_ukv)
    g_q, g_kv = mla_q_norm.reshape(nl, 1, Q_LORA), mla_kv_norm.reshape(nl, 1, KV_LORA)
    swa_sinks = _sink_rows(swa_sink, swa_bias.shape[2])
    na_bias = _na_bias_tables(na_rpb, rows)
    wa = jnp.pad(w_branch[:, 0].reshape(nl, MLA_HEADS, MLA_V, D_MODEL),
                 ((0, 0), (0, 0), (0, MLA_HEAD_PAD - MLA_V), (0, 0))).reshape(nl, MLA_W, D_MODEL).astype(BF16)
    merge_params = (wa, w_branch[:, 1:].astype(BF16), (0.5 * w_out).astype(BF16),
                    ln1_g.reshape(nl, 1, D_MODEL), ln1_b.reshape(nl, 1, D_MODEL))
    ffn_params = (w_ff1.astype(BF16), w_ff2.astype(BF16), (0.5 * w_ple).astype(BF16),
                  (0.5 * w_ple_gate).astype(BF16), ln2_g.reshape(nl, 1, D_MODEL), ln2_b.reshape(nl, 1, D_MODEL))

    embed_params = (ln_emb_g.reshape(1, D_MODEL), ln_emb_b.reshape(1, D_MODEL))
    hb = _embed_layer_norm(x.reshape(m, D_MODEL), *embed_params)
    h = x
    for i in range(DEPTH):
        z3 = _matmul(hb, w_main, i, 1024, 2304, BF16, "in_proj").reshape(bsz, seq, Z_COLS)
        z4 = z3.reshape(bsz, 1, seq, Z_COLS)
        hb3 = hb.reshape(bsz, seq, D_MODEL)
        q_a, k_a, vt_a = _mla_prep(z3, cos_t, sin_t, g_q, g_kv, mla_weights, i)
        y_a = _mla_flash(q_a, k_a, vt_a)
        (y_b,) = _banded_attention(z4, B_BLK, swa_bias, swa_sinks, i,
                                   half=SWA_HALF, subtiles=SWA_SUBTILES, want_lse=False)
        y_c = _neighborhood_attention(z3, na_bias, i)
        o_d, l_d = [], []
        for g, (window, r) in enumerate(DIL_CONFIGS):
            if r == 1:
                zd, blk = z4, D1_BLK
            else:
                zd, blk = _matmul_fold(hb3, w_dil, i, g - 1, r, 1024, f"in_proj_r{r}"), 0
            o_g, l_g = _banded_attention(zd, blk, dil_bias[g], None, i, half=window // (2 * r),
                                         subtiles=DIL_SUBTILES, want_lse=True)
            o_d.append(o_g)
            l_d.append(l_g)
        h, hb = _layer_tail(z3, y_a, y_b, y_c, o_d, l_d, h, embed_params, p, merge_params, ffn_params, i)
        hb = hb.reshape(m, D_MODEL)
    return h
```

```python
import functools
import math

import jax
import jax.numpy as jnp
import numpy as np
from jax import lax
from jax.experimental import pallas as pl
from jax.experimental.pallas import tpu as pltpu

F32 = jnp.float32
BF16 = jnp.bfloat16

D_MODEL = 1024
DEPTH = 2
HEAD_DIM = 64
GRID_W = 64
NEG_INF = -1e30

MLA_HEADS = 4
MLA_NOPE = 64
MLA_ROPE = 32
MLA_V = 64
Q_LORA = 256
KV_LORA = 128
ROPE_THETA = 10000.0

SWA_HEADS = 4
SWA_KV_HEADS = 2
SWA_HALF = 128

NA_HEADS = 4
NA_KH = 8
NA_KW = 16

DIL_CONFIGS = ((128, 1), (512, 4), (2048, 16))
DIL_GROUPS = 3
DIL_HEADS = 4

T5_BUCKETS = 32
T5_MAX_DIST = 1024

N_BRANCH = 4
BRANCH_W = 256
D_FF = 4 * D_MODEL
PLE_DIM = 256

ALPHA = (2 * DEPTH) ** 0.25

LANES = 128
HEADS = 4
HEAD_BLOCK = HEADS * HEAD_DIM

GATE_COLS = N_BRANCH * D_MODEL
A_COL = GATE_COLS
A_WIDTH = 512
B_BLK = (A_COL + A_WIDTH) // HEAD_BLOCK
C_BLK = B_BLK + 3
D1_BLK = C_BLK + 3
Z_BLKS = D1_BLK + 3
Z_COLS = Z_BLKS * HEAD_BLOCK
QKV_COLS = 3 * HEAD_BLOCK
assert DIL_CONFIGS[0][1] == 1 and all(r > 1 for _, r in DIL_CONFIGS[1:])

MLA_HEAD_PAD = 128
MLA_W = MLA_HEADS * MLA_HEAD_PAD
LOG2E = math.log2(math.e)
QK_SCALE = HEAD_DIM ** -0.5 * LOG2E

VMEM_LIMIT = 56 * 1024 * 1024


def _cparams(sem):
    return pltpu.CompilerParams(dimension_semantics=sem, vmem_limit_bytes=VMEM_LIMIT)


def _layer_norm(u, g, b, eps=1e-5):
    mu = jnp.mean(u, axis=-1, keepdims=True)
    d = u - mu
    var = jnp.mean(d * d, axis=-1, keepdims=True)
    return d * lax.rsqrt(var + eps) * g + b


def _rms_norm(u, g, eps=1e-6):
    return u * lax.rsqrt(jnp.mean(u * u, axis=-1, keepdims=True) + eps) * g


def _ln_kernel(x_ref, g_ref, b_ref, hb_ref):
    hb_ref[...] = _layer_norm(x_ref[...], g_ref[...], b_ref[...]).astype(BF16)


def _embed_layer_norm(x2, g, b, tm=512):
    m, d = x2.shape
    row = pl.BlockSpec((tm, d), lambda i: (i, 0))
    vec = pl.BlockSpec((1, d), lambda i: (0, 0))
    return pl.pallas_call(
        _ln_kernel,
        grid=(m // tm,),
        in_specs=[row, vec, vec],
        out_specs=row,
        out_shape=jax.ShapeDtypeStruct((m, d), BF16),
        compiler_params=_cparams(("parallel",)),
        name="embed_ln",
    )(x2, g, b)


def _mm_kernel(a_ref, w_ref, o_ref):
    o_ref[...] = jnp.dot(a_ref[...], w_ref[...], preferred_element_type=F32).astype(o_ref.dtype)


def _layer_block(arr, layer):
    zeros = (0,) * (arr.ndim - 1)
    return pl.BlockSpec((None,) + arr.shape[1:], lambda *_: (layer,) + zeros)


def _matmul(a, w, layer, tm, tn, out_dtype, name):
    m, k = a.shape
    n = w.shape[2]
    return pl.pallas_call(
        _mm_kernel,
        grid=(m // tm, n // tn),
        in_specs=[pl.BlockSpec((tm, k), lambda i, j: (i, 0)),
                  pl.BlockSpec((None, k, tn), lambda i, j: (layer, 0, j))],
        out_specs=pl.BlockSpec((tm, tn), lambda i, j: (i, j)),
        out_shape=jax.ShapeDtypeStruct((m, n), out_dtype),
        compiler_params=_cparams(("parallel", "arbitrary")),
        name=name,
    )(a, w)


def _prep_w_in(w):
    nl = w.shape[0]
    o = 0
    a = w[..., o:o + Q_LORA + KV_LORA + MLA_ROPE]
    o += Q_LORA + KV_LORA + MLA_ROPE
    bq = w[..., o:o + HEAD_BLOCK]
    o += HEAD_BLOCK
    kvw = SWA_KV_HEADS * HEAD_DIM
    bk = w[..., o:o + kvw]
    o += kvw
    bv = w[..., o:o + kvw]
    o += kvw
    c = w[..., o:o + 3 * HEAD_BLOCK]
    o += 3 * HEAD_BLOCK
    d = w[..., o:o + 3 * DIL_GROUPS * HEAD_BLOCK]
    o += 3 * DIL_GROUPS * HEAD_BLOCK
    gates = w[..., o:]
    rep = SWA_HEADS // SWA_KV_HEADS

    def expand(t):
        t = t.reshape(nl, D_MODEL, SWA_KV_HEADS, 1, HEAD_DIM)
        return jnp.broadcast_to(t, (nl, D_MODEL, SWA_KV_HEADS, rep, HEAD_DIM)).reshape(nl, D_MODEL, HEAD_BLOCK)

    a = jnp.pad(a, ((0, 0), (0, 0), (0, A_WIDTH - a.shape[-1])))
    qkv_scale = jnp.concatenate([jnp.full((HEAD_BLOCK,), QK_SCALE, F32), jnp.ones((2 * HEAD_BLOCK,), F32)])
    c = c * qkv_scale
    d = d.reshape(nl, D_MODEL, 3, DIL_GROUPS, HEAD_BLOCK).transpose(0, 3, 1, 2, 4)
    d = d.reshape(nl, DIL_GROUPS, D_MODEL, QKV_COLS) * qkv_scale
    main = jnp.concatenate([0.5 * gates, a, QK_SCALE * bq, expand(bk), expand(bv), c, d[:, 0]], axis=-1)
    return main.astype(BF16), d[:, 1:].astype(BF16)


FOLD_STRIDE = 4


def _mm_fold_kernel(a_ref, w_ref, o_ref, acc_ref, tmp_ref, *, dilation):
    acc = jnp.dot(a_ref[...], w_ref[...], preferred_element_type=F32)
    nblk, tm, _ = acc_ref.shape
    for c in range(nblk):
        acc_ref[c] = acc[:, c * LANES:(c + 1) * LANES]
    if dilation <= FOLD_STRIDE:
        n = tm // dilation
        for j in range(dilation):
            for c in range(nblk):
                o_ref[j, :, c * LANES:(c + 1) * LANES] = (
                    acc_ref[c, pl.ds(j, n, stride=dilation), :].astype(o_ref.dtype))
    else:
        outer = dilation // FOLD_STRIDE
        n1, n = tm // FOLD_STRIDE, tm // dilation
        for c in range(nblk):
            for j1 in range(FOLD_STRIDE):
                tmp_ref[c, j1 * n1:(j1 + 1) * n1, :] = acc_ref[c, pl.ds(j1, n1, stride=FOLD_STRIDE), :]
        for j2 in range(outer):
            for j1 in range(FOLD_STRIDE):
                for c in range(nblk):
                    o_ref[FOLD_STRIDE * j2 + j1, :, c * LANES:(c + 1) * LANES] = (
                        tmp_ref[c, pl.ds(j1 * n1 + j2, n, stride=outer), :].astype(o_ref.dtype))


def _matmul_fold(a3, w, layer, group, dilation, tm, name):
    bsz, seq, k = a3.shape
    n = w.shape[-1]
    assert dilation <= FOLD_STRIDE or (dilation % FOLD_STRIDE == 0 and dilation // FOLD_STRIDE <= FOLD_STRIDE)
    slab = pltpu.VMEM((n // LANES, tm, LANES), F32)
    return pl.pallas_call(
        functools.partial(_mm_fold_kernel, dilation=dilation),
        grid=(bsz, seq // tm),
        in_specs=[pl.BlockSpec((None, tm, k), lambda b, i: (b, i, 0)),
                  pl.BlockSpec((None, None, k, n), lambda b, i: (layer, group, 0, 0))],
        out_specs=pl.BlockSpec((None, dilation, tm // dilation, n), lambda b, i: (b, 0, i, 0)),
        out_shape=jax.ShapeDtypeStruct((bsz, dilation, seq // dilation, n), BF16),
        scratch_shapes=[slab, slab],
        compiler_params=_cparams(("parallel", "parallel")),
        name=name,
    )(a3, w)


def _rope_tables(seq):
    half = MLA_ROPE // 2
    inv = ROPE_THETA ** (-jnp.arange(half, dtype=F32) / half)
    ang = jnp.arange(seq).astype(F32)[:, None] * inv[None, :]
    cos, sin = jnp.cos(ang), jnp.sin(ang)
    pad = MLA_HEAD_PAD - MLA_NOPE - MLA_ROPE
    cos_t = jnp.concatenate([jnp.ones((seq, MLA_NOPE), F32), cos, cos, jnp.zeros((seq, pad), F32)], axis=1)
    sin_t = jnp.concatenate([jnp.zeros((seq, MLA_NOPE), F32), sin, sin, jnp.zeros((seq, pad), F32)], axis=1)
    return cos_t, sin_t


def _rot_cols(w_rope):
    half = MLA_ROPE // 2
    return jnp.concatenate([-w_rope[..., half:], w_rope[..., :half]], axis=-1)


def _prep_mla_weights(w_uq, w_ukv):
    nl = w_uq.shape[0]
    pad = MLA_HEAD_PAD - MLA_NOPE - MLA_ROPE
    lead = ((0, 0), (0, 0), (0, 0))
    wq = w_uq.reshape(nl, Q_LORA, MLA_HEADS, MLA_NOPE + MLA_ROPE)
    wq_main = jnp.pad(wq, lead + ((0, pad),)).reshape(nl, Q_LORA, MLA_W)
    wq_rot = jnp.concatenate([jnp.zeros((nl, Q_LORA, MLA_HEADS, MLA_NOPE), F32), _rot_cols(wq[..., MLA_NOPE:]),
                              jnp.zeros((nl, Q_LORA, MLA_HEADS, pad), F32)], axis=-1).reshape(nl, Q_LORA, MLA_W)
    wkv = w_ukv.reshape(nl, KV_LORA, MLA_HEADS, MLA_NOPE + MLA_V)
    wk_n = jnp.pad(wkv[..., :MLA_NOPE], lead + ((0, MLA_HEAD_PAD - MLA_NOPE),)).reshape(nl, KV_LORA, MLA_W)
    wv = jnp.pad(wkv[..., MLA_NOPE:], lead + ((0, MLA_HEAD_PAD - MLA_V),)).reshape(nl, KV_LORA, MLA_W)
    eye = jnp.eye(MLA_ROPE, dtype=F32)
    place = jnp.concatenate([jnp.zeros((MLA_ROPE, MLA_NOPE), F32), eye, jnp.zeros((MLA_ROPE, pad), F32)], axis=1)
    place_rot = jnp.concatenate([jnp.zeros((MLA_ROPE, MLA_NOPE), F32), _rot_cols(eye),
                                 jnp.zeros((MLA_ROPE, pad), F32)], axis=1)

    def lift(pm):
        pm = jnp.pad(pm, ((0, LANES - MLA_ROPE), (0, 0)))
        return jnp.tile(pm, (1, MLA_HEADS))

    def per_layer(t):
        return jnp.broadcast_to(t, (nl,) + t.shape)

    wk_main = jnp.concatenate([wk_n, per_layer(lift(place))], axis=1)
    wk_rot = per_layer(jnp.concatenate([jnp.zeros((KV_LORA, MLA_W), F32), lift(place_rot)], axis=0))
    return tuple(t.astype(BF16) for t in (wq_main, wq_rot, wk_main, wk_rot, wv))


def _mla_prep_kernel(za_ref, cos_ref, sin_ref, gq_ref, gkv_ref, wq_ref, wqr_ref, wk_ref, wkr_ref, wv_ref,
                     q_ref, k_ref, vt_ref):
    za = za_ref[...]
    cq = _rms_norm(za[:, :Q_LORA].astype(F32), gq_ref[...]).astype(BF16)
    ckv = _rms_norm(za[:, Q_LORA:Q_LORA + KV_LORA].astype(F32), gkv_ref[...]).astype(BF16)
    cos = jnp.concatenate([cos_ref[...]] * MLA_HEADS, axis=1)
    sin = jnp.concatenate([sin_ref[...]] * MLA_HEADS, axis=1)
    dot = functools.partial(jnp.dot, preferred_element_type=F32)
    q = dot(cq, wq_ref[...]) * cos + dot(cq, wqr_ref[...]) * sin
    q_ref[...] = (q * ((MLA_NOPE + MLA_ROPE) ** -0.5 * LOG2E)).astype(BF16)
    lhs = jnp.concatenate([ckv, za[:, Q_LORA + KV_LORA:]], axis=1)
    k_ref[...] = (dot(lhs, wk_ref[...]) * cos + dot(lhs, wkr_ref[...]) * sin).astype(BF16)
    lane = lax.broadcasted_iota(jnp.int32, (1, MLA_W), 1)
    ones_col = jnp.where(lane % MLA_HEAD_PAD == MLA_V, 1.0, 0.0)
    vt_ref[...] = (dot(ckv, wv_ref[...]) + ones_col).T.astype(BF16)


def _mla_prep(z3, cos_t, sin_t, g_q, g_kv, weights, layer, tm=1024):
    bsz, seq, _ = z3.shape
    tm = min(tm, seq)
    params = (g_q, g_kv) + tuple(weights)
    tab = pl.BlockSpec((tm, MLA_HEAD_PAD), lambda b, i: (i, 0))
    out = pl.BlockSpec((None, tm, MLA_W), lambda b, i: (b, i, 0))
    shp = jax.ShapeDtypeStruct((bsz, seq, MLA_W), BF16)
    return pl.pallas_call(
        _mla_prep_kernel,
        grid=(bsz, seq // tm),
        in_specs=[pl.BlockSpec((None, tm, A_WIDTH), lambda b, i: (b, i, A_COL // A_WIDTH)), tab, tab]
        + [_layer_block(t, layer) for t in params],
        out_specs=[out, out, pl.BlockSpec((None, MLA_W, tm), lambda b, i: (b, 0, i))],
        out_shape=[shp, shp, jax.ShapeDtypeStruct((bsz, MLA_W, seq), BF16)],
        compiler_params=_cparams(("parallel", "parallel")),
        name="mla_prep",
    )(z3, cos_t, sin_t, *params)


def _mla_flash_kernel(q_ref, k_ref, vt_ref, o_ref, m_scr, acc_scr, s_scr, p_scr):
    ki = pl.program_id(2)

    @pl.when(ki == 0)
    def _():
        m_scr[...] = jnp.full(m_scr.shape, -jnp.inf, F32)
        acc_scr[...] = jnp.zeros(acc_scr.shape, F32)

    heads = [slice(h * MLA_HEAD_PAD, (h + 1) * MLA_HEAD_PAD) for h in range(MLA_HEADS)]
    for h, sl in enumerate(heads):
        s_scr[h] = lax.dot_general(k_ref[:, sl], q_ref[:, sl], (((1,), (1,)), ((), ())),
                                   preferred_element_type=F32)
    alphas = []
    for h in range(MLA_HEADS):
        s = s_scr[h]
        m_prev = m_scr[h]
        m_new = jnp.maximum(m_prev, jnp.max(s, axis=0, keepdims=True))
        p_scr[h] = jnp.exp2(s - m_new).astype(BF16)
        alphas.append(jnp.exp2(m_prev - m_new))
        m_scr[h] = m_new
    for h, sl in enumerate(heads):
        acc_scr[h] = alphas[h] * acc_scr[h] + jnp.dot(vt_ref[sl, :], p_scr[h], preferred_element_type=F32)

    @pl.when(ki == pl.num_programs(2) - 1)
    def _():
        for h, sl in enumerate(heads):
            acc = acc_scr[h]
            o_ref[:, sl] = (acc / acc[MLA_V:MLA_V + 1, :]).T.astype(o_ref.dtype)


def _mla_flash(q, k, vt, tq=512, tk=2048):
    bsz, seq, _ = q.shape
    tq, tk = min(tq, seq), min(tk, seq)
    return pl.pallas_call(
        _mla_flash_kernel,
        grid=(bsz, seq // tq, seq // tk),
        in_specs=[pl.BlockSpec((None, tq, MLA_W), lambda b, i, j: (b, i, 0)),
                  pl.BlockSpec((None, tk, MLA_W), lambda b, i, j: (b, j, 0)),
                  pl.BlockSpec((None, MLA_W, tk), lambda b, i, j: (b, 0, j))],
        out_specs=pl.BlockSpec((None, tq, MLA_W), lambda b, i, j: (b, i, 0)),
        out_shape=jax.ShapeDtypeStruct((bsz, seq, MLA_W), BF16),
        scratch_shapes=[pltpu.VMEM((MLA_HEADS, 1, tq), F32),
                        pltpu.VMEM((MLA_HEADS, MLA_HEAD_PAD, tq), F32),
                        pltpu.VMEM((MLA_HEADS, tk, tq), F32),
                        pltpu.VMEM((MLA_HEADS, tk, tq), BF16)],
        compiler_params=_cparams(("parallel", "parallel", "arbitrary")),
        name="mla_flash",
    )(q, k, vt)


def _t5_bucket(rel):
    nb = T5_BUCKETS // 2
    max_exact = nb // 2
    n = jnp.abs(rel)
    large = max_exact + (jnp.log(jnp.maximum(n, 1).astype(F32) / max_exact)
                         / math.log(T5_MAX_DIST / max_exact) * (nb - max_exact)).astype(jnp.int32)
    large = jnp.minimum(large, nb - 1)
    return jnp.where(rel > 0, nb, 0) + jnp.where(n < max_exact, n, large)


def _band_bias_tables(rel_bias, head_lo, tile, half, dilation):
    width = tile + 2 * half
    col = jnp.arange(width)[None, :]
    rel = col - half - jnp.arange(tile)[:, None]
    onehot = jax.nn.one_hot(_t5_bucket(rel * dilation), T5_BUCKETS, dtype=F32)
    bias = jnp.einsum("qkn,nh->hqk", onehot, rel_bias[:, head_lo:head_lo + HEADS].astype(F32),
                      precision=lax.Precision.HIGHEST)
    band = jnp.abs(rel) <= half
    left_ok = col >= half
    right_ok = col < tile + half
    masks = jnp.stack([band, band & left_ok, band & right_ok, band & left_ok & right_ok])
    return jnp.where(masks[:, None], LOG2E * bias[None].astype(F32), NEG_INF)


def _head_of_lane():
    return lax.broadcasted_iota(jnp.int32, (1, HEAD_BLOCK), 1) // HEAD_DIM


def _per_head_lanes(cols):
    head = _head_of_lane()
    out = cols[HEADS - 1]
    for h in range(HEADS - 2, -1, -1):
        out = jnp.where(head == h, cols[h], out)
    return out


def _packed_heads_attention(q, kwin, vwin, bias, sink_rows):
    t = q.shape[0]
    head = _head_of_lane()
    qs = jnp.concatenate([jnp.where(head == h, q, 0) for h in range(HEADS)], axis=0)
    s = lax.dot_general(qs, kwin, (((1,), (1,)), ((), ())), preferred_element_type=F32)
    s = s + bias.reshape(HEADS * t, bias.shape[-1])
    if sink_rows is not None:
        m = jnp.max(jnp.concatenate([jnp.maximum(s[:, :LANES], sink_rows), s[:, LANES:]], axis=1),
                    axis=1, keepdims=True)
    else:
        m = jnp.max(s, axis=1, keepdims=True)
    e = jnp.exp2(s - m)
    if sink_rows is not None:
        lane = lax.broadcasted_iota(jnp.int32, (1, LANES), 1)
        e_sink = jnp.where(lane == 0, jnp.exp2(sink_rows - m), 0.0)
        den = jnp.sum(jnp.concatenate([e[:, :LANES] + e_sink, e[:, LANES:]], axis=1), axis=1, keepdims=True)
    else:
        den = jnp.sum(e, axis=1, keepdims=True)
    o = jnp.dot(e.astype(BF16), vwin, preferred_element_type=F32) * (1.0 / den)
    out = o[(HEADS - 1) * t:]
    for h in range(HEADS - 2, -1, -1):
        out = jnp.where(head == h, o[h * t:(h + 1) * t], out)
    return out, m, den


def _banded_kernel(*refs, tile, subtiles, half, length, use_sink, want_lse):
    refs = list(refs)
    sink_ref = refs.pop(0) if use_sink else None
    q_ref, k_ref, v_ref, bias_ref, o_ref = refs[:5]
    lse_ref = refs[5] if want_lse else None
    i = pl.program_id(2)
    nt = length // tile
    width = tile + 2 * half
    head = _head_of_lane()
    for u in range(subtiles):
        t = i * subtiles + u
        main = t * tile
        can_first, can_last = u == 0, u == subtiles - 1
        if can_first or can_last:
            left = jnp.maximum(main - half, 0) if can_first else main - half
            right = jnp.minimum(main + tile, length - half) if can_last else main + tile
            left, right = pl.multiple_of(left, half), pl.multiple_of(right, half)
            main = pl.multiple_of(main, tile)

            def window(ref):
                return jnp.concatenate([ref[pl.ds(left, half), :], ref[pl.ds(main, tile), :],
                                        ref[pl.ds(right, half), :]], axis=0)

            variant = 0
            if can_first:
                variant = variant + (t == 0).astype(jnp.int32)
            if can_last:
                variant = variant + 2 * (t == nt - 1).astype(jnp.int32)
        else:
            start = pl.multiple_of(main - half, half)

            def window(ref):
                return ref[pl.ds(start, width), :]

            variant = 0
        rows = slice(u * tile, (u + 1) * tile)
        kwin, vwin = window(k_ref), window(v_ref)
        q = q_ref[rows, :]
        sink_rows = sink_ref[...] if use_sink else None
        out, m, den = _packed_heads_attention(q, kwin, vwin, bias_ref[variant], sink_rows)
        o_ref[rows, :] = out.astype(o_ref.dtype)
        if want_lse:
            lse = m + jnp.log2(den)
            lse_ref[rows, :] = _per_head_lanes([lse[h * tile:(h + 1) * tile] for h in range(HEADS)])


def _sink_rows(sink, tile):
    rows = jnp.repeat(LOG2E * sink.astype(F32), tile, axis=1)
    return jnp.broadcast_to(rows[:, :, None], rows.shape + (LANES,))


def _banded_attention(zf, blk, bias, sink_rows, layer, *, half, subtiles, want_lse):
    bsz, dilation, length, _ = zf.shape
    tile = bias.shape[2]
    subtiles = min(subtiles, length // tile)
    step = tile * subtiles
    use_sink = sink_rows is not None
    in_specs = [pl.BlockSpec((None, None, step, HEAD_BLOCK), lambda b, j, i: (b, j, i, blk)),
                pl.BlockSpec((None, None, length, HEAD_BLOCK), lambda b, j, i: (b, j, 0, blk + 1)),
                pl.BlockSpec((None, None, length, HEAD_BLOCK), lambda b, j, i: (b, j, 0, blk + 2)),
                pl.BlockSpec(bias.shape, lambda b, j, i: (0, 0, 0, 0))]
    args = [zf, zf, zf, bias]
    if use_sink:
        in_specs.insert(0, _layer_block(sink_rows, layer))
        args.insert(0, sink_rows)
    o_spec = pl.BlockSpec((None, None, step, HEAD_BLOCK), lambda b, j, i: (b, j, i, 0))
    out_specs = [o_spec]
    out_shape = [jax.ShapeDtypeStruct((bsz, dilation, length, HEAD_BLOCK), BF16)]
    if want_lse:
        out_specs.append(o_spec)
        out_shape.append(jax.ShapeDtypeStruct((bsz, dilation, length, HEAD_BLOCK), F32))
    return pl.pallas_call(
        functools.partial(_banded_kernel, tile=tile, subtiles=subtiles, half=half, length=length,
                          use_sink=use_sink, want_lse=want_lse),
        grid=(bsz, dilation, length // step),
        in_specs=in_specs,
        out_specs=out_specs,
        out_shape=out_shape,
        compiler_params=_cparams(("parallel", "parallel", "arbitrary")),
        name=f"banded_r{dilation}",
    )(*args)


NA_TILE_ROWS = 4
NA_WIN_ROWS = NA_TILE_ROWS + NA_KH
NA_SUBTILES = 4


def _na_bias_tables(rpb, rows):
    nl = rpb.shape[0]
    tr, wr = NA_TILE_ROWS, NA_WIN_ROWS
    exact = functools.partial(jnp.einsum, precision=lax.Precision.HIGHEST)
    c = np.arange(GRID_W)
    cs = np.clip(c - NA_KW // 2, 0, GRID_W - NA_KW)
    col_ok = (c[None, :] >= cs[:, None]) & (c[None, :] < cs[:, None] + NA_KW)
    dc = np.clip(c[None, :] - c[:, None], -(NA_KW - 1), NA_KW - 1) + (NA_KW - 1)
    by_col = exact("lhdn,qkn->lhdqk", rpb.astype(F32), np.eye(2 * NA_KW - 1, dtype=np.float32)[dc])
    tabs = []
    for r0, ws in ((0, 0), (tr, 0), (rows - tr, rows - wr)):
        r = r0 + np.arange(tr)
        start = np.clip(r - NA_KH // 2, 0, rows - NA_KH)
        kr = ws + np.arange(wr)
        row_ok = (kr[None, :] >= start[:, None]) & (kr[None, :] < start[:, None] + NA_KH)
        dr = np.clip(kr[None, :] - r[:, None] + (NA_KH - 1), 0, 2 * NA_KH - 2)
        bias = exact("rad,lhdqk->lhrqak", np.eye(2 * NA_KH - 1, dtype=np.float32)[dr], by_col)
        ok = row_ok[:, None, :, None] & col_ok[None, :, None, :]
        tabs.append(jnp.where(ok, LOG2E * bias, NEG_INF).reshape(nl, HEADS, tr * GRID_W, wr * GRID_W))
    return jnp.stack(tabs, axis=1)


def _na_kernel(q_ref, k_ref, v_ref, bias_ref, o_ref, *, rows):
    i = pl.program_id(1)
    nt = rows // NA_TILE_ROWS
    win = NA_WIN_ROWS * GRID_W
    tq = NA_TILE_ROWS * GRID_W
    head = _head_of_lane()
    for u in range(NA_SUBTILES):
        t = i * NA_SUBTILES + u
        ws = jnp.clip(t * NA_TILE_ROWS - NA_KH // 2, 0, rows - NA_WIN_ROWS) * GRID_W
        ws = pl.multiple_of(ws, GRID_W)
        variant = jnp.where(t == 0, 0, jnp.where(t == nt - 1, 2, 1))
        kwin = k_ref[pl.ds(ws, win), :]
        vwin = v_ref[pl.ds(ws, win), :]
        q = q_ref[u * tq:(u + 1) * tq, :]
        acc = None
        inv_den = []
        for h in range(HEADS):
            mine = head == h
            s = lax.dot_general(q, jnp.where(mine, kwin, 0), (((1,), (1,)), ((), ())),
                                preferred_element_type=F32)
            s = s + bias_ref[variant, h]
            m = jnp.max(s, axis=1, keepdims=True)
            e = jnp.exp2(s - m)
            inv_den.append(1.0 / jnp.sum(e, axis=1, keepdims=True))
            part = jnp.dot(e.astype(BF16), jnp.where(mine, vwin, 0), preferred_element_type=F32)
            acc = part if acc is None else acc + part
        o_ref[u * tq:(u + 1) * tq, :] = (acc * _per_head_lanes(inv_den)).astype(o_ref.dtype)


def _neighborhood_attention(z3, bias, layer):
    bsz, seq, _ = z3.shape
    rows = seq // GRID_W
    tq = NA_TILE_ROWS * GRID_W * NA_SUBTILES
    return pl.pallas_call(
        functools.partial(_na_kernel, rows=rows),
        grid=(bsz, rows // (NA_TILE_ROWS * NA_SUBTILES)),
        in_specs=[pl.BlockSpec((None, tq, HEAD_BLOCK), lambda b, i: (b, i, C_BLK)),
                  pl.BlockSpec((None, seq, HEAD_BLOCK), lambda b, i: (b, 0, C_BLK + 1)),
                  pl.BlockSpec((None, seq, HEAD_BLOCK), lambda b, i: (b, 0, C_BLK + 2)),
                  _layer_block(bias, layer)],
        out_specs=pl.BlockSpec((None, tq, HEAD_BLOCK), lambda b, i: (b, i, 0)),
        out_shape=jax.ShapeDtypeStruct((bsz, seq, HEAD_BLOCK), BF16),
        compiler_params=_cparams(("parallel", "arbitrary")),
        name="neighborhood",
    )(z3, z3, z3, bias)


def _unfold(ref, scr):
    dilation, n, cols = ref.shape
    if dilation == 1:
        return ref[0].astype(F32)
    nblk = cols // LANES
    for j in range(dilation):
        for c in range(nblk):
            scr[c, pl.ds(j, n, stride=dilation), :] = ref[j, :, c * LANES:(c + 1) * LANES].astype(F32)
    return jnp.concatenate([scr[c] for c in range(nblk)], axis=1)


def _merge_rows(gate_ref, ya_ref, yb_ref, yc_ref, o0_ref, o1_ref, o2_ref, l0_ref, l1_ref, l2_ref, h_ref,
                eg_ref, eb_ref, wa_ref, wb_ref, wo_ref, g_ref, b_ref, scr, first_layer):
    dot = functools.partial(jnp.dot, preferred_element_type=F32)
    l0, l1, l2 = _unfold(l0_ref, None), _unfold(l1_ref, scr[0]), _unfold(l2_ref, scr[1])
    o0, o1, o2 = _unfold(o0_ref, None), _unfold(o1_ref, scr[2]), _unfold(o2_ref, scr[3])
    lm = jnp.maximum(jnp.maximum(l0, l1), l2)
    e0, e1, e2 = jnp.exp2(l0 - lm), jnp.exp2(l1 - lm), jnp.exp2(l2 - lm)
    yd = (e0 * o0 + e1 * o1 + e2 * o2) / (e0 + e1 + e2)
    branches = (dot(ya_ref[...], wa_ref[...]), dot(yb_ref[0], wb_ref[0]), dot(yc_ref[...], wb_ref[1]),
                dot(yd.astype(BF16), wb_ref[2]))
    merged = None
    for n in range(N_BRANCH):
        gate2 = 1.0 + jnp.tanh(gate_ref[:, n * D_MODEL:(n + 1) * D_MODEL].astype(F32))
        term = gate2 * branches[n]
        merged = term if merged is None else merged + term
    h = h_ref[...]
    if first_layer:
        h = _layer_norm(h, eg_ref[...], eb_ref[...])
    u = ALPHA * h + dot(merged.astype(BF16), wo_ref[...])
    return _layer_norm(u, g_ref[...], b_ref[...])


FF_CHUNK = 1024
N_MERGE_REFS = 18
TAIL_SUBTILES = 2


def _ffn_rows(h, p_ref, w1_ref, w2_ref, wp_ref, wg_ref, g_ref, b_ref):
    dot = functools.partial(jnp.dot, preferred_element_type=F32)
    hb = h.astype(BF16)
    ff = None
    for c in range(D_FF // FF_CHUNK):
        sl = slice(c * FF_CHUNK, (c + 1) * FF_CHUNK)
        a = jnp.maximum(dot(hb, w1_ref[:, sl]), 0.0)
        part = dot((a * a).astype(BF16), w2_ref[sl, :])
        ff = part if ff is None else ff + part
    ple = dot(p_ref[...].astype(BF16), wp_ref[...]) * (1.0 + jnp.tanh(dot(hb, wg_ref[...])))
    return _layer_norm(ALPHA * h + ff + ple, g_ref[...], b_ref[...])


def _layer_tail_kernel(*refs, first_layer):
    merge_refs, rest = refs[:N_MERGE_REFS], refs[N_MERGE_REFS:]
    ffn_refs, (h_out_ref, hb_out_ref), scr = rest[:7], rest[7:9], rest[9:]
    tm = h_out_ref.shape[0] // TAIL_SUBTILES

    def rows(ref, u):
        return ref.at[pl.ds(u * tm, tm), :]

    def fold(ref, u):
        n = ref.shape[1] // TAIL_SUBTILES
        return ref.at[:, pl.ds(u * n, n), :]

    h1 = []
    for u in range(TAIL_SUBTILES):
        gate, ya, yb, yc, o0, o1, o2, l0, l1, l2, h = merge_refs[:11]
        views = (rows(gate, u), rows(ya, u), fold(yb, u), rows(yc, u), fold(o0, u), fold(o1, u), fold(o2, u),
                 fold(l0, u), fold(l1, u), fold(l2, u), rows(h, u))
        sub_scr = tuple(s.at[:, pl.ds(u * tm, tm), :] for s in scr)
        h1.append(_merge_rows(*views, *merge_refs[11:], sub_scr, first_layer))
    for u in range(TAIL_SUBTILES):
        y = _ffn_rows(h1[u], rows(ffn_refs[0], u), *ffn_refs[1:])
        h_out_ref[pl.ds(u * tm, tm), :] = y
        hb_out_ref[pl.ds(u * tm, tm), :] = y.astype(BF16)


def _layer_tail(z3, ya, yb, yc, od, ld, h3, embed_params, p4, merge_params, ffn_params, layer, tm=512):
    bsz, seq, _ = h3.shape

    def rows(width):
        return pl.BlockSpec((None, tm, width), lambda b, i: (b, i, 0))

    def folded(t):
        dilation = t.shape[1]
        return pl.BlockSpec((None, dilation, tm // dilation, HEAD_BLOCK), lambda b, i: (b, 0, i, 0))

    def weights(t):
        zeros = (0,) * (t.ndim - 1)
        return pl.BlockSpec((None,) + t.shape[1:], lambda b, i: (layer,) + zeros, pipeline_mode=pl.Buffered(1))

    in_specs = ([rows(GATE_COLS), rows(MLA_W), folded(yb), rows(HEAD_BLOCK)]
                + [folded(t) for t in od] + [folded(t) for t in ld] + [rows(D_MODEL)]
                + [pl.BlockSpec(t.shape, lambda b, i: (0, 0)) for t in embed_params]
                + [weights(t) for t in merge_params]
                + [pl.BlockSpec((None, None, tm, PLE_DIM), lambda b, i: (layer, b, i, 0))]
                + [weights(t) for t in ffn_params])
    assert len(in_specs) == N_MERGE_REFS + 7
    return pl.pallas_call(
        functools.partial(_layer_tail_kernel, first_layer=layer == 0),
        grid=(bsz, seq // tm),
        in_specs=in_specs,
        out_specs=[rows(D_MODEL), rows(D_MODEL)],
        out_shape=[jax.ShapeDtypeStruct((bsz, seq, D_MODEL), F32), jax.ShapeDtypeStruct((bsz, seq, D_MODEL), BF16)],
        scratch_shapes=[pltpu.VMEM((HEAD_BLOCK // LANES, tm, LANES), F32)] * 4,
        compiler_params=_cparams(("parallel", "parallel")),
        name="layer_tail",
    )(z3, ya, yb, yc, *od, *ld, h3, *embed_params, *merge_params, p4, *ffn_params)


SWA_TILE, SWA_SUBTILES = 256, 4
DIL_TILE, DIL_SUBTILES = 128, 8


def kernel(x, p, ln_emb_g, ln_emb_b, rel_bias, w_in, mla_q_norm, mla_w_uq, mla_kv_norm, mla_w_ukv, swa_sink,
           na_rpb, w_branch, w_out, ln1_g, ln1_b, w_ff1, w_ff2, w_ple, w_ple_gate, ln2_g, ln2_b):
    bsz, seq, _ = x.shape
    m = bsz * seq
    rows = seq // GRID_W
    cos_t, sin_t = _rope_tables(seq)
    swa_bias = _band_bias_tables(rel_bias, 0, min(SWA_TILE, seq), SWA_HALF, 1)
    dil_bias = []
    for g, (window, r) in enumerate(DIL_CONFIGS):
        half = window // (2 * r)
        dil_bias.append(_band_bias_tables(rel_bias, SWA_HEADS + g * DIL_HEADS, min(DIL_TILE, seq // r), half, r))

    nl = w_in.shape[0]
    w_main, w_dil = _prep_w_in(w_in)
    mla_weights = _prep_mla_weights(mla_w_uq, mla_w_ukv)
    g_q, g_kv = mla_q_norm.reshape(nl, 1, Q_LORA), mla_kv_norm.reshape(nl, 1, KV_LORA)
    swa_sinks = _sink_rows(swa_sink, swa_bias.shape[2])
    na_bias = _na_bias_tables(na_rpb, rows)
    wa = jnp.pad(w_branch[:, 0].reshape(nl, MLA_HEADS, MLA_V, D_MODEL),
                 ((0, 0), (0, 0), (0, MLA_HEAD_PAD - MLA_V), (0, 0))).reshape(nl, MLA_W, D_MODEL).astype(BF16)
    merge_params = (wa, w_branch[:, 1:].astype(BF16), (0.5 * w_out).astype(BF16),
                    ln1_g.reshape(nl, 1, D_MODEL), ln1_b.reshape(nl, 1, D_MODEL))
    ffn_params = (w_ff1.astype(BF16), w_ff2.astype(BF16), (0.5 * w_ple).astype(BF16),
                  (0.5 * w_ple_gate).astype(BF16), ln2_g.reshape(nl, 1, D_MODEL), ln2_b.reshape(nl, 1, D_MODEL))

    embed_params = (ln_emb_g.reshape(1, D_MODEL), ln_emb_b.reshape(1, D_MODEL))
    hb = _embed_layer_norm(x.reshape(m, D_MODEL), *embed_params)
    h = x
    for i in range(DEPTH):
        z3 = _matmul(hb, w_main, i, 1024, 2304, BF16, "in_proj").reshape(bsz, seq, Z_COLS)
        z4 = z3.reshape(bsz, 1, seq, Z_COLS)
        hb3 = hb.reshape(bsz, seq, D_MODEL)
        q_a, k_a, vt_a = _mla_prep(z3, cos_t, sin_t, g_q, g_kv, mla_weights, i)
        y_a = _mla_flash(q_a, k_a, vt_a)
        (y_b,) = _banded_attention(z4, B_BLK, swa_bias, swa_sinks, i,
                                   half=SWA_HALF, subtiles=SWA_SUBTILES, want_lse=False)
        y_c = _neighborhood_attention(z3, na_bias, i)
        o_d, l_d = [], []
        for g, (window, r) in enumerate(DIL_CONFIGS):
            if r == 1:
                zd, blk = z4, D1_BLK
            else:
                zd, blk = _matmul_fold(hb3, w_dil, i, g - 1, r, 1024, f"in_proj_r{r}"), 0
            o_g, l_g = _banded_attention(zd, blk, dil_bias[g], None, i, half=window // (2 * r),
                                         subtiles=DIL_SUBTILES, want_lse=True)
            o_d.append(o_g)
            l_d.append(l_g)
        h, hb = _layer_tail(z3, y_a, y_b, y_c, o_d, l_d, h, embed_params, p, merge_params, ffn_params, i)
        hb = hb.reshape(m, D_MODEL)
    return h
```

```python
import functools
import math

import jax
import jax.numpy as jnp
import numpy as np
from jax import lax
from jax.experimental import pallas as pl
from jax.experimental.pallas import tpu as pltpu

F32 = jnp.float32
BF16 = jnp.bfloat16

D_MODEL = 1024
DEPTH = 2
HEAD_DIM = 64
GRID_W = 64
NEG_INF = -1e30

MLA_HEADS = 4
MLA_NOPE = 64
MLA_ROPE = 32
MLA_V = 64
Q_LORA = 256
KV_LORA = 128
ROPE_THETA = 10000.0

SWA_HEADS = 4
SWA_KV_HEADS = 2
SWA_HALF = 128

NA_HEADS = 4
NA_KH = 8
NA_KW = 16

DIL_CONFIGS = ((128, 1), (512, 4), (2048, 16))
DIL_GROUPS = 3
DIL_HEADS = 4

T5_BUCKETS = 32
T5_MAX_DIST = 1024

N_BRANCH = 4
BRANCH_W = 256
D_FF = 4 * D_MODEL
PLE_DIM = 256

ALPHA = (2 * DEPTH) ** 0.25

LANES = 128
HEADS = 4
HEAD_BLOCK = HEADS * HEAD_DIM

GATE_COLS = N_BRANCH * D_MODEL
A_COL = GATE_COLS
A_WIDTH = 512
B_BLK = (A_COL + A_WIDTH) // HEAD_BLOCK
C_BLK = B_BLK + 3
D1_BLK = C_BLK + 3
Z_BLKS = D1_BLK + 3
Z_COLS = Z_BLKS * HEAD_BLOCK
QKV_COLS = 3 * HEAD_BLOCK
assert DIL_CONFIGS[0][1] == 1 and all(r > 1 for _, r in DIL_CONFIGS[1:])

MLA_HEAD_PAD = 128
MLA_W = MLA_HEADS * MLA_HEAD_PAD
LOG2E = math.log2(math.e)
QK_SCALE = HEAD_DIM ** -0.5 * LOG2E

V7X_VMEM_BYTES = 64 * 1024 * 1024
VMEM_LIMIT = V7X_VMEM_BYTES - 8 * 1024 * 1024


def _cparams(sem):
    return pltpu.CompilerParams(dimension_semantics=sem, vmem_limit_bytes=VMEM_LIMIT)


def _layer_norm(u, g, b, eps=1e-5):
    mu = jnp.mean(u, axis=-1, keepdims=True)
    d = u - mu
    var = jnp.mean(d * d, axis=-1, keepdims=True)
    return d * lax.rsqrt(var + eps) * g + b


def _rms_norm(u, g, eps=1e-6):
    return u * lax.rsqrt(jnp.mean(u * u, axis=-1, keepdims=True) + eps) * g


def _ln_kernel(x_ref, g_ref, b_ref, hb_ref):
    hb_ref[...] = _layer_norm(x_ref[...], g_ref[...], b_ref[...]).astype(BF16)


def _embed_layer_norm(x2, g, b, tm=512):
    m, d = x2.shape
    row = pl.BlockSpec((tm, d), lambda i: (i, 0))
    vec = pl.BlockSpec((1, d), lambda i: (0, 0))
    return pl.pallas_call(
        _ln_kernel,
        grid=(m // tm,),
        in_specs=[row, vec, vec],
        out_specs=row,
        out_shape=jax.ShapeDtypeStruct((m, d), BF16),
        compiler_params=_cparams(("parallel",)),
        name="embed_ln",
    )(x2, g, b)


def _mm_kernel(a_ref, w_ref, o_ref):
    o_ref[...] = jnp.dot(a_ref[...], w_ref[...], preferred_element_type=F32).astype(o_ref.dtype)


def _layer_block(arr, layer):
    zeros = (0,) * (arr.ndim - 1)
    return pl.BlockSpec((None,) + arr.shape[1:], lambda *_: (layer,) + zeros)


def _matmul(a, w, layer, tm, tn, out_dtype, name):
    m, k = a.shape
    n = w.shape[2]
    return pl.pallas_call(
        _mm_kernel,
        grid=(m // tm, n // tn),
        in_specs=[pl.BlockSpec((tm, k), lambda i, j: (i, 0)),
                  pl.BlockSpec((None, k, tn), lambda i, j: (layer, 0, j))],
        out_specs=pl.BlockSpec((tm, tn), lambda i, j: (i, j)),
        out_shape=jax.ShapeDtypeStruct((m, n), out_dtype),
        compiler_params=_cparams(("parallel", "arbitrary")),
        name=name,
    )(a, w)


def _prep_w_in(w):
    nl = w.shape[0]
    o = 0
    a = w[..., o:o + Q_LORA + KV_LORA + MLA_ROPE]
    o += Q_LORA + KV_LORA + MLA_ROPE
    bq = w[..., o:o + HEAD_BLOCK]
    o += HEAD_BLOCK
    kvw = SWA_KV_HEADS * HEAD_DIM
    bk = w[..., o:o + kvw]
    o += kvw
    bv = w[..., o:o + kvw]
    o += kvw
    c = w[..., o:o + 3 * HEAD_BLOCK]
    o += 3 * HEAD_BLOCK
    d = w[..., o:o + 3 * DIL_GROUPS * HEAD_BLOCK]
    o += 3 * DIL_GROUPS * HEAD_BLOCK
    gates = w[..., o:]
    rep = SWA_HEADS // SWA_KV_HEADS

    def expand(t):
        t = t.reshape(nl, D_MODEL, SWA_KV_HEADS, 1, HEAD_DIM)
        return jnp.broadcast_to(t, (nl, D_MODEL, SWA_KV_HEADS, rep, HEAD_DIM)).reshape(nl, D_MODEL, HEAD_BLOCK)

    a = jnp.pad(a, ((0, 0), (0, 0), (0, A_WIDTH - a.shape[-1])))
    qkv_scale = jnp.concatenate([jnp.full((HEAD_BLOCK,), QK_SCALE, F32), jnp.ones((2 * HEAD_BLOCK,), F32)])
    c = c * qkv_scale
    d = d.reshape(nl, D_MODEL, 3, DIL_GROUPS, HEAD_BLOCK).transpose(0, 3, 1, 2, 4)
    d = d.reshape(nl, DIL_GROUPS, D_MODEL, QKV_COLS) * qkv_scale
    main = jnp.concatenate([0.5 * gates, a, QK_SCALE * bq, expand(bk), expand(bv), c, d[:, 0]], axis=-1)
    return main.astype(BF16), d[:, 1:].astype(BF16)


FOLD_STRIDE = 4


def _mm_fold_kernel(a_ref, w_ref, o_ref, acc_ref, tmp_ref, *, dilation):
    acc = jnp.dot(a_ref[...], w_ref[...], preferred_element_type=F32)
    nblk, tm, _ = acc_ref.shape
    for c in range(nblk):
        acc_ref[c] = acc[:, c * LANES:(c + 1) * LANES]
    if dilation <= FOLD_STRIDE:
        n = tm // dilation
        for j in range(dilation):
            for c in range(nblk):
                o_ref[j, :, c * LANES:(c + 1) * LANES] = (
                    acc_ref[c, pl.ds(j, n, stride=dilation), :].astype(o_ref.dtype))
    else:
        outer = dilation // FOLD_STRIDE
        n1, n = tm // FOLD_STRIDE, tm // dilation
        for c in range(nblk):
            for j1 in range(FOLD_STRIDE):
                tmp_ref[c, j1 * n1:(j1 + 1) * n1, :] = acc_ref[c, pl.ds(j1, n1, stride=FOLD_STRIDE), :]
        for j2 in range(outer):
            for j1 in range(FOLD_STRIDE):
                for c in range(nblk):
                    o_ref[FOLD_STRIDE * j2 + j1, :, c * LANES:(c + 1) * LANES] = (
                        tmp_ref[c, pl.ds(j1 * n1 + j2, n, stride=outer), :].astype(o_ref.dtype))


def _matmul_fold(a3, w, layer, group, dilation, tm, name):
    bsz, seq, k = a3.shape
    n = w.shape[-1]
    assert dilation <= FOLD_STRIDE or (dilation % FOLD_STRIDE == 0 and dilation // FOLD_STRIDE <= FOLD_STRIDE)
    slab = pltpu.VMEM((n // LANES, tm, LANES), F32)
    return pl.pallas_call(
        functools.partial(_mm_fold_kernel, dilation=dilation),
        grid=(bsz, seq // tm),
        in_specs=[pl.BlockSpec((None, tm, k), lambda b, i: (b, i, 0)),
                  pl.BlockSpec((None, None, k, n), lambda b, i: (layer, group, 0, 0))],
        out_specs=pl.BlockSpec((None, dilation, tm // dilation, n), lambda b, i: (b, 0, i, 0)),
        out_shape=jax.ShapeDtypeStruct((bsz, dilation, seq // dilation, n), BF16),
        scratch_shapes=[slab, slab],
        compiler_params=_cparams(("parallel", "parallel")),
        name=name,
    )(a3, w)


def _rope_tables(seq):
    half = MLA_ROPE // 2
    inv = ROPE_THETA ** (-jnp.arange(half, dtype=F32) / half)
    ang = jnp.arange(seq).astype(F32)[:, None] * inv[None, :]
    cos, sin = jnp.cos(ang), jnp.sin(ang)
    pad = MLA_HEAD_PAD - MLA_NOPE - MLA_ROPE
    cos_t = jnp.concatenate([jnp.ones((seq, MLA_NOPE), F32), cos, cos, jnp.zeros((seq, pad), F32)], axis=1)
    sin_t = jnp.concatenate([jnp.zeros((seq, MLA_NOPE), F32), sin, sin, jnp.zeros((seq, pad), F32)], axis=1)
    return cos_t, sin_t


def _rot_cols(w_rope):
    half = MLA_ROPE // 2
    return jnp.concatenate([-w_rope[..., half:], w_rope[..., :half]], axis=-1)


def _prep_mla_weights(w_uq, w_ukv):
    nl = w_uq.shape[0]
    pad = MLA_HEAD_PAD - MLA_NOPE - MLA_ROPE
    lead = ((0, 0), (0, 0), (0, 0))
    wq = w_uq.reshape(nl, Q_LORA, MLA_HEADS, MLA_NOPE + MLA_ROPE)
    wq_main = jnp.pad(wq, lead + ((0, pad),)).reshape(nl, Q_LORA, MLA_W)
    wq_rot = jnp.concatenate([jnp.zeros((nl, Q_LORA, MLA_HEADS, MLA_NOPE), F32), _rot_cols(wq[..., MLA_NOPE:]),
                              jnp.zeros((nl, Q_LORA, MLA_HEADS, pad), F32)], axis=-1).reshape(nl, Q_LORA, MLA_W)
    wkv = w_ukv.reshape(nl, KV_LORA, MLA_HEADS, MLA_NOPE + MLA_V)
    wk_n = jnp.pad(wkv[..., :MLA_NOPE], lead + ((0, MLA_HEAD_PAD - MLA_NOPE),)).reshape(nl, KV_LORA, MLA_W)
    wv = jnp.pad(wkv[..., MLA_NOPE:], lead + ((0, MLA_HEAD_PAD - MLA_V),)).reshape(nl, KV_LORA, MLA_W)
    eye = jnp.eye(MLA_ROPE, dtype=F32)
    place = jnp.concatenate([jnp.zeros((MLA_ROPE, MLA_NOPE), F32), eye, jnp.zeros((MLA_ROPE, pad), F32)], axis=1)
    place_rot = jnp.concatenate([jnp.zeros((MLA_ROPE, MLA_NOPE), F32), _rot_cols(eye),
                                 jnp.zeros((MLA_ROPE, pad), F32)], axis=1)

    def lift(pm):
        pm = jnp.pad(pm, ((0, LANES - MLA_ROPE), (0, 0)))
        return jnp.tile(pm, (1, MLA_HEADS))

    def per_layer(t):
        return jnp.broadcast_to(t, (nl,) + t.shape)

    wk_main = jnp.concatenate([wk_n, per_layer(lift(place))], axis=1)
    wk_rot = per_layer(jnp.concatenate([jnp.zeros((KV_LORA, MLA_W), F32), lift(place_rot)], axis=0))
    return tuple(t.astype(BF16) for t in (wq_main, wq_rot, wk_main, wk_rot, wv))


def _mla_prep_kernel(za_ref, cos_ref, sin_ref, gq_ref, gkv_ref, wq_ref, wqr_ref, wk_ref, wkr_ref, wv_ref,
                     q_ref, k_ref, vt_ref):
    za = za_ref[...]
    cq = _rms_norm(za[:, :Q_LORA].astype(F32), gq_ref[...]).astype(BF16)
    ckv = _rms_norm(za[:, Q_LORA:Q_LORA + KV_LORA].astype(F32), gkv_ref[...]).astype(BF16)
    cos = jnp.concatenate([cos_ref[...]] * MLA_HEADS, axis=1)
    sin = jnp.concatenate([sin_ref[...]] * MLA_HEADS, axis=1)
    dot = functools.partial(jnp.dot, preferred_element_type=F32)
    q = dot(cq, wq_ref[...]) * cos + dot(cq, wqr_ref[...]) * sin
    q_ref[...] = (q * ((MLA_NOPE + MLA_ROPE) ** -0.5 * LOG2E)).astype(BF16)
    lhs = jnp.concatenate([ckv, za[:, Q_LORA + KV_LORA:]], axis=1)
    k_ref[...] = (dot(lhs, wk_ref[...]) * cos + dot(lhs, wkr_ref[...]) * sin).astype(BF16)
    lane = lax.broadcasted_iota(jnp.int32, (1, MLA_W), 1)
    ones_col = jnp.where(lane % MLA_HEAD_PAD == MLA_V, 1.0, 0.0)
    vt_ref[...] = (dot(ckv, wv_ref[...]) + ones_col).T.astype(BF16)


def _mla_prep(z3, cos_t, sin_t, g_q, g_kv, weights, layer, tm=1024):
    bsz, seq, _ = z3.shape
    tm = min(tm, seq)
    params = (g_q, g_kv) + tuple(weights)
    tab = pl.BlockSpec((tm, MLA_HEAD_PAD), lambda b, i: (i, 0))
    out = pl.BlockSpec((None, tm, MLA_W), lambda b, i: (b, i, 0))
    shp = jax.ShapeDtypeStruct((bsz, seq, MLA_W), BF16)
    return pl.pallas_call(
        _mla_prep_kernel,
        grid=(bsz, seq // tm),
        in_specs=[pl.BlockSpec((None, tm, A_WIDTH), lambda b, i: (b, i, A_COL // A_WIDTH)), tab, tab]
        + [_layer_block(t, layer) for t in params],
        out_specs=[out, out, pl.BlockSpec((None, MLA_W, tm), lambda b, i: (b, 0, i))],
        out_shape=[shp, shp, jax.ShapeDtypeStruct((bsz, MLA_W, seq), BF16)],
        compiler_params=_cparams(("parallel", "parallel")),
        name="mla_prep",
    )(z3, cos_t, sin_t, *params)


def _mla_flash_kernel(q_ref, k_ref, vt_ref, o_ref, m_scr, acc_scr, s_scr, p_scr):
    ki = pl.program_id(2)

    @pl.when(ki == 0)
    def _():
        m_scr[...] = jnp.full(m_scr.shape, -jnp.inf, F32)
        acc_scr[...] = jnp.zeros(acc_scr.shape, F32)

    heads = [slice(h * MLA_HEAD_PAD, (h + 1) * MLA_HEAD_PAD) for h in range(MLA_HEADS)]
    for h, sl in enumerate(heads):
        s_scr[h] = lax.dot_general(k_ref[:, sl], q_ref[:, sl], (((1,), (1,)), ((), ())),
                                   preferred_element_type=F32)
    alphas = []
    for h in range(MLA_HEADS):
        s = s_scr[h]
        m_prev = m_scr[h]
        m_new = jnp.maximum(m_prev, jnp.max(s, axis=0, keepdims=True))
        p_scr[h] = jnp.exp2(s - m_new).astype(BF16)
        alphas.append(jnp.exp2(m_prev - m_new))
        m_scr[h] = m_new
    for h, sl in enumerate(heads):
        acc_scr[h] = alphas[h] * acc_scr[h] + jnp.dot(vt_ref[sl, :], p_scr[h], preferred_element_type=F32)

    @pl.when(ki == pl.num_programs(2) - 1)
    def _():
        for h, sl in enumerate(heads):
            acc = acc_scr[h]
            o_ref[:, sl] = (acc / acc[MLA_V:MLA_V + 1, :]).T.astype(o_ref.dtype)


def _mla_flash(q, k, vt, tq=1024, tk=1024):
    bsz, seq, _ = q.shape
    tq, tk = min(tq, seq), min(tk, seq)
    return pl.pallas_call(
        _mla_flash_kernel,
        grid=(bsz, seq // tq, seq // tk),
        in_specs=[pl.BlockSpec((None, tq, MLA_W), lambda b, i, j: (b, i, 0)),
                  pl.BlockSpec((None, tk, MLA_W), lambda b, i, j: (b, j, 0)),
                  pl.BlockSpec((None, MLA_W, tk), lambda b, i, j: (b, 0, j))],
        out_specs=pl.BlockSpec((None, tq, MLA_W), lambda b, i, j: (b, i, 0)),
        out_shape=jax.ShapeDtypeStruct((bsz, seq, MLA_W), BF16),
        scratch_shapes=[pltpu.VMEM((MLA_HEADS, 1, tq), F32),
                        pltpu.VMEM((MLA_HEADS, MLA_HEAD_PAD, tq), F32),
                        pltpu.VMEM((MLA_HEADS, tk, tq), F32),
                        pltpu.VMEM((MLA_HEADS, tk, tq), BF16)],
        compiler_params=_cparams(("parallel", "parallel", "arbitrary")),
        name="mla_flash",
    )(q, k, vt)


def _t5_bucket(rel):
    nb = T5_BUCKETS // 2
    max_exact = nb // 2
    n = jnp.abs(rel)
    large = max_exact + (jnp.log(jnp.maximum(n, 1).astype(F32) / max_exact)
                         / math.log(T5_MAX_DIST / max_exact) * (nb - max_exact)).astype(jnp.int32)
    large = jnp.minimum(large, nb - 1)
    return jnp.where(rel > 0, nb, 0) + jnp.where(n < max_exact, n, large)


def _band_bias_tables(rel_bias, head_lo, tile, half, dilation):
    width = tile + 2 * half
    col = jnp.arange(width)[None, :]
    rel = col - half - jnp.arange(tile)[:, None]
    onehot = jax.nn.one_hot(_t5_bucket(rel * dilation), T5_BUCKETS, dtype=F32)
    bias = jnp.einsum("qkn,nh->hqk", onehot, rel_bias[:, head_lo:head_lo + HEADS].astype(F32),
                      precision=lax.Precision.HIGHEST)
    band = jnp.abs(rel) <= half
    left_ok = col >= half
    right_ok = col < tile + half
    masks = jnp.stack([band, band & left_ok, band & right_ok, band & left_ok & right_ok])
    return jnp.where(masks[:, None], LOG2E * bias[None].astype(F32), NEG_INF)


def _head_of_lane():
    return lax.broadcasted_iota(jnp.int32, (1, HEAD_BLOCK), 1) // HEAD_DIM


def _per_head_lanes(cols):
    head = _head_of_lane()
    out = cols[HEADS - 1]
    for h in range(HEADS - 2, -1, -1):
        out = jnp.where(head == h, cols[h], out)
    return out


def _packed_heads_attention(q, kwin, vwin, bias, sink_rows):
    t = q.shape[0]
    head = _head_of_lane()
    qs = jnp.concatenate([jnp.where(head == h, q, 0) for h in range(HEADS)], axis=0)
    s = lax.dot_general(qs, kwin, (((1,), (1,)), ((), ())), preferred_element_type=F32)
    s = s + bias.reshape(HEADS * t, bias.shape[-1])
    if sink_rows is not None:
        m = jnp.max(jnp.concatenate([jnp.maximum(s[:, :LANES], sink_rows), s[:, LANES:]], axis=1),
                    axis=1, keepdims=True)
    else:
        m = jnp.max(s, axis=1, keepdims=True)
    e = jnp.exp2(s - m)
    if sink_rows is not None:
        lane = lax.broadcasted_iota(jnp.int32, (1, LANES), 1)
        e_sink = jnp.where(lane == 0, jnp.exp2(sink_rows - m), 0.0)
        den = jnp.sum(jnp.concatenate([e[:, :LANES] + e_sink, e[:, LANES:]], axis=1), axis=1, keepdims=True)
    else:
        den = jnp.sum(e, axis=1, keepdims=True)
    o = jnp.dot(e.astype(BF16), vwin, preferred_element_type=F32) * (1.0 / den)
    out = o[(HEADS - 1) * t:]
    for h in range(HEADS - 2, -1, -1):
        out = jnp.where(head == h, o[h * t:(h + 1) * t], out)
    return out, m, den


def _banded_kernel(*refs, tile, subtiles, half, length, use_sink, want_lse):
    refs = list(refs)
    sink_ref = refs.pop(0) if use_sink else None
    q_ref, k_ref, v_ref, bias_ref, o_ref = refs[:5]
    lse_ref = refs[5] if want_lse else None
    i = pl.program_id(2)
    nt = length // tile
    width = tile + 2 * half
    head = _head_of_lane()
    for u in range(subtiles):
        t = i * subtiles + u
        main = t * tile
        can_first, can_last = u == 0, u == subtiles - 1
        if can_first or can_last:
            left = jnp.maximum(main - half, 0) if can_first else main - half
            right = jnp.minimum(main + tile, length - half) if can_last else main + tile
            left, right = pl.multiple_of(left, half), pl.multiple_of(right, half)
            main = pl.multiple_of(main, tile)

            def window(ref):
                return jnp.concatenate([ref[pl.ds(left, half), :], ref[pl.ds(main, tile), :],
                                        ref[pl.ds(right, half), :]], axis=0)

            variant = 0
            if can_first:
                variant = variant + (t == 0).astype(jnp.int32)
            if can_last:
                variant = variant + 2 * (t == nt - 1).astype(jnp.int32)
        else:
            start = pl.multiple_of(main - half, half)

            def window(ref):
                return ref[pl.ds(start, width), :]

            variant = 0
        rows = slice(u * tile, (u + 1) * tile)
        kwin, vwin = window(k_ref), window(v_ref)
        q = q_ref[rows, :]
        sink_rows = sink_ref[...] if use_sink else None
        out, m, den = _packed_heads_attention(q, kwin, vwin, bias_ref[variant], sink_rows)
        o_ref[rows, :] = out.astype(o_ref.dtype)
        if want_lse:
            lse = m + jnp.log2(den)
            lse_ref[rows, :] = _per_head_lanes([lse[h * tile:(h + 1) * tile] for h in range(HEADS)])


def _sink_rows(sink, tile):
    rows = jnp.repeat(LOG2E * sink.astype(F32), tile, axis=1)
    return jnp.broadcast_to(rows[:, :, None], rows.shape + (LANES,))


def _banded_attention(zf, blk, bias, sink_rows, layer, *, half, subtiles, want_lse):
    bsz, dilation, length, _ = zf.shape
    tile = bias.shape[2]
    subtiles = min(subtiles, length // tile)
    step = tile * subtiles
    use_sink = sink_rows is not None
    in_specs = [pl.BlockSpec((None, None, step, HEAD_BLOCK), lambda b, j, i: (b, j, i, blk)),
                pl.BlockSpec((None, None, length, HEAD_BLOCK), lambda b, j, i: (b, j, 0, blk + 1)),
                pl.BlockSpec((None, None, length, HEAD_BLOCK), lambda b, j, i: (b, j, 0, blk + 2)),
                pl.BlockSpec(bias.shape, lambda b, j, i: (0, 0, 0, 0))]
    args = [zf, zf, zf, bias]
    if use_sink:
        in_specs.insert(0, _layer_block(sink_rows, layer))
        args.insert(0, sink_rows)
    o_spec = pl.BlockSpec((None, None, step, HEAD_BLOCK), lambda b, j, i: (b, j, i, 0))
    out_specs = [o_spec]
    out_shape = [jax.ShapeDtypeStruct((bsz, dilation, length, HEAD_BLOCK), BF16)]
    if want_lse:
        out_specs.append(o_spec)
        out_shape.append(jax.ShapeDtypeStruct((bsz, dilation, length, HEAD_BLOCK), F32))
    return pl.pallas_call(
        functools.partial(_banded_kernel, tile=tile, subtiles=subtiles, half=half, length=length,
                          use_sink=use_sink, want_lse=want_lse),
        grid=(bsz, dilation, length // step),
        in_specs=in_specs,
        out_specs=out_specs,
        out_shape=out_shape,
        compiler_params=_cparams(("parallel", "parallel", "arbitrary")),
        name=f"banded_r{dilation}",
    )(*args)


NA_TILE_ROWS = 4
NA_WIN_ROWS = NA_TILE_ROWS + NA_KH
NA_SUBTILES = 4


def _na_bias_tables(rpb, rows):
    nl = rpb.shape[0]
    tr, wr = NA_TILE_ROWS, NA_WIN_ROWS
    exact = functools.partial(jnp.einsum, precision=lax.Precision.HIGHEST)
    c = np.arange(GRID_W)
    cs = np.clip(c - NA_KW // 2, 0, GRID_W - NA_KW)
    col_ok = (c[None, :] >= cs[:, None]) & (c[None, :] < cs[:, None] + NA_KW)
    dc = np.clip(c[None, :] - c[:, None], -(NA_KW - 1), NA_KW - 1) + (NA_KW - 1)
    by_col = exact("lhdn,qkn->lhdqk", rpb.astype(F32), np.eye(2 * NA_KW - 1, dtype=np.float32)[dc])
    tabs = []
    for r0, ws in ((0, 0), (tr, 0), (rows - tr, rows - wr)):
        r = r0 + np.arange(tr)
        start = np.clip(r - NA_KH // 2, 0, rows - NA_KH)
        kr = ws + np.arange(wr)
        row_ok = (kr[None, :] >= start[:, None]) & (kr[None, :] < start[:, None] + NA_KH)
        dr = np.clip(kr[None, :] - r[:, None] + (NA_KH - 1), 0, 2 * NA_KH - 2)
        bias = exact("rad,lhdqk->lhrqak", np.eye(2 * NA_KH - 1, dtype=np.float32)[dr], by_col)
        ok = row_ok[:, None, :, None] & col_ok[None, :, None, :]
        tabs.append(jnp.where(ok, LOG2E * bias, NEG_INF).reshape(nl, HEADS, tr * GRID_W, wr * GRID_W))
    return jnp.stack(tabs, axis=1)


def _na_kernel(q_ref, k_ref, v_ref, bias_ref, o_ref, *, rows):
    i = pl.program_id(1)
    nt = rows // NA_TILE_ROWS
    win = NA_WIN_ROWS * GRID_W
    tq = NA_TILE_ROWS * GRID_W
    head = _head_of_lane()
    for u in range(NA_SUBTILES):
        t = i * NA_SUBTILES + u
        ws = jnp.clip(t * NA_TILE_ROWS - NA_KH // 2, 0, rows - NA_WIN_ROWS) * GRID_W
        ws = pl.multiple_of(ws, GRID_W)
        variant = jnp.where(t == 0, 0, jnp.where(t == nt - 1, 2, 1))
        kwin = k_ref[pl.ds(ws, win), :]
        vwin = v_ref[pl.ds(ws, win), :]
        q = q_ref[u * tq:(u + 1) * tq, :]
        acc = None
        inv_den = []
        for h in range(HEADS):
            mine = head == h
            s = lax.dot_general(q, jnp.where(mine, kwin, 0), (((1,), (1,)), ((), ())),
                                preferred_element_type=F32)
            s = s + bias_ref[variant, h]
            m = jnp.max(s, axis=1, keepdims=True)
            e = jnp.exp2(s - m)
            inv_den.append(1.0 / jnp.sum(e, axis=1, keepdims=True))
            part = jnp.dot(e.astype(BF16), jnp.where(mine, vwin, 0), preferred_element_type=F32)
            acc = part if acc is None else acc + part
        o_ref[u * tq:(u + 1) * tq, :] = (acc * _per_head_lanes(inv_den)).astype(o_ref.dtype)


def _neighborhood_attention(z3, bias, layer):
    bsz, seq, _ = z3.shape
    rows = seq // GRID_W
    tq = NA_TILE_ROWS * GRID_W * NA_SUBTILES
    return pl.pallas_call(
        functools.partial(_na_kernel, rows=rows),
        grid=(bsz, rows // (NA_TILE_ROWS * NA_SUBTILES)),
        in_specs=[pl.BlockSpec((None, tq, HEAD_BLOCK), lambda b, i: (b, i, C_BLK)),
                  pl.BlockSpec((None, seq, HEAD_BLOCK), lambda b, i: (b, 0, C_BLK + 1)),
                  pl.BlockSpec((None, seq, HEAD_BLOCK), lambda b, i: (b, 0, C_BLK + 2)),
                  _layer_block(bias, layer)],
        out_specs=pl.BlockSpec((None, tq, HEAD_BLOCK), lambda b, i: (b, i, 0)),
        out_shape=jax.ShapeDtypeStruct((bsz, seq, HEAD_BLOCK), BF16),
        compiler_params=_cparams(("parallel", "arbitrary")),
        name="neighborhood",
    )(z3, z3, z3, bias)


def _unfold(ref, scr):
    dilation, n, cols = ref.shape
    if dilation == 1:
        return ref[0].astype(F32)
    nblk = cols // LANES
    for j in range(dilation):
        for c in range(nblk):
            scr[c, pl.ds(j, n, stride=dilation), :] = ref[j, :, c * LANES:(c + 1) * LANES].astype(F32)
    return jnp.concatenate([scr[c] for c in range(nblk)], axis=1)


def _merge_rows(gate_ref, ya_ref, yb_ref, yc_ref, o0_ref, o1_ref, o2_ref, l0_ref, l1_ref, l2_ref, h_ref,
                eg_ref, eb_ref, wa_ref, wb_ref, wo_ref, g_ref, b_ref, scr, first_layer):
    dot = functools.partial(jnp.dot, preferred_element_type=F32)
    l0, l1, l2 = _unfold(l0_ref, None), _unfold(l1_ref, scr[0]), _unfold(l2_ref, scr[1])
    o0, o1, o2 = _unfold(o0_ref, None), _unfold(o1_ref, scr[2]), _unfold(o2_ref, scr[3])
    lm = jnp.maximum(jnp.maximum(l0, l1), l2)
    e0, e1, e2 = jnp.exp2(l0 - lm), jnp.exp2(l1 - lm), jnp.exp2(l2 - lm)
    yd = (e0 * o0 + e1 * o1 + e2 * o2) / (e0 + e1 + e2)
    branches = (dot(ya_ref[...], wa_ref[...]), dot(yb_ref[0], wb_ref[0]), dot(yc_ref[...], wb_ref[1]),
                dot(yd.astype(BF16), wb_ref[2]))
    merged = None
    for n in range(N_BRANCH):
        gate2 = 1.0 + jnp.tanh(gate_ref[:, n * D_MODEL:(n + 1) * D_MODEL].astype(F32))
        term = gate2 * branches[n]
        merged = term if merged is None else merged + term
    h = h_ref[...]
    if first_layer:
        h = _layer_norm(h, eg_ref[...], eb_ref[...])
    u = ALPHA * h + dot(merged.astype(BF16), wo_ref[...])
    return _layer_norm(u, g_ref[...], b_ref[...])


FF_CHUNK = 1024
N_MERGE_REFS = 18
TAIL_SUBTILES = 2


def _ffn_rows(h, p_ref, w1_ref, w2_ref, wp_ref, wg_ref, g_ref, b_ref):
    dot = functools.partial(jnp.dot, preferred_element_type=F32)
    hb = h.astype(BF16)
    ff = None
    for c in range(D_FF // FF_CHUNK):
        sl = slice(c * FF_CHUNK, (c + 1) * FF_CHUNK)
        a = jnp.maximum(dot(hb, w1_ref[:, sl]), 0.0)
        part = dot((a * a).astype(BF16), w2_ref[sl, :])
        ff = part if ff is None else ff + part
    ple = dot(p_ref[...].astype(BF16), wp_ref[...]) * (1.0 + jnp.tanh(dot(hb, wg_ref[...])))
    return _layer_norm(ALPHA * h + ff + ple, g_ref[...], b_ref[...])


def _layer_tail_kernel(*refs, first_layer):
    merge_refs, rest = refs[:N_MERGE_REFS], refs[N_MERGE_REFS:]
    ffn_refs, (h_out_ref, hb_out_ref), scr = rest[:7], rest[7:9], rest[9:]
    tm = h_out_ref.shape[0] // TAIL_SUBTILES

    def rows(ref, u):
        return ref.at[pl.ds(u * tm, tm), :]

    def fold(ref, u):
        n = ref.shape[1] // TAIL_SUBTILES
        return ref.at[:, pl.ds(u * n, n), :]

    h1 = []
    for u in range(TAIL_SUBTILES):
        gate, ya, yb, yc, o0, o1, o2, l0, l1, l2, h = merge_refs[:11]
        views = (rows(gate, u), rows(ya, u), fold(yb, u), rows(yc, u), fold(o0, u), fold(o1, u), fold(o2, u),
                 fold(l0, u), fold(l1, u), fold(l2, u), rows(h, u))
        sub_scr = tuple(s.at[:, pl.ds(u * tm, tm), :] for s in scr)
        h1.append(_merge_rows(*views, *merge_refs[11:], sub_scr, first_layer))
    for u in range(TAIL_SUBTILES):
        y = _ffn_rows(h1[u], rows(ffn_refs[0], u), *ffn_refs[1:])
        h_out_ref[pl.ds(u * tm, tm), :] = y
        hb_out_ref[pl.ds(u * tm, tm), :] = y.astype(BF16)


def _layer_tail(z3, ya, yb, yc, od, ld, h3, embed_params, p4, merge_params, ffn_params, layer, tm=512):
    bsz, seq, _ = h3.shape

    def rows(width):
        return pl.BlockSpec((None, tm, width), lambda b, i: (b, i, 0))

    def folded(t):
        dilation = t.shape[1]
        return pl.BlockSpec((None, dilation, tm // dilation, HEAD_BLOCK), lambda b, i: (b, 0, i, 0))

    def weights(t):
        zeros = (0,) * (t.ndim - 1)
        return pl.BlockSpec((None,) + t.shape[1:], lambda b, i: (layer,) + zeros, pipeline_mode=pl.Buffered(1))

    in_specs = ([rows(GATE_COLS), rows(MLA_W), folded(yb), rows(HEAD_BLOCK)]
                + [folded(t) for t in od] + [folded(t) for t in ld] + [rows(D_MODEL)]
                + [pl.BlockSpec(t.shape, lambda b, i: (0, 0)) for t in embed_params]
                + [weights(t) for t in merge_params]
                + [pl.BlockSpec((None, None, tm, PLE_DIM), lambda b, i: (layer, b, i, 0))]
                + [weights(t) for t in ffn_params])
    assert len(in_specs) == N_MERGE_REFS + 7
    return pl.pallas_call(
        functools.partial(_layer_tail_kernel, first_layer=layer == 0),
        grid=(bsz, seq // tm),
        in_specs=in_specs,
        out_specs=[rows(D_MODEL), rows(D_MODEL)],
        out_shape=[jax.ShapeDtypeStruct((bsz, seq, D_MODEL), F32), jax.ShapeDtypeStruct((bsz, seq, D_MODEL), BF16)],
        scratch_shapes=[pltpu.VMEM((HEAD_BLOCK // LANES, tm, LANES), F32)] * 4,
        compiler_params=_cparams(("parallel", "parallel")),
        name="layer_tail",
    )(z3, ya, yb, yc, *od, *ld, h3, *embed_params, *merge_params, p4, *ffn_params)


SWA_TILE, SWA_SUBTILES = 256, 4
DIL_TILE, DIL_SUBTILES = 128, 8


def kernel(x, p, ln_emb_g, ln_emb_b, rel_bias, w_in, mla_q_norm, mla_w_uq, mla_kv_norm, mla_w_ukv, swa_sink,
           na_rpb, w_branch, w_out, ln1_g, ln1_b, w_ff1, w_ff2, w_ple, w_ple_gate, ln2_g, ln2_b):
    bsz, seq, _ = x.shape
    m = bsz * seq
    rows = seq // GRID_W
    cos_t, sin_t = _rope_tables(seq)
    swa_bias = _band_bias_tables(rel_bias, 0, min(SWA_TILE, seq), SWA_HALF, 1)
    dil_bias = []
    for g, (window, r) in enumerate(DIL_CONFIGS):
        half = window // (2 * r)
        dil_bias.append(_band_bias_tables(rel_bias, SWA_HEADS + g * DIL_HEADS, min(DIL_TILE, seq // r), half, r))

    nl = w_in.shape[0]
    w_main, w_dil = _prep_w_in(w_in)
    mla_weights = _prep_mla_weights(mla_w_uq, mla_w_ukv)
    g_q, g_kv = mla_q_norm.reshape(nl, 1, Q_LORA), mla_kv_norm.reshape(nl, 1, KV_LORA)
    swa_sinks = _sink_rows(swa_sink, swa_bias.shape[2])
    na_bias = _na_bias_tables(na_rpb, rows)
    wa = jnp.pad(w_branch[:, 0].reshape(nl, MLA_HEADS, MLA_V, D_MODEL),
                 ((0, 0), (0, 0), (0, MLA_HEAD_PAD - MLA_V), (0, 0))).reshape(nl, MLA_W, D_MODEL).astype(BF16)
    merge_params = (wa, w_branch[:, 1:].astype(BF16), (0.5 * w_out).astype(BF16),
                    ln1_g.reshape(nl, 1, D_MODEL), ln1_b.reshape(nl, 1, D_MODEL))
    ffn_params = (w_ff1.astype(BF16), w_ff2.astype(BF16), (0.5 * w_ple).astype(BF16),
                  (0.5 * w_ple_gate).astype(BF16), ln2_g.reshape(nl, 1, D_MODEL), ln2_b.reshape(nl, 1, D_MODEL))

    embed_params = (ln_emb_g.reshape(1, D_MODEL), ln_emb_b.reshape(1, D_MODEL))
    hb = _embed_layer_norm(x.reshape(m, D_MODEL), *embed_params)
    h = x
    for i in range(DEPTH):
        z3 = _matmul(hb, w_main, i, 1024, 2304, BF16, "in_proj").reshape(bsz, seq, Z_COLS)
        z4 = z3.reshape(bsz, 1, seq, Z_COLS)
        hb3 = hb.reshape(bsz, seq, D_MODEL)
        q_a, k_a, vt_a = _mla_prep(z3, cos_t, sin_t, g_q, g_kv, mla_weights, i)
        y_a = _mla_flash(q_a, k_a, vt_a)
        (y_b,) = _banded_attention(z4, B_BLK, swa_bias, swa_sinks, i,
                                   half=SWA_HALF, subtiles=SWA_SUBTILES, want_lse=False)
        y_c = _neighborhood_attention(z3, na_bias, i)
        o_d, l_d = [], []
        for g, (window, r) in enumerate(DIL_CONFIGS):
            if r == 1:
                zd, blk = z4, D1_BLK
            else:
                zd, blk = _matmul_fold(hb3, w_dil, i, g - 1, r, 1024, f"in_proj_r{r}"), 0
            o_g, l_g = _banded_attention(zd, blk, dil_bias[g], None, i, half=window // (2 * r),
                                         subtiles=DIL_SUBTILES, want_lse=True)
            o_d.append(o_g)
            l_d.append(l_g)
        h, hb = _layer_tail(z3, y_a, y_b, y_c, o_d, l_d, h, embed_params, p, merge_params, ffn_params, i)
        hb = hb.reshape(m, D_MODEL)
    return h
```

```python
import functools
import math

import jax
import jax.numpy as jnp
import numpy as np
from jax import lax
from jax.experimental import pallas as pl
from jax.experimental.pallas import tpu as pltpu

F32 = jnp.float32
BF16 = jnp.bfloat16

D_MODEL = 1024
DEPTH = 2
HEAD_DIM = 64
GRID_W = 64
NEG_INF = -1e30

MLA_HEADS = 4
MLA_NOPE = 64
MLA_ROPE = 32
MLA_V = 64
Q_LORA = 256
KV_LORA = 128
ROPE_THETA = 10000.0

SWA_HEADS = 4
SWA_KV_HEADS = 2
SWA_HALF = 128

NA_HEADS = 4
NA_KH = 8
NA_KW = 16

DIL_CONFIGS = ((128, 1), (512, 4), (2048, 16))
DIL_GROUPS = 3
DIL_HEADS = 4

T5_BUCKETS = 32
T5_MAX_DIST = 1024

N_BRANCH = 4
BRANCH_W = 256
D_FF = 4 * D_MODEL
PLE_DIM = 256

ALPHA = (2 * DEPTH) ** 0.25

LANES = 128
HEADS = 4
HEAD_BLOCK = HEADS * HEAD_DIM

GATE_COLS = N_BRANCH * D_MODEL
A_COL = GATE_COLS
A_WIDTH = 512
B_BLK = (A_COL + A_WIDTH) // HEAD_BLOCK
C_BLK = B_BLK + 3
D1_BLK = C_BLK + 3
Z_BLKS = D1_BLK + 3
Z_COLS = Z_BLKS * HEAD_BLOCK
QKV_COLS = 3 * HEAD_BLOCK
assert DIL_CONFIGS[0][1] == 1 and all(r > 1 for _, r in DIL_CONFIGS[1:])

MLA_HEAD_PAD = 128
MLA_W = MLA_HEADS * MLA_HEAD_PAD
LOG2E = math.log2(math.e)
QK_SCALE = HEAD_DIM ** -0.5 * LOG2E

V7X_VMEM_BYTES = 64 * 1024 * 1024
VMEM_LIMIT = V7X_VMEM_BYTES - 8 * 1024 * 1024


def _cparams(sem):
    return pltpu.CompilerParams(dimension_semantics=sem, vmem_limit_bytes=VMEM_LIMIT)


def _layer_norm(u, g, b, eps=1e-5):
    mu = jnp.mean(u, axis=-1, keepdims=True)
    d = u - mu
    var = jnp.mean(d * d, axis=-1, keepdims=True)
    return d * lax.rsqrt(var + eps) * g + b


def _rms_norm(u, g, eps=1e-6):
    return u * lax.rsqrt(jnp.mean(u * u, axis=-1, keepdims=True) + eps) * g


def _ln_kernel(x_ref, g_ref, b_ref, hb_ref):
    hb_ref[...] = _layer_norm(x_ref[...], g_ref[...], b_ref[...]).astype(BF16)


def _embed_layer_norm(x2, g, b, tm=512):
    m, d = x2.shape
    row = pl.BlockSpec((tm, d), lambda i: (i, 0))
    vec = pl.BlockSpec((1, d), lambda i: (0, 0))
    return pl.pallas_call(
        _ln_kernel,
        grid=(m // tm,),
        in_specs=[row, vec, vec],
        out_specs=row,
        out_shape=jax.ShapeDtypeStruct((m, d), BF16),
        compiler_params=_cparams(("parallel",)),
        name="embed_ln",
    )(x2, g, b)


def _mm_kernel(a_ref, w_ref, o_ref):
    o_ref[...] = jnp.dot(a_ref[...], w_ref[...], preferred_element_type=F32).astype(o_ref.dtype)


def _layer_block(arr, layer):
    zeros = (0,) * (arr.ndim - 1)
    return pl.BlockSpec((None,) + arr.shape[1:], lambda *_: (layer,) + zeros)


def _matmul(a, w, layer, tm, tn, out_dtype, name):
    m, k = a.shape
    n = w.shape[2]
    return pl.pallas_call(
        _mm_kernel,
        grid=(m // tm, n // tn),
        in_specs=[pl.BlockSpec((tm, k), lambda i, j: (i, 0)),
                  pl.BlockSpec((None, k, tn), lambda i, j: (layer, 0, j))],
        out_specs=pl.BlockSpec((tm, tn), lambda i, j: (i, j)),
        out_shape=jax.ShapeDtypeStruct((m, n), out_dtype),
        compiler_params=_cparams(("parallel", "arbitrary")),
        name=name,
    )(a, w)


def _prep_w_in(w):
    nl = w.shape[0]
    o = 0
    a = w[..., o:o + Q_LORA + KV_LORA + MLA_ROPE]
    o += Q_LORA + KV_LORA + MLA_ROPE
    bq = w[..., o:o + HEAD_BLOCK]
    o += HEAD_BLOCK
    kvw = SWA_KV_HEADS * HEAD_DIM
    bk = w[..., o:o + kvw]
    o += kvw
    bv = w[..., o:o + kvw]
    o += kvw
    c = w[..., o:o + 3 * HEAD_BLOCK]
    o += 3 * HEAD_BLOCK
    d = w[..., o:o + 3 * DIL_GROUPS * HEAD_BLOCK]
    o += 3 * DIL_GROUPS * HEAD_BLOCK
    gates = w[..., o:]
    rep = SWA_HEADS // SWA_KV_HEADS

    def expand(t):
        t = t.reshape(nl, D_MODEL, SWA_KV_HEADS, 1, HEAD_DIM)
        return jnp.broadcast_to(t, (nl, D_MODEL, SWA_KV_HEADS, rep, HEAD_DIM)).reshape(nl, D_MODEL, HEAD_BLOCK)

    a = jnp.pad(a, ((0, 0), (0, 0), (0, A_WIDTH - a.shape[-1])))
    qkv_scale = jnp.concatenate([jnp.full((HEAD_BLOCK,), QK_SCALE, F32), jnp.ones((2 * HEAD_BLOCK,), F32)])
    c = c * qkv_scale
    d = d.reshape(nl, D_MODEL, 3, DIL_GROUPS, HEAD_BLOCK).transpose(0, 3, 1, 2, 4)
    d = d.reshape(nl, DIL_GROUPS, D_MODEL, QKV_COLS) * qkv_scale
    main = jnp.concatenate([0.5 * gates, a, QK_SCALE * bq, expand(bk), expand(bv), c, d[:, 0]], axis=-1)
    return main.astype(BF16), d[:, 1:].astype(BF16)


FOLD_STRIDE = 4


def _mm_fold_kernel(a_ref, w_ref, o_ref, acc_ref, tmp_ref, *, dilation):
    acc = jnp.dot(a_ref[...], w_ref[...], preferred_element_type=F32)
    nblk, tm, _ = acc_ref.shape
    for c in range(nblk):
        acc_ref[c] = acc[:, c * LANES:(c + 1) * LANES]
    if dilation <= FOLD_STRIDE:
        n = tm // dilation
        for j in range(dilation):
            for c in range(nblk):
                o_ref[j, :, c * LANES:(c + 1) * LANES] = (
                    acc_ref[c, pl.ds(j, n, stride=dilation), :].astype(o_ref.dtype))
    else:
        outer = dilation // FOLD_STRIDE
        n1, n = tm // FOLD_STRIDE, tm // dilation
        for c in range(nblk):
            for j1 in range(FOLD_STRIDE):
                tmp_ref[c, j1 * n1:(j1 + 1) * n1, :] = acc_ref[c, pl.ds(j1, n1, stride=FOLD_STRIDE), :]
        for j2 in range(outer):
            for j1 in range(FOLD_STRIDE):
                for c in range(nblk):
                    o_ref[FOLD_STRIDE * j2 + j1, :, c * LANES:(c + 1) * LANES] = (
                        tmp_ref[c, pl.ds(j1 * n1 + j2, n, stride=outer), :].astype(o_ref.dtype))


def _matmul_fold(a3, w, layer, group, dilation, tm, name):
    bsz, seq, k = a3.shape
    n = w.shape[-1]
    assert dilation <= FOLD_STRIDE or (dilation % FOLD_STRIDE == 0 and dilation // FOLD_STRIDE <= FOLD_STRIDE)
    slab = pltpu.VMEM((n // LANES, tm, LANES), F32)
    return pl.pallas_call(
        functools.partial(_mm_fold_kernel, dilation=dilation),
        grid=(bsz, seq // tm),
        in_specs=[pl.BlockSpec((None, tm, k), lambda b, i: (b, i, 0)),
                  pl.BlockSpec((None, None, k, n), lambda b, i: (layer, group, 0, 0))],
        out_specs=pl.BlockSpec((None, dilation, tm // dilation, n), lambda b, i: (b, 0, i, 0)),
        out_shape=jax.ShapeDtypeStruct((bsz, dilation, seq // dilation, n), BF16),
        scratch_shapes=[slab, slab],
        compiler_params=_cparams(("parallel", "parallel")),
        name=name,
    )(a3, w)


def _rope_tables(seq):
    half = MLA_ROPE // 2
    inv = ROPE_THETA ** (-jnp.arange(half, dtype=F32) / half)
    ang = jnp.arange(seq).astype(F32)[:, None] * inv[None, :]
    cos, sin = jnp.cos(ang), jnp.sin(ang)
    pad = MLA_HEAD_PAD - MLA_NOPE - MLA_ROPE
    cos_t = jnp.concatenate([jnp.ones((seq, MLA_NOPE), F32), cos, cos, jnp.zeros((seq, pad), F32)], axis=1)
    sin_t = jnp.concatenate([jnp.zeros((seq, MLA_NOPE), F32), sin, sin, jnp.zeros((seq, pad), F32)], axis=1)
    return cos_t, sin_t


def _rot_cols(w_rope):
    half = MLA_ROPE // 2
    return jnp.concatenate([-w_rope[..., half:], w_rope[..., :half]], axis=-1)


def _prep_mla_weights(w_uq, w_ukv):
    nl = w_uq.shape[0]
    pad = MLA_HEAD_PAD - MLA_NOPE - MLA_ROPE
    lead = ((0, 0), (0, 0), (0, 0))
    wq = w_uq.reshape(nl, Q_LORA, MLA_HEADS, MLA_NOPE + MLA_ROPE)
    wq_main = jnp.pad(wq, lead + ((0, pad),)).reshape(nl, Q_LORA, MLA_W)
    wq_rot = jnp.concatenate([jnp.zeros((nl, Q_LORA, MLA_HEADS, MLA_NOPE), F32), _rot_cols(wq[..., MLA_NOPE:]),
                              jnp.zeros((nl, Q_LORA, MLA_HEADS, pad), F32)], axis=-1).reshape(nl, Q_LORA, MLA_W)
    wkv = w_ukv.reshape(nl, KV_LORA, MLA_HEADS, MLA_NOPE + MLA_V)
    wk_n = jnp.pad(wkv[..., :MLA_NOPE], lead + ((0, MLA_HEAD_PAD - MLA_NOPE),)).reshape(nl, KV_LORA, MLA_W)
    wv = jnp.pad(wkv[..., MLA_NOPE:], lead + ((0, MLA_HEAD_PAD - MLA_V),)).reshape(nl, KV_LORA, MLA_W)
    eye = jnp.eye(MLA_ROPE, dtype=F32)
    place = jnp.concatenate([jnp.zeros((MLA_ROPE, MLA_NOPE), F32), eye, jnp.zeros((MLA_ROPE, pad), F32)], axis=1)
    place_rot = jnp.concatenate([jnp.zeros((MLA_ROPE, MLA_NOPE), F32), _rot_cols(eye),
                                 jnp.zeros((MLA_ROPE, pad), F32)], axis=1)

    def lift(pm):
        pm = jnp.pad(pm, ((0, LANES - MLA_ROPE), (0, 0)))
        return jnp.tile(pm, (1, MLA_HEADS))

    def per_layer(t):
        return jnp.broadcast_to(t, (nl,) + t.shape)

    wk_main = jnp.concatenate([wk_n, per_layer(lift(place))], axis=1)
    wk_rot = per_layer(jnp.concatenate([jnp.zeros((KV_LORA, MLA_W), F32), lift(place_rot)], axis=0))
    return tuple(t.astype(BF16) for t in (wq_main, wq_rot, wk_main, wk_rot, wv))


def _mla_prep_kernel(za_ref, cos_ref, sin_ref, gq_ref, gkv_ref, wq_ref, wqr_ref, wk_ref, wkr_ref, wv_ref,
                     q_ref, k_ref, vt_ref):
    za = za_ref[...]
    cq = _rms_norm(za[:, :Q_LORA].astype(F32), gq_ref[...]).astype(BF16)
    ckv = _rms_norm(za[:, Q_LORA:Q_LORA + KV_LORA].astype(F32), gkv_ref[...]).astype(BF16)
    cos = jnp.concatenate([cos_ref[...]] * MLA_HEADS, axis=1)
    sin = jnp.concatenate([sin_ref[...]] * MLA_HEADS, axis=1)
    dot = functools.partial(jnp.dot, preferred_element_type=F32)
    q = dot(cq, wq_ref[...]) * cos + dot(cq, wqr_ref[...]) * sin
    q_ref[...] = (q * ((MLA_NOPE + MLA_ROPE) ** -0.5 * LOG2E)).astype(BF16)
    lhs = jnp.concatenate([ckv, za[:, Q_LORA + KV_LORA:]], axis=1)
    k_ref[...] = (dot(lhs, wk_ref[...]) * cos + dot(lhs, wkr_ref[...]) * sin).astype(BF16)
    lane = lax.broadcasted_iota(jnp.int32, (1, MLA_W), 1)
    ones_col = jnp.where(lane % MLA_HEAD_PAD == MLA_V, 1.0, 0.0)
    vt_ref[...] = (dot(ckv, wv_ref[...]) + ones_col).T.astype(BF16)


def _mla_prep(z3, cos_t, sin_t, g_q, g_kv, weights, layer, tm=1024):
    bsz, seq, _ = z3.shape
    tm = min(tm, seq)
    params = (g_q, g_kv) + tuple(weights)
    tab = pl.BlockSpec((tm, MLA_HEAD_PAD), lambda b, i: (i, 0))
    out = pl.BlockSpec((None, tm, MLA_W), lambda b, i: (b, i, 0))
    shp = jax.ShapeDtypeStruct((bsz, seq, MLA_W), BF16)
    return pl.pallas_call(
        _mla_prep_kernel,
        grid=(bsz, seq // tm),
        in_specs=[pl.BlockSpec((None, tm, A_WIDTH), lambda b, i: (b, i, A_COL // A_WIDTH)), tab, tab]
        + [_layer_block(t, layer) for t in params],
        out_specs=[out, out, pl.BlockSpec((None, MLA_W, tm), lambda b, i: (b, 0, i))],
        out_shape=[shp, shp, jax.ShapeDtypeStruct((bsz, MLA_W, seq), BF16)],
        compiler_params=_cparams(("parallel", "parallel")),
        name="mla_prep",
    )(z3, cos_t, sin_t, *params)


def _mla_flash_kernel(q_ref, k_ref, vt_ref, o_ref, m_scr, acc_scr, s_scr, p_scr):
    ki = pl.program_id(2)

    @pl.when(ki == 0)
    def _():
        m_scr[...] = jnp.full(m_scr.shape, -jnp.inf, F32)
        acc_scr[...] = jnp.zeros(acc_scr.shape, F32)

    heads = [slice(h * MLA_HEAD_PAD, (h + 1) * MLA_HEAD_PAD) for h in range(MLA_HEADS)]
    for h, sl in enumerate(heads):
        s_scr[h] = lax.dot_general(k_ref[:, sl], q_ref[:, sl], (((1,), (1,)), ((), ())),
                                   preferred_element_type=F32)
    alphas = []
    for h in range(MLA_HEADS):
        s = s_scr[h]
        m_prev = m_scr[h]
        m_new = jnp.maximum(m_prev, jnp.max(s, axis=0, keepdims=True))
        p_scr[h] = jnp.exp2(s - m_new).astype(BF16)
        alphas.append(jnp.exp2(m_prev - m_new))
        m_scr[h] = m_new
    for h, sl in enumerate(heads):
        acc_scr[h] = alphas[h] * acc_scr[h] + jnp.dot(vt_ref[sl, :], p_scr[h], preferred_element_type=F32)

    @pl.when(ki == pl.num_programs(2) - 1)
    def _():
        for h, sl in enumerate(heads):
            acc = acc_scr[h]
            o_ref[:, sl] = (acc / acc[MLA_V:MLA_V + 1, :]).T.astype(o_ref.dtype)


def _mla_flash(q, k, vt, tq=1024, tk=1024):
    bsz, seq, _ = q.shape
    tq, tk = min(tq, seq), min(tk, seq)
    return pl.pallas_call(
        _mla_flash_kernel,
        grid=(bsz, seq // tq, seq // tk),
        in_specs=[pl.BlockSpec((None, tq, MLA_W), lambda b, i, j: (b, i, 0)),
                  pl.BlockSpec((None, tk, MLA_W), lambda b, i, j: (b, j, 0)),
                  pl.BlockSpec((None, MLA_W, tk), lambda b, i, j: (b, 0, j))],
        out_specs=pl.BlockSpec((None, tq, MLA_W), lambda b, i, j: (b, i, 0)),
        out_shape=jax.ShapeDtypeStruct((bsz, seq, MLA_W), BF16),
        scratch_shapes=[pltpu.VMEM((MLA_HEADS, 1, tq), F32),
                        pltpu.VMEM((MLA_HEADS, MLA_HEAD_PAD, tq), F32),
                        pltpu.VMEM((MLA_HEADS, tk, tq), F32),
                        pltpu.VMEM((MLA_HEADS, tk, tq), BF16)],
        compiler_params=_cparams(("parallel", "parallel", "arbitrary")),
        name="mla_flash",
    )(q, k, vt)


def _t5_bucket(rel):
    nb = T5_BUCKETS // 2
    max_exact = nb // 2
    n = jnp.abs(rel)
    large = max_exact + (jnp.log(jnp.maximum(n, 1).astype(F32) / max_exact)
                         / math.log(T5_MAX_DIST / max_exact) * (nb - max_exact)).astype(jnp.int32)
    large = jnp.minimum(large, nb - 1)
    return jnp.where(rel > 0, nb, 0) + jnp.where(n < max_exact, n, large)


def _band_bias_tables(rel_bias, head_lo, tile, half, dilation):
    width = tile + 2 * half
    col = jnp.arange(width)[None, :]
    rel = col - half - jnp.arange(tile)[:, None]
    onehot = jax.nn.one_hot(_t5_bucket(rel * dilation), T5_BUCKETS, dtype=F32)
    bias = jnp.einsum("qkn,nh->hqk", onehot, rel_bias[:, head_lo:head_lo + HEADS].astype(F32),
                      precision=lax.Precision.HIGHEST)
    band = jnp.abs(rel) <= half
    left_ok = col >= half
    right_ok = col < tile + half
    masks = jnp.stack([band, band & left_ok, band & right_ok, band & left_ok & right_ok])
    return jnp.where(masks[:, None], LOG2E * bias[None].astype(F32), NEG_INF)


def _head_of_lane():
    return lax.broadcasted_iota(jnp.int32, (1, HEAD_BLOCK), 1) // HEAD_DIM


def _per_head_lanes(cols):
    head = _head_of_lane()
    out = cols[HEADS - 1]
    for h in range(HEADS - 2, -1, -1):
        out = jnp.where(head == h, cols[h], out)
    return out


def _packed_heads_attention(q, kwin, vwin, bias, sink_rows):
    t = q.shape[0]
    head = _head_of_lane()
    qs = jnp.concatenate([jnp.where(head == h, q, 0) for h in range(HEADS)], axis=0)
    s = lax.dot_general(qs, kwin, (((1,), (1,)), ((), ())), preferred_element_type=F32)
    s = s + bias.reshape(HEADS * t, bias.shape[-1])
    if sink_rows is not None:
        m = jnp.max(jnp.concatenate([jnp.maximum(s[:, :LANES], sink_rows), s[:, LANES:]], axis=1),
                    axis=1, keepdims=True)
    else:
        m = jnp.max(s, axis=1, keepdims=True)
    e = jnp.exp2(s - m)
    if sink_rows is not None:
        lane = lax.broadcasted_iota(jnp.int32, (1, LANES), 1)
        e_sink = jnp.where(lane == 0, jnp.exp2(sink_rows - m), 0.0)
        den = jnp.sum(jnp.concatenate([e[:, :LANES] + e_sink, e[:, LANES:]], axis=1), axis=1, keepdims=True)
    else:
        den = jnp.sum(e, axis=1, keepdims=True)
    o = jnp.dot(e.astype(BF16), vwin, preferred_element_type=F32) * (1.0 / den)
    out = o[(HEADS - 1) * t:]
    for h in range(HEADS - 2, -1, -1):
        out = jnp.where(head == h, o[h * t:(h + 1) * t], out)
    return out, m, den


def _banded_kernel(*refs, tile, subtiles, half, length, use_sink, want_lse):
    refs = list(refs)
    sink_ref = refs.pop(0) if use_sink else None
    q_ref, k_ref, v_ref, bias_ref, o_ref = refs[:5]
    lse_ref = refs[5] if want_lse else None
    i = pl.program_id(2)
    nt = length // tile
    width = tile + 2 * half
    head = _head_of_lane()
    for u in range(subtiles):
        t = i * subtiles + u
        main = t * tile
        can_first, can_last = u == 0, u == subtiles - 1
        if can_first or can_last:
            left = jnp.maximum(main - half, 0) if can_first else main - half
            right = jnp.minimum(main + tile, length - half) if can_last else main + tile
            left, right = pl.multiple_of(left, half), pl.multiple_of(right, half)
            main = pl.multiple_of(main, tile)

            def window(ref):
                return jnp.concatenate([ref[pl.ds(left, half), :], ref[pl.ds(main, tile), :],
                                        ref[pl.ds(right, half), :]], axis=0)

            variant = 0
            if can_first:
                variant = variant + (t == 0).astype(jnp.int32)
            if can_last:
                variant = variant + 2 * (t == nt - 1).astype(jnp.int32)
        else:
            start = pl.multiple_of(main - half, half)

            def window(ref):
                return ref[pl.ds(start, width), :]

            variant = 0
        rows = slice(u * tile, (u + 1) * tile)
        kwin, vwin = window(k_ref), window(v_ref)
        q = q_ref[rows, :]
        sink_rows = sink_ref[...] if use_sink else None
        out, m, den = _packed_heads_attention(q, kwin, vwin, bias_ref[variant], sink_rows)
        o_ref[rows, :] = out.astype(o_ref.dtype)
        if want_lse:
            lse = m + jnp.log2(den)
            lse_ref[rows, :] = _per_head_lanes([lse[h * tile:(h + 1) * tile] for h in range(HEADS)])


def _sink_rows(sink, tile):
    rows = jnp.repeat(LOG2E * sink.astype(F32), tile, axis=1)
    return jnp.broadcast_to(rows[:, :, None], rows.shape + (LANES,))


def _banded_attention(zf, blk, bias, sink_rows, layer, *, half, subtiles, want_lse):
    bsz, dilation, length, _ = zf.shape
    tile = bias.shape[2]
    subtiles = min(subtiles, length // tile)
    step = tile * subtiles
    use_sink = sink_rows is not None
    in_specs = [pl.BlockSpec((None, None, step, HEAD_BLOCK), lambda b, j, i: (b, j, i, blk)),
                pl.BlockSpec((None, None, length, HEAD_BLOCK), lambda b, j, i: (b, j, 0, blk + 1)),
                pl.BlockSpec((None, None, length, HEAD_BLOCK), lambda b, j, i: (b, j, 0, blk + 2)),
                pl.BlockSpec(bias.shape, lambda b, j, i: (0, 0, 0, 0))]
    args = [zf, zf, zf, bias]
    if use_sink:
        in_specs.insert(0, _layer_block(sink_rows, layer))
        args.insert(0, sink_rows)
    o_spec = pl.BlockSpec((None, None, step, HEAD_BLOCK), lambda b, j, i: (b, j, i, 0))
    out_specs = [o_spec]
    out_shape = [jax.ShapeDtypeStruct((bsz, dilation, length, HEAD_BLOCK), BF16)]
    if want_lse:
        out_specs.append(o_spec)
        out_shape.append(jax.ShapeDtypeStruct((bsz, dilation, length, HEAD_BLOCK), F32))
    return pl.pallas_call(
        functools.partial(_banded_kernel, tile=tile, subtiles=subtiles, half=half, length=length,
                          use_sink=use_sink, want_lse=want_lse),
        grid=(bsz, dilation, length // step),
        in_specs=in_specs,
        out_specs=out_specs,
        out_shape=out_shape,
        compiler_params=_cparams(("parallel", "parallel", "arbitrary")),
        name=f"banded_r{dilation}",
    )(*args)


NA_TILE_ROWS = 4
NA_WIN_ROWS = NA_TILE_ROWS + NA_KH
NA_SUBTILES = 8


def _na_bias_tables(rpb, rows):
    nl = rpb.shape[0]
    tr, wr = NA_TILE_ROWS, NA_WIN_ROWS
    exact = functools.partial(jnp.einsum, precision=lax.Precision.HIGHEST)
    c = np.arange(GRID_W)
    cs = np.clip(c - NA_KW // 2, 0, GRID_W - NA_KW)
    col_ok = (c[None, :] >= cs[:, None]) & (c[None, :] < cs[:, None] + NA_KW)
    dc = np.clip(c[None, :] - c[:, None], -(NA_KW - 1), NA_KW - 1) + (NA_KW - 1)
    by_col = exact("lhdn,qkn->lhdqk", rpb.astype(F32), np.eye(2 * NA_KW - 1, dtype=np.float32)[dc])
    tabs = []
    for r0, ws in ((0, 0), (tr, 0), (rows - tr, rows - wr)):
        r = r0 + np.arange(tr)
        start = np.clip(r - NA_KH // 2, 0, rows - NA_KH)
        kr = ws + np.arange(wr)
        row_ok = (kr[None, :] >= start[:, None]) & (kr[None, :] < start[:, None] + NA_KH)
        dr = np.clip(kr[None, :] - r[:, None] + (NA_KH - 1), 0, 2 * NA_KH - 2)
        bias = exact("rad,lhdqk->lhrqak", np.eye(2 * NA_KH - 1, dtype=np.float32)[dr], by_col)
        ok = row_ok[:, None, :, None] & col_ok[None, :, None, :]
        tabs.append(jnp.where(ok, LOG2E * bias, NEG_INF).reshape(nl, HEADS, tr * GRID_W, wr * GRID_W))
    return jnp.stack(tabs, axis=1)


def _na_kernel(q_ref, k_ref, v_ref, bias_ref, o_ref, *, rows):
    i = pl.program_id(1)
    nt = rows // NA_TILE_ROWS
    win = NA_WIN_ROWS * GRID_W
    tq = NA_TILE_ROWS * GRID_W
    head = _head_of_lane()
    for u in range(NA_SUBTILES):
        t = i * NA_SUBTILES + u
        ws = jnp.clip(t * NA_TILE_ROWS - NA_KH // 2, 0, rows - NA_WIN_ROWS) * GRID_W
        ws = pl.multiple_of(ws, GRID_W)
        variant = jnp.where(t == 0, 0, jnp.where(t == nt - 1, 2, 1))
        kwin = k_ref[pl.ds(ws, win), :]
        vwin = v_ref[pl.ds(ws, win), :]
        q = q_ref[u * tq:(u + 1) * tq, :]
        acc = None
        inv_den = []
        for h in range(HEADS):
            mine = head == h
            s = lax.dot_general(q, jnp.where(mine, kwin, 0), (((1,), (1,)), ((), ())),
                                preferred_element_type=F32)
            s = s + bias_ref[variant, h]
            m = jnp.max(s, axis=1, keepdims=True)
            e = jnp.exp2(s - m)
            inv_den.append(1.0 / jnp.sum(e, axis=1, keepdims=True))
            part = jnp.dot(e.astype(BF16), jnp.where(mine, vwin, 0), preferred_element_type=F32)
            acc = part if acc is None else acc + part
        o_ref[u * tq:(u + 1) * tq, :] = (acc * _per_head_lanes(inv_den)).astype(o_ref.dtype)


def _neighborhood_attention(z3, bias, layer):
    bsz, seq, _ = z3.shape
    rows = seq // GRID_W
    tq = NA_TILE_ROWS * GRID_W * NA_SUBTILES
    return pl.pallas_call(
        functools.partial(_na_kernel, rows=rows),
        grid=(bsz, rows // (NA_TILE_ROWS * NA_SUBTILES)),
        in_specs=[pl.BlockSpec((None, tq, HEAD_BLOCK), lambda b, i: (b, i, C_BLK)),
                  pl.BlockSpec((None, seq, HEAD_BLOCK), lambda b, i: (b, 0, C_BLK + 1)),
                  pl.BlockSpec((None, seq, HEAD_BLOCK), lambda b, i: (b, 0, C_BLK + 2)),
                  _layer_block(bias, layer)],
        out_specs=pl.BlockSpec((None, tq, HEAD_BLOCK), lambda b, i: (b, i, 0)),
        out_shape=jax.ShapeDtypeStruct((bsz, seq, HEAD_BLOCK), BF16),
        compiler_params=_cparams(("parallel", "arbitrary")),
        name="neighborhood",
    )(z3, z3, z3, bias)


def _unfold(ref, scr):
    dilation, n, cols = ref.shape
    if dilation == 1:
        return ref[0].astype(F32)
    nblk = cols // LANES
    for j in range(dilation):
        for c in range(nblk):
            scr[c, pl.ds(j, n, stride=dilation), :] = ref[j, :, c * LANES:(c + 1) * LANES].astype(F32)
    return jnp.concatenate([scr[c] for c in range(nblk)], axis=1)


def _merge_rows(gate_ref, ya_ref, yb_ref, yc_ref, o0_ref, o1_ref, o2_ref, l0_ref, l1_ref, l2_ref, h_ref,
                eg_ref, eb_ref, wa_ref, wb_ref, wo_ref, g_ref, b_ref, scr, first_layer):
    dot = functools.partial(jnp.dot, preferred_element_type=F32)
    l0, l1, l2 = _unfold(l0_ref, None), _unfold(l1_ref, scr[0]), _unfold(l2_ref, scr[1])
    o0, o1, o2 = _unfold(o0_ref, None), _unfold(o1_ref, scr[2]), _unfold(o2_ref, scr[3])
    lm = jnp.maximum(jnp.maximum(l0, l1), l2)
    e0, e1, e2 = jnp.exp2(l0 - lm), jnp.exp2(l1 - lm), jnp.exp2(l2 - lm)
    yd = (e0 * o0 + e1 * o1 + e2 * o2) / (e0 + e1 + e2)
    branches = (dot(ya_ref[...], wa_ref[...]), dot(yb_ref[0], wb_ref[0]), dot(yc_ref[...], wb_ref[1]),
                dot(yd.astype(BF16), wb_ref[2]))
    merged = None
    for n in range(N_BRANCH):
        gate2 = 1.0 + jnp.tanh(gate_ref[:, n * D_MODEL:(n + 1) * D_MODEL].astype(F32))
        term = gate2 * branches[n]
        merged = term if merged is None else merged + term
    h = h_ref[...]
    if first_layer:
        h = _layer_norm(h, eg_ref[...], eb_ref[...])
    u = ALPHA * h + dot(merged.astype(BF16), wo_ref[...])
    return _layer_norm(u, g_ref[...], b_ref[...])


FF_CHUNK = 1024
N_MERGE_REFS = 18
TAIL_SUBTILES = 2


def _ffn_rows(h, p_ref, w1_ref, w2_ref, wp_ref, wg_ref, g_ref, b_ref):
    dot = functools.partial(jnp.dot, preferred_element_type=F32)
    hb = h.astype(BF16)
    ff = None
    for c in range(D_FF // FF_CHUNK):
        sl = slice(c * FF_CHUNK, (c + 1) * FF_CHUNK)
        a = jnp.maximum(dot(hb, w1_ref[:, sl]), 0.0)
        part = dot((a * a).astype(BF16), w2_ref[sl, :])
        ff = part if ff is None else ff + part
    ple = dot(p_ref[...].astype(BF16), wp_ref[...]) * (1.0 + jnp.tanh(dot(hb, wg_ref[...])))
    return _layer_norm(ALPHA * h + ff + ple, g_ref[...], b_ref[...])


def _layer_tail_kernel(*refs, first_layer):
    merge_refs, rest = refs[:N_MERGE_REFS], refs[N_MERGE_REFS:]
    ffn_refs, (h_out_ref, hb_out_ref), scr = rest[:7], rest[7:9], rest[9:]
    tm = h_out_ref.shape[0] // TAIL_SUBTILES

    def rows(ref, u):
        return ref.at[pl.ds(u * tm, tm), :]

    def fold(ref, u):
        n = ref.shape[1] // TAIL_SUBTILES
        return ref.at[:, pl.ds(u * n, n), :]

    h1 = []
    for u in range(TAIL_SUBTILES):
        gate, ya, yb, yc, o0, o1, o2, l0, l1, l2, h = merge_refs[:11]
        views = (rows(gate, u), rows(ya, u), fold(yb, u), rows(yc, u), fold(o0, u), fold(o1, u), fold(o2, u),
                 fold(l0, u), fold(l1, u), fold(l2, u), rows(h, u))
        sub_scr = tuple(s.at[:, pl.ds(u * tm, tm), :] for s in scr)
        h1.append(_merge_rows(*views, *merge_refs[11:], sub_scr, first_layer))
    for u in range(TAIL_SUBTILES):
        y = _ffn_rows(h1[u], rows(ffn_refs[0], u), *ffn_refs[1:])
        h_out_ref[pl.ds(u * tm, tm), :] = y
        hb_out_ref[pl.ds(u * tm, tm), :] = y.astype(BF16)


def _layer_tail(z3, ya, yb, yc, od, ld, h3, embed_params, p4, merge_params, ffn_params, layer, tm=512):
    bsz, seq, _ = h3.shape

    def rows(width):
        return pl.BlockSpec((None, tm, width), lambda b, i: (b, i, 0))

    def folded(t):
        dilation = t.shape[1]
        return pl.BlockSpec((None, dilation, tm // dilation, HEAD_BLOCK), lambda b, i: (b, 0, i, 0))

    def weights(t):
        zeros = (0,) * (t.ndim - 1)
        return pl.BlockSpec((None,) + t.shape[1:], lambda b, i: (layer,) + zeros, pipeline_mode=pl.Buffered(1))

    in_specs = ([rows(GATE_COLS), rows(MLA_W), folded(yb), rows(HEAD_BLOCK)]
                + [folded(t) for t in od] + [folded(t) for t in ld] + [rows(D_MODEL)]
                + [pl.BlockSpec(t.shape, lambda b, i: (0, 0)) for t in embed_params]
                + [weights(t) for t in merge_params]
                + [pl.BlockSpec((None, None, tm, PLE_DIM), lambda b, i: (layer, b, i, 0))]
                + [weights(t) for t in ffn_params])
    assert len(in_specs) == N_MERGE_REFS + 7
    return pl.pallas_call(
        functools.partial(_layer_tail_kernel, first_layer=layer == 0),
        grid=(bsz, seq // tm),
        in_specs=in_specs,
        out_specs=[rows(D_MODEL), rows(D_MODEL)],
        out_shape=[jax.ShapeDtypeStruct((bsz, seq, D_MODEL), F32), jax.ShapeDtypeStruct((bsz, seq, D_MODEL), BF16)],
        scratch_shapes=[pltpu.VMEM((HEAD_BLOCK // LANES, tm, LANES), F32)] * 4,
        compiler_params=_cparams(("parallel", "parallel")),
        name="layer_tail",
    )(z3, ya, yb, yc, *od, *ld, h3, *embed_params, *merge_params, p4, *ffn_params)


SWA_TILE, SWA_SUBTILES = 256, 4
DIL_TILE, DIL_SUBTILES = 128, 16


def kernel(x, p, ln_emb_g, ln_emb_b, rel_bias, w_in, mla_q_norm, mla_w_uq, mla_kv_norm, mla_w_ukv, swa_sink,
           na_rpb, w_branch, w_out, ln1_g, ln1_b, w_ff1, w_ff2, w_ple, w_ple_gate, ln2_g, ln2_b):
    bsz, seq, _ = x.shape
    m = bsz * seq
    rows = seq // GRID_W
    cos_t, sin_t = _rope_tables(seq)
    swa_bias = _band_bias_tables(rel_bias, 0, min(SWA_TILE, seq), SWA_HALF, 1)
    dil_bias = []
    for g, (window, r) in enumerate(DIL_CONFIGS):
        half = window // (2 * r)
        dil_bias.append(_band_bias_tables(rel_bias, SWA_HEADS + g * DIL_HEADS, min(DIL_TILE, seq // r), half, r))

    nl = w_in.shape[0]
    w_main, w_dil = _prep_w_in(w_in)
    mla_weights = _prep_mla_weights(mla_w_uq, mla_w_ukv)
    g_q, g_kv = mla_q_norm.reshape(nl, 1, Q_LORA), mla_kv_norm.reshape(nl, 1, KV_LORA)
    swa_sinks = _sink_rows(swa_sink, swa_bias.shape[2])
    na_bias = _na_bias_tables(na_rpb, rows)
    wa = jnp.pad(w_branch[:, 0].reshape(nl, MLA_HEADS, MLA_V, D_MODEL),
                 ((0, 0), (0, 0), (0, MLA_HEAD_PAD - MLA_V), (0, 0))).reshape(nl, MLA_W, D_MODEL).astype(BF16)
    merge_params = (wa, w_branch[:, 1:].astype(BF16), (0.5 * w_out).astype(BF16),
                    ln1_g.reshape(nl, 1, D_MODEL), ln1_b.reshape(nl, 1, D_MODEL))
    ffn_params = (w_ff1.astype(BF16), w_ff2.astype(BF16), (0.5 * w_ple).astype(BF16),
                  (0.5 * w_ple_gate).astype(BF16), ln2_g.reshape(nl, 1, D_MODEL), ln2_b.reshape(nl, 1, D_MODEL))

    embed_params = (ln_emb_g.reshape(1, D_MODEL), ln_emb_b.reshape(1, D_MODEL))
    hb = _embed_layer_norm(x.reshape(m, D_MODEL), *embed_params)
    h = x
    for i in range(DEPTH):
        z3 = _matmul(hb, w_main, i, 1024, 2304, BF16, "in_proj").reshape(bsz, seq, Z_COLS)
        z4 = z3.reshape(bsz, 1, seq, Z_COLS)
        hb3 = hb.reshape(bsz, seq, D_MODEL)
        q_a, k_a, vt_a = _mla_prep(z3, cos_t, sin_t, g_q, g_kv, mla_weights, i)
        y_a = _mla_flash(q_a, k_a, vt_a)
        (y_b,) = _banded_attention(z4, B_BLK, swa_bias, swa_sinks, i,
                                   half=SWA_HALF, subtiles=SWA_SUBTILES, want_lse=False)
        y_c = _neighborhood_attention(z3, na_bias, i)
        o_d, l_d = [], []
        for g, (window, r) in enumerate(DIL_CONFIGS):
            if r == 1:
                zd, blk = z4, D1_BLK
            else:
                zd, blk = _matmul_fold(hb3, w_dil, i, g - 1, r, 1024, f"in_proj_r{r}"), 0
            o_g, l_g = _banded_attention(zd, blk, dil_bias[g], None, i, half=window // (2 * r),
                                         subtiles=DIL_SUBTILES, want_lse=True)
            o_d.append(o_g)
            l_d.append(l_g)
        h, hb = _layer_tail(z3, y_a, y_b, y_c, o_d, l_d, h, embed_params, p, merge_params, ffn_params, i)
        hb = hb.reshape(m, D_MODEL)
    return h
```
